```python
import math
import jax
import jax.numpy as jnp
from jax import lax
import numpy as np

D_MODEL = 2048
BATCH = 4
SEQ = 4096
DEPTH = 4

N_MIXERS = 3
NORM_EPS = 1e-6
MEM_TOKENS = 256
MEM_HEADS = 4
MEM_HEAD_DIM = D_MODEL // 16
MEM_WIDTH = MEM_HEADS * MEM_HEAD_DIM
MIX_WIDTH = D_MODEL - MEM_WIDTH
RET_HEADS = 6
RET_HEAD_DIM = MIX_WIDTH // RET_HEADS
RET_CHUNK = 128
CONV_WIDTH = 3
MOBA_HEADS = 12
MOBA_HEAD_DIM = MIX_WIDTH // MOBA_HEADS
MOBA_BLOCK = 256
MOBA_TOPK = 3
MOBA_Q_CHUNK = 16
D_FF = ((8 * D_MODEL // 3 + 255) // 256) * 256
MIXER_COLS = (4 * MIX_WIDTH, 3 * MIX_WIDTH, 3 * MIX_WIDTH)

kernel_name = 'hybrid_retention_shortconv_moba_trunk'


def rms_norm(x, gain):
    xf = x.astype(jnp.float32)
    y = xf * lax.rsqrt(jnp.mean(xf * xf, axis=-1, keepdims=True) + NORM_EPS)
    return (y * gain.astype(jnp.float32)).astype(x.dtype)


def causal_dwconv3(x, w):
    s = x.shape[1]
    xp = jnp.pad(x, ((0, 0), (CONV_WIDTH - 1, 0), (0, 0)))
    return w[0] * xp[:, :s] + w[1] * xp[:, 1:s + 1] + w[2] * xp[:, 2:s + 2]


def split_heads(t, n_heads):
    b, s, _ = t.shape
    return t.reshape(b, s, n_heads, -1).transpose(0, 2, 1, 3)


def merge_heads(t):
    b, h, s, d = t.shape
    return t.transpose(0, 2, 1, 3).reshape(b, s, h * d)


def alibi_slopes(n):
    def pow2_slopes(m):
        start = 2.0 ** (-8.0 / m)
        return [start ** (i + 1) for i in range(m)]
    if math.log2(n).is_integer():
        s = pow2_slopes(n)
    else:
        c = 2 ** int(math.floor(math.log2(n)))
        s = pow2_slopes(c) + list(alibi_slopes(2 * c))[0::2][: n - c]
    return np.asarray(s, dtype=np.float32)


def chunkwise_retention(q, k, v):
    b, h, s, d = q.shape
    c = RET_CHUNK
    n = s // c
    dt = q.dtype
    lg = jnp.log1p(-jnp.exp2(-5.0 - jnp.arange(h, dtype=jnp.float32)))
    i = jnp.arange(c, dtype=jnp.float32)
    diff = i[:, None] - i[None, :]
    intra = jnp.where(diff >= 0, jnp.exp(jnp.maximum(diff, 0.0)[None] * lg[:, None, None]), 0.0)
    q_decay = jnp.exp((i + 1.0)[None] * lg[:, None])
    k_decay = jnp.exp((c - 1.0 - i)[None] * lg[:, None])
    chunk_decay = jnp.exp(c * lg).astype(dt)
    qc = q.reshape(b, h, n, c, d)
    kc = k.reshape(b, h, n, c, d)
    vc = v.reshape(b, h, n, c, d)
    scores = jnp.einsum('bhncd,bhnmd->bhncm', qc, kc) * intra[:, None].astype(dt)
    y_intra = jnp.einsum('bhncm,bhnme->bhnce', scores, vc)
    kv = jnp.einsum('bhnmd,bhnme->nbhde', kc * k_decay[:, None, :, None].astype(dt), vc)

    def step(state, kv_n):
        return chunk_decay[:, None, None] * state + kv_n, state

    _, s_prev = lax.scan(step, jnp.zeros((b, h, d, d), dt), kv)
    y_cross = jnp.einsum('bhncd,nbhde->bhnce', qc * q_decay[:, None, :, None].astype(dt), s_prev)
    return (y_intra + y_cross).reshape(b, h, s, d)


def retention_mixer(body, gn_gain):
    q, k, v, g = jnp.split(body, 4, axis=-1)
    q, k, v = (split_heads(t, RET_HEADS) for t in (q, k, v))
    y = chunkwise_retention(q, k * RET_HEAD_DIM ** -0.5, v)
    yf = y.astype(jnp.float32)
    yf = yf * lax.rsqrt(jnp.mean(yf * yf, axis=-1, keepdims=True) + NORM_EPS)
    y = (merge_heads(yf) * gn_gain.astype(jnp.float32)).astype(body.dtype)
    return jax.nn.silu(g) * y


def short_conv_mixer(body, conv_w):
    gate_b, gate_c, h = jnp.split(body, 3, axis=-1)
    return gate_b * causal_dwconv3(gate_c * h, conv_w)


def moba_attention(q, k, v, slopes):
    b, h, s, d = q.shape
    n_blk = -(-s // MOBA_BLOCK)
    s_pad = n_blk * MOBA_BLOCK
    pad = ((0, 0), (0, 0), (0, s_pad - s), (0, 0))
    kb = jnp.pad(k, pad).reshape(b, h, n_blk, MOBA_BLOCK, d)
    vb = jnp.pad(v, pad).reshape(b, h, n_blk, MOBA_BLOCK, d)
    k_mean = jnp.mean(kb, axis=3)
    t = jnp.arange(s)
    own = t // MOBA_BLOCK
    gate = jnp.einsum('bhsd,bhnd->bhsn', q, k_mean).astype(jnp.float32)
    past = jnp.arange(n_blk)[None, :] < own[:, None]
    gate = jnp.where(past, gate, -jnp.inf)
    k_eff = min(MOBA_TOPK, n_blk)
    _, top = lax.top_k(gate, k_eff)
    idx = jnp.concatenate([top, jnp.broadcast_to(own.astype(top.dtype)[None, None, :, None], (b, h, s, 1))], axis=-1)
    valid = jnp.concatenate([jnp.arange(k_eff)[None, :] < own[:, None], jnp.ones((s, 1), bool)], axis=-1)
    k1 = k_eff + 1
    n_q = s // MOBA_Q_CHUNK
    q_ch = jnp.moveaxis(q.reshape(b, h, n_q, MOBA_Q_CHUNK, d), 2, 0)
    i_ch = jnp.moveaxis(idx.reshape(b, h, n_q, MOBA_Q_CHUNK, k1), 2, 0)
    t_ch = t.reshape(n_q, MOBA_Q_CHUNK)
    v_ch = valid.reshape(n_q, MOBA_Q_CHUNK, k1)
    offs = jnp.arange(MOBA_BLOCK)
    scale = d ** -0.5
    gather = jax.vmap(jax.vmap(lambda blocks, ids: blocks[ids]))

    def attend(args):
        qc, ic, tc, vmask = args
        kg = gather(kb, ic)
        vg = gather(vb, ic)
        logits = jnp.einsum('bhqd,bhqjsd->bhqjs', qc, kg).astype(jnp.float32) * scale
        key_pos = ic[..., None] * MOBA_BLOCK + offs
        dist = (tc[:, None, None] - key_pos).astype(jnp.float32)
        logits = logits - slopes[:, None, None, None] * dist
        mask = vmask[:, :, None] & (dist >= 0)
        logits = jnp.where(mask, logits, -jnp.inf)
        shp = logits.shape
        p = jax.nn.softmax(logits.reshape(shp[0], shp[1], shp[2], -1), axis=-1).reshape(shp).astype(qc.dtype)
        return jnp.einsum('bhqjs,bhqjsd->bhqd', p, vg)

    out = lax.map(attend, (q_ch, i_ch, t_ch, v_ch))
    return jnp.moveaxis(out, 0, 2).reshape(b, h, s, d)


def moba_mixer(body):
    q, k, v = jnp.split(body, 3, axis=-1)
    q, k, v = (split_heads(t, MOBA_HEADS) for t in (q, k, v))
    slopes = jnp.asarray(alibi_slopes(MOBA_HEADS))
    return merge_heads(moba_attention(q, k, v, slopes))


def memory_attention(q_mem, mem_kv):
    q = split_heads(q_mem, MEM_HEADS)
    k, v = jnp.split(mem_kv, 2, axis=-1)
    k = split_heads(k, MEM_HEADS)
    v = split_heads(v, MEM_HEADS)
    logits = jnp.einsum('bhsd,bhmd->bhsm', q, k).astype(jnp.float32) * MEM_HEAD_DIM ** -0.5
    p = jax.nn.softmax(logits, axis=-1).astype(q.dtype)
    return merge_heads(jnp.einsum('bhsm,bhmd->bhsd', p, v))


def conv_ffn(x, w_up, conv_w, conv_b, w_down):
    h = causal_dwconv3(x @ w_up, conv_w) + conv_b
    g, u = jnp.split(h, 2, axis=-1)
    return (jax.nn.silu(g) * u) @ w_down


def setup_inputs(seed: int = 0) -> dict:
    key = jax.random.key(seed)
    keys = iter(jax.random.split(key, 64))
    f32 = jnp.float32

    def dense(fan_in, fan_out):
        return jax.random.normal(next(keys), (fan_in, fan_out), f32) * fan_in ** -0.5

    def gain(n):
        return 1.0 + 0.02 * jax.random.normal(next(keys), (n,), f32)

    inputs = {
        'x': jax.random.normal(next(keys), (BATCH, SEQ, D_MODEL), f32),
        'mem': jax.random.normal(next(keys), (BATCH, MEM_TOKENS, D_MODEL), f32),
        'mem_norm': gain(D_MODEL),
    }
    for i in range(DEPTH):
        kind = i % N_MIXERS
        p = 'l%d_' % i
        inputs[p + 'norm_mix'] = gain(D_MODEL)
        inputs[p + 'w_in'] = dense(D_MODEL, MIXER_COLS[kind] + MEM_WIDTH)
        if kind == 0:
            inputs[p + 'ret_gn'] = gain(MIX_WIDTH)
        elif kind == 1:
            inputs[p + 'conv_w'] = jax.random.normal(next(keys), (CONV_WIDTH, MIX_WIDTH), f32) * CONV_WIDTH ** -0.5
        inputs[p + 'w_mem_kv'] = dense(D_MODEL, 2 * MEM_WIDTH)
        inputs[p + 'w_o'] = dense(D_MODEL, D_MODEL)
        inputs[p + 'norm_ffn'] = gain(D_MODEL)
        inputs[p + 'ffn_w_up'] = dense(D_MODEL, 2 * D_FF)
        inputs[p + 'ffn_conv_w'] = jax.random.normal(next(keys), (CONV_WIDTH, 2 * D_FF), f32) * CONV_WIDTH ** -0.5
        inputs[p + 'ffn_conv_b'] = 0.01 * jax.random.normal(next(keys), (2 * D_FF,), f32)
        inputs[p + 'ffn_w_down'] = dense(D_FF, D_MODEL)
    inputs['final_norm'] = gain(D_MODEL)
    return inputs


def reference(x, mem, mem_norm,
              l0_norm_mix, l0_w_in, l0_ret_gn, l0_w_mem_kv, l0_w_o, l0_norm_ffn, l0_ffn_w_up, l0_ffn_conv_w, l0_ffn_conv_b, l0_ffn_w_down,
              l1_norm_mix, l1_w_in, l1_conv_w, l1_w_mem_kv, l1_w_o, l1_norm_ffn, l1_ffn_w_up, l1_ffn_conv_w, l1_ffn_conv_b, l1_ffn_w_down,
              l2_norm_mix, l2_w_in, l2_w_mem_kv, l2_w_o, l2_norm_ffn, l2_ffn_w_up, l2_ffn_conv_w, l2_ffn_conv_b, l2_ffn_w_down,
              l3_norm_mix, l3_w_in, l3_ret_gn, l3_w_mem_kv, l3_w_o, l3_norm_ffn, l3_ffn_w_up, l3_ffn_conv_w, l3_ffn_conv_b, l3_ffn_w_down,
              final_norm):
    layers = [
        (l0_norm_mix, l0_w_in, l0_ret_gn, l0_w_mem_kv, l0_w_o, l0_norm_ffn, l0_ffn_w_up, l0_ffn_conv_w, l0_ffn_conv_b, l0_ffn_w_down),
        (l1_norm_mix, l1_w_in, l1_conv_w, l1_w_mem_kv, l1_w_o, l1_norm_ffn, l1_ffn_w_up, l1_ffn_conv_w, l1_ffn_conv_b, l1_ffn_w_down),
        (l2_norm_mix, l2_w_in, None, l2_w_mem_kv, l2_w_o, l2_norm_ffn, l2_ffn_w_up, l2_ffn_conv_w, l2_ffn_conv_b, l2_ffn_w_down),
        (l3_norm_mix, l3_w_in, l3_ret_gn, l3_w_mem_kv, l3_w_o, l3_norm_ffn, l3_ffn_w_up, l3_ffn_conv_w, l3_ffn_conv_b, l3_ffn_w_down),
    ]
    mem_n = rms_norm(mem, mem_norm)
    for i in range(DEPTH):
        norm_mix, w_in, extra, w_mem_kv, w_o, norm_ffn, w_up, cw, cb, w_down = layers[i]
        kind = i % N_MIXERS
        h = rms_norm(x, norm_mix)
        proj = h @ w_in
        body, q_mem = proj[..., :-MEM_WIDTH], proj[..., -MEM_WIDTH:]
        if kind == 0:
            tok = retention_mixer(body, extra)
        elif kind == 1:
            tok = short_conv_mixer(body, extra)
        else:
            tok = moba_mixer(body)
        mem_out = memory_attention(q_mem, mem_n @ w_mem_kv)
        x = x + jnp.concatenate([tok, mem_out], axis=-1) @ w_o
        x = x + conv_ffn(rms_norm(x, norm_ffn), w_up, cw, cb, w_down)
    return rms_norm(x, final_norm)
```

```python
import functools
import math

import jax
import jax.numpy as jnp
import numpy as np
from jax import lax
from jax.experimental import pallas as pl
from jax.experimental.pallas import tpu as pltpu

F32 = jnp.float32
BF16 = jnp.bfloat16

NORM_EPS = 1e-6
MEM_HEADS = 4
RET_HEADS = 6
RET_CHUNK = 256
CONV_WIDTH = 3
MOBA_HEADS = 12
MOBA_BLOCK = 256
MOBA_TOPK = 3

SUBLANES = 8
VMEM_LIMIT_BYTES = 56 * 1024 * 1024

_NT_DIMS = (((1,), (1,)), ((), ()))
_TN_DIMS = (((0,), (0,)), ((), ()))


def _params(*semantics):
    return pltpu.CompilerParams(dimension_semantics=semantics,
                                vmem_limit_bytes=VMEM_LIMIT_BYTES)


def _rms_norm(x, gain):
    return x * lax.rsqrt(jnp.mean(x * x, axis=-1, keepdims=True) + NORM_EPS) * gain


def _silu(g):
    return g * (1.0 / (1.0 + jnp.exp(-g)))


def _shift_rows(u, prev, k):
    rolled = pltpu.roll(u, k, axis=0)
    prev_rolled = pltpu.roll(prev, k, axis=0)
    row = lax.broadcasted_iota(jnp.int32, prev.shape, 0)
    head = jnp.where(row < k, prev_rolled, rolled[:SUBLANES])
    return jnp.concatenate([head, rolled[SUBLANES:]], axis=0)


def _causal_conv3(u, prev, w):
    return (w[0:1] * _shift_rows(u, prev, 2) + w[1:2] * _shift_rows(u, prev, 1)
            + w[2:3] * u)


def _norm_matmul_kernel(x_ref, g_ref, w_ref, o_ref):
    h = _rms_norm(x_ref[...], g_ref[...]).astype(BF16)
    o_ref[...] = jnp.dot(h, w_ref[...], preferred_element_type=F32).astype(o_ref.dtype)


def _norm_matmul(x, gain, w, *, tm, tn):
    m, d = x.shape
    n = w.shape[1]
    return pl.pallas_call(
        _norm_matmul_kernel,
        grid=(n // tn, m // tm),
        in_specs=[
            pl.BlockSpec((tm, d), lambda j, i: (i, 0)),
            pl.BlockSpec((1, d), lambda j, i: (0, 0)),
            pl.BlockSpec((d, tn), lambda j, i: (0, j)),
        ],
        out_specs=pl.BlockSpec((tm, tn), lambda j, i: (i, j)),
        out_shape=jax.ShapeDtypeStruct((m, n), BF16),
        compiler_params=_params("parallel", "parallel"),
        name="norm_matmul",
    )(x, gain.reshape(1, d), w)


def _retention_kernel(lg_ref, q_ref, k_ref, v_ref, g_ref, gn_ref, o_ref, state_ref, *, chunk):
    lg = lg_ref[pl.program_id(1)]
    rows, d = q_ref.shape
    c = chunk
    k_scale = d ** -0.5

    @pl.when(pl.program_id(2) == 0)
    def _():
        state_ref[...] = jnp.zeros_like(state_ref)

    diff = (lax.broadcasted_iota(jnp.int32, (c, c), 0)
            - lax.broadcasted_iota(jnp.int32, (c, c), 1)).astype(F32)
    intra = jnp.where(diff >= 0, jnp.exp(jnp.maximum(diff, 0.0) * lg), 0.0) * k_scale
    pos = lax.broadcasted_iota(jnp.int32, (c, d), 0).astype(F32)
    q_decay = jnp.exp((pos + 1.0) * lg)
    k_decay = jnp.exp((c - 1.0 - pos) * lg) * k_scale
    chunk_decay = jnp.exp(jnp.zeros((1, d), F32) + c * lg)
    gn = gn_ref[...]

    for s in range(rows // c):
        sl = pl.ds(s * c, c)
        q = q_ref[sl, :]
        k = k_ref[sl, :]
        v = v_ref[sl, :]
        scores = lax.dot_general(q, k, _NT_DIMS, preferred_element_type=F32) * intra
        y = jnp.dot(scores.astype(BF16), v, preferred_element_type=F32)
        state = state_ref[...]
        y = y + jnp.dot(q, state.astype(BF16), preferred_element_type=F32) * q_decay
        kd = (k.astype(F32) * k_decay).astype(BF16)
        kv = lax.dot_general(kd, v, _TN_DIMS, preferred_element_type=F32)
        state_ref[...] = state * chunk_decay + kv
        g = g_ref[sl, :].astype(F32)
        o_ref[sl, :] = (_silu(g) * _rms_norm(y, gn)).astype(o_ref.dtype)


def _retention(proj, gn_gain, *, batch, seq, rows):
    m = proj.shape[0]
    d = gn_gain.shape[0] // RET_HEADS
    nt = seq // rows
    lg = np.log1p(-np.exp2(-5.0 - np.arange(RET_HEADS))).astype(np.float32)

    def spec(part):
        return pl.BlockSpec((rows, d), lambda b, h, t, lg_ref: (b * nt + t, part * RET_HEADS + h))

    return pl.pallas_call(
        functools.partial(_retention_kernel, chunk=RET_CHUNK),
        grid_spec=pltpu.PrefetchScalarGridSpec(
            num_scalar_prefetch=1,
            grid=(batch, RET_HEADS, nt),
            in_specs=[spec(0), spec(1), spec(2), spec(3),
                      pl.BlockSpec((1, d), lambda b, h, t, lg_ref: (0, h))],
            out_specs=pl.BlockSpec((rows, d), lambda b, h, t, lg_ref: (b * nt + t, h)),
            scratch_shapes=[pltpu.VMEM((d, d), F32)],
        ),
        out_shape=jax.ShapeDtypeStruct((m, RET_HEADS * d), BF16),
        compiler_params=_params("parallel", "parallel", "arbitrary"),
        name="retention",
    )(jnp.asarray(lg), proj, proj, proj, proj, gn_gain.reshape(1, -1))


def _alibi_slopes(n):
    def pow2_slopes(m):
        start = 2.0 ** (-8.0 / m)
        return [start ** (i + 1) for i in range(m)]
    if math.log2(n).is_integer():
        s = pow2_slopes(n)
    else:
        c = 2 ** int(math.floor(math.log2(n)))
        s = pow2_slopes(c) + list(_alibi_slopes(2 * c))[0::2][: n - c]
    return np.asarray(s, dtype=np.float32)


def _moba_kernel(slope_ref, q_ref, k_ref, v_ref, o_ref, kmean_ref, vt_ref, sel_ref):
    slope = slope_ref[pl.program_id(1)]
    qi = pl.program_id(2)
    blk, d = q_ref.shape
    nblk = k_ref.shape[0] // blk
    scale = d ** -0.5

    @pl.when(qi == 0)
    def _():
        for j in range(nblk):
            rows = pl.ds(j * blk, blk)
            kmean_ref[j:j + 1, :] = jnp.sum(k_ref[rows, :].astype(F32), axis=0,
                                            keepdims=True) * (1.0 / blk)
            vt_ref[j] = v_ref[rows, :].astype(F32).T.astype(BF16)

    q = q_ref[...]

    kmean = kmean_ref[...]
    kmean_hi = kmean.astype(BF16)
    kmean_lo = (kmean - kmean_hi.astype(F32)).astype(BF16)
    gate = (lax.dot_general(kmean_hi, q, _NT_DIMS, preferred_element_type=F32)
            + lax.dot_general(kmean_lo, q, _NT_DIMS, preferred_element_type=F32))
    blk_id = lax.broadcasted_iota(jnp.int32, gate.shape, 0)
    rank = jnp.zeros(gate.shape, jnp.int32)
    for other in range(nblk):
        g_other = gate[other:other + 1, :]
        beats = (g_other > gate) | ((g_other == gate) & (other < blk_id))
        rank = rank + jnp.where(other < qi, jnp.where(beats, 1, 0), 0)
    sel_ref[...] = jnp.where((blk_id < qi) & (rank < MOBA_TOPK), 1.0, 0.0)

    dist0 = (lax.broadcasted_iota(jnp.int32, (blk, blk), 1)
             - lax.broadcasted_iota(jnp.int32, (blk, blk), 0)).astype(F32)
    bias0 = slope * dist0

    def logits(j):
        kb = k_ref[pl.ds(pl.multiple_of(j * blk, blk), blk), :]
        s = lax.dot_general(kb, q, _NT_DIMS, preferred_element_type=F32)
        return s * scale - bias0

    lg = jnp.where(dist0 >= 0, logits(qi), -jnp.inf)
    m = jnp.max(lg, axis=0, keepdims=True)
    p = jnp.exp(lg - m)
    l = jnp.sum(p, axis=0, keepdims=True)
    acc = jnp.dot(vt_ref[qi], p.astype(BF16), preferred_element_type=F32)

    def past_block(j, carry):
        m, l, acc = carry
        block_dist = ((qi - j) * blk).astype(F32)
        lg = logits(j) - slope * block_dist
        lg = jnp.where(sel_ref[pl.ds(j, 1), :] > 0.0, lg, -jnp.inf)
        m_new = jnp.maximum(m, jnp.max(lg, axis=0, keepdims=True))
        alpha = jnp.exp(m - m_new)
        p = jnp.exp(lg - m_new)
        l = alpha * l + jnp.sum(p, axis=0, keepdims=True)
        acc = alpha * acc + jnp.dot(vt_ref[j], p.astype(BF16), preferred_element_type=F32)
        return m_new, l, acc

    m, l, acc = lax.fori_loop(0, qi, past_block, (m, l, acc))
    o_ref[...] = (acc / l).T.astype(o_ref.dtype)


def _moba(proj, *, batch, seq, d):
    m = proj.shape[0]
    blk = MOBA_BLOCK
    assert seq % blk == 0
    nq = seq // blk
    slopes = _alibi_slopes(MOBA_HEADS)
    return pl.pallas_call(
        _moba_kernel,
        grid_spec=pltpu.PrefetchScalarGridSpec(
            num_scalar_prefetch=1,
            grid=(batch, MOBA_HEADS, nq),
            in_specs=[
                pl.BlockSpec((blk, d), lambda b, h, i, s_ref: (b * nq + i, h)),
                pl.BlockSpec((seq, d), lambda b, h, i, s_ref: (b, MOBA_HEADS + h)),
                pl.BlockSpec((seq, d), lambda b, h, i, s_ref: (b, 2 * MOBA_HEADS + h)),
            ],
            out_specs=pl.BlockSpec((blk, d), lambda b, h, i, s_ref: (b * nq + i, h)),
            scratch_shapes=[pltpu.VMEM((nq, d), F32),
                            pltpu.VMEM((nq, d, blk), BF16),
                            pltpu.VMEM((nq, blk), F32)],
        ),
        out_shape=jax.ShapeDtypeStruct((m, MOBA_HEADS * d), BF16),
        compiler_params=_params("parallel", "parallel", "arbitrary"),
        name="moba",
    )(jnp.asarray(slopes), proj, proj, proj)


def _memory_attention(qm_ref, kv_ref):
    width = qm_ref.shape[1]
    dh = width // MEM_HEADS
    outs = []
    for h in range(MEM_HEADS):
        q = qm_ref[:, h * dh:(h + 1) * dh]
        k = kv_ref[:, h * dh:(h + 1) * dh]
        v = kv_ref[:, width + h * dh:width + (h + 1) * dh]
        s = lax.dot_general(q, k, _NT_DIMS, preferred_element_type=F32) * dh ** -0.5
        p = jnp.exp(s - jnp.max(s, axis=-1, keepdims=True))
        o = jnp.dot(p.astype(BF16), v, preferred_element_type=F32)
        outs.append((o / jnp.sum(p, axis=-1, keepdims=True)).astype(BF16))
    return jnp.concatenate(outs, axis=-1)


def _project_out(tok, mem_out, x_ref, wo_ref, o_ref):
    mix = tok.shape[1]
    o_ref[...] = (x_ref[...]
                  + jnp.dot(tok, wo_ref[:mix, :], preferred_element_type=F32)
                  + jnp.dot(mem_out, wo_ref[mix:, :], preferred_element_type=F32))


def _out_proj_kernel(tok_ref, qm_ref, kv_ref, x_ref, wo_ref, o_ref):
    _project_out(tok_ref[...], _memory_attention(qm_ref, kv_ref), x_ref, wo_ref, o_ref)


def _conv_out_proj_kernel(gb_ref, gc_ref, hh_ref, gc_prev_ref, hh_prev_ref, cw_ref,
                          qm_ref, kv_ref, x_ref, wo_ref, o_ref, *, tiles_per_seq):
    first = pl.program_id(0) % tiles_per_seq == 0
    u = gc_ref[...].astype(F32) * hh_ref[...].astype(F32)
    prev = gc_prev_ref[...].astype(F32) * hh_prev_ref[...].astype(F32)
    prev = jnp.where(first, 0.0, prev)
    tok = (gb_ref[...].astype(F32) * _causal_conv3(u, prev, cw_ref[...])).astype(BF16)
    _project_out(tok, _memory_attention(qm_ref, kv_ref), x_ref, wo_ref, o_ref)


def _out_proj(tok, proj, conv_w, kv, x, wo, *, seq, mem_tokens, tm):
    m, d = x.shape
    mem_width = kv.shape[1] // 2
    mix = d - mem_width
    tiles_per_seq = seq // tm
    qm_block = proj.shape[1] // mem_width - 1
    tail_specs = [
        pl.BlockSpec((tm, mem_width), lambda i: (i, qm_block)),
        pl.BlockSpec((mem_tokens, 2 * mem_width), lambda i: (i // tiles_per_seq, 0)),
        pl.BlockSpec((tm, d), lambda i: (i, 0)),
        pl.BlockSpec((d, d), lambda i: (0, 0)),
    ]
    if tok is not None:
        kernel = _out_proj_kernel
        head_specs = [pl.BlockSpec((tm, mix), lambda i: (i, 0))]
        head_args = (tok,)
    else:
        kernel = functools.partial(_conv_out_proj_kernel, tiles_per_seq=tiles_per_seq)
        halo = tm // SUBLANES

        def prev_rows(part):
            return pl.BlockSpec((SUBLANES, mix), lambda i: (jnp.maximum(i * halo - 1, 0), part))

        head_specs = [pl.BlockSpec((tm, mix), lambda i: (i, 0)),
                      pl.BlockSpec((tm, mix), lambda i: (i, 1)),
                      pl.BlockSpec((tm, mix), lambda i: (i, 2)),
                      prev_rows(1), prev_rows(2),
                      pl.BlockSpec((CONV_WIDTH, mix), lambda i: (0, 0))]
        head_args = (proj, proj, proj, proj, proj, conv_w)
    return pl.pallas_call(
        kernel,
        grid=(m // tm,),
        in_specs=head_specs + tail_specs,
        out_specs=pl.BlockSpec((tm, d), lambda i: (i, 0)),
        out_shape=jax.ShapeDtypeStruct((m, d), F32),
        compiler_params=_params("parallel"),
        name="out_proj",
    )(*head_args, proj, kv, x, wo)


def _conv_ffn_kernel(x_ref, g_ref, wg_ref, wu_ref, cwg_ref, cwu_ref, cbg_ref, cbu_ref, wd_ref,
                     fg_ref, o_ref, h_ref, carry_ref, *, tiles_per_seq, final_norm):
    f = pl.program_id(1)
    first = pl.program_id(0) % tiles_per_seq == 0
    tm = x_ref.shape[0]

    @pl.when(f == 0)
    def _():
        h_ref[...] = _rms_norm(x_ref[...], g_ref[...]).astype(BF16)

    h = h_ref[...]

    def conv_branch(w_ref, cw_ref, cb_ref, slot):
        up = jnp.dot(h, w_ref[...], preferred_element_type=F32)
        prev = jnp.where(first, 0.0, carry_ref[f, slot])
        carry_ref[f, slot] = up[tm - SUBLANES:, :]
        return _causal_conv3(up, prev, cw_ref[...]) + cb_ref[...]

    gate = conv_branch(wg_ref, cwg_ref, cbg_ref, 0)
    lin = conv_branch(wu_ref, cwu_ref, cbu_ref, 1)
    act = (_silu(gate) * lin).astype(BF16)
    down = jnp.dot(act, wd_ref[...], preferred_element_type=F32)

    @pl.when(f == 0)
    def _():
        o_ref[...] = x_ref[...] + down

    @pl.when(f > 0)
    def _():
        o_ref[...] += down

    if final_norm:
        @pl.when(f == pl.num_programs(1) - 1)
        def _():
            o_ref[...] = _rms_norm(o_ref[...], fg_ref[...])


def _conv_ffn(x, gain, w_up, conv_w, conv_b, w_down, final_gain, *, seq, tm, tf, final_norm):
    m, d = x.shape
    d_ff = w_down.shape[0]
    nf = d_ff // tf
    return pl.pallas_call(
        functools.partial(_conv_ffn_kernel, tiles_per_seq=seq // tm, final_norm=final_norm),
        grid=(m // tm, nf),
        in_specs=[
            pl.BlockSpec((tm, d), lambda i, f: (i, 0)),
            pl.BlockSpec((1, d), lambda i, f: (0, 0)),
            pl.BlockSpec((d, tf), lambda i, f: (0, f)),
            pl.BlockSpec((d, tf), lambda i, f: (0, nf + f)),
            pl.BlockSpec((CONV_WIDTH, tf), lambda i, f: (0, f)),
            pl.BlockSpec((CONV_WIDTH, tf), lambda i, f: (0, nf + f)),
            pl.BlockSpec((1, tf), lambda i, f: (0, f)),
            pl.BlockSpec((1, tf), lambda i, f: (0, nf + f)),
            pl.BlockSpec((tf, d), lambda i, f: (f, 0)),
            pl.BlockSpec((1, d), lambda i, f: (0, 0)),
        ],
        out_specs=pl.BlockSpec((tm, d), lambda i, f: (i, 0)),
        out_shape=jax.ShapeDtypeStruct((m, d), F32),
        scratch_shapes=[pltpu.VMEM((tm, d), BF16),
                        pltpu.VMEM((nf, 2, SUBLANES, tf), F32)],
        compiler_params=_params("arbitrary", "arbitrary"),
        name="conv_ffn",
    )(x, gain.reshape(1, d), w_up, w_up, conv_w, conv_w,
      conv_b.reshape(1, -1), conv_b.reshape(1, -1), w_down, final_gain.reshape(1, d))


def _half_tile(n):
    return n // 2 if n % 512 == 0 else n


def _trunk(x, mem, mem_norm, layers, final_norm):
    batch, seq, d = x.shape
    mem_tokens = mem.shape[1]
    x = x.reshape(batch * seq, d)
    mem = mem.reshape(batch * mem_tokens, d)
    n_layers = len(layers)
    for i, (norm_mix, w_in, extra, w_mem_kv, w_o, norm_ffn, w_up, cw, cb, w_down) in enumerate(layers):
        kind = i % 3
        mem_width = w_mem_kv.shape[1] // 2
        mix = d - mem_width
        w_in = w_in.astype(BF16)
        proj = _norm_matmul(x, norm_mix, w_in, tm=512, tn=_half_tile(w_in.shape[1]))
        kv = _norm_matmul(mem, mem_norm, w_mem_kv.astype(BF16), tm=mem_tokens,
                          tn=w_mem_kv.shape[1])
        if kind == 0:
            tok = _retention(proj, extra, batch=batch, seq=seq, rows=1024)
        elif kind == 1:
            tok = None
        else:
            tok = _moba(proj, batch=batch, seq=seq, d=mix // MOBA_HEADS)
        x = _out_proj(tok, proj, extra, kv, x, w_o.astype(BF16),
                      seq=seq, mem_tokens=mem_tokens, tm=512)
        x = _conv_ffn(x, norm_ffn, w_up.astype(BF16), cw, cb, w_down.astype(BF16), final_norm,
                      seq=seq, tm=512, tf=512, final_norm=(i == n_layers - 1))
    return x.reshape(batch, seq, d)


def kernel(x, mem, mem_norm,
           l0_norm_mix, l0_w_in, l0_ret_gn, l0_w_mem_kv, l0_w_o, l0_norm_ffn, l0_ffn_w_up, l0_ffn_conv_w, l0_ffn_conv_b, l0_ffn_w_down,
           l1_norm_mix, l1_w_in, l1_conv_w, l1_w_mem_kv, l1_w_o, l1_norm_ffn, l1_ffn_w_up, l1_ffn_conv_w, l1_ffn_conv_b, l1_ffn_w_down,
           l2_norm_mix, l2_w_in, l2_w_mem_kv, l2_w_o, l2_norm_ffn, l2_ffn_w_up, l2_ffn_conv_w, l2_ffn_conv_b, l2_ffn_w_down,
           l3_norm_mix, l3_w_in, l3_ret_gn, l3_w_mem_kv, l3_w_o, l3_norm_ffn, l3_ffn_w_up, l3_ffn_conv_w, l3_ffn_conv_b, l3_ffn_w_down,
           final_norm):
    layers = [
        (l0_norm_mix, l0_w_in, l0_ret_gn, l0_w_mem_kv, l0_w_o, l0_norm_ffn, l0_ffn_w_up, l0_ffn_conv_w, l0_ffn_conv_b, l0_ffn_w_down),
        (l1_norm_mix, l1_w_in, l1_conv_w, l1_w_mem_kv, l1_w_o, l1_norm_ffn, l1_ffn_w_up, l1_ffn_conv_w, l1_ffn_conv_b, l1_ffn_w_down),
        (l2_norm_mix, l2_w_in, None, l2_w_mem_kv, l2_w_o, l2_norm_ffn, l2_ffn_w_up, l2_ffn_conv_w, l2_ffn_conv_b, l2_ffn_w_down),
        (l3_norm_mix, l3_w_in, l3_ret_gn, l3_w_mem_kv, l3_w_o, l3_norm_ffn, l3_ffn_w_up, l3_ffn_conv_w, l3_ffn_conv_b, l3_ffn_w_down),
    ]
    return _trunk(x, mem, mem_norm, layers, final_norm)
```

```python
import functools
import math

import jax
import jax.numpy as jnp
import numpy as np
from jax import lax
from jax.experimental import pallas as pl
from jax.experimental.pallas import tpu as pltpu

F32 = jnp.float32
BF16 = jnp.bfloat16

NORM_EPS = 1e-6
MEM_HEADS = 4
RET_HEADS = 6
RET_CHUNK = 256
CONV_WIDTH = 3
MOBA_HEADS = 12
MOBA_BLOCK = 256
MOBA_TOPK = 3

SUBLANES = 8
VMEM_LIMIT_BYTES = 56 * 1024 * 1024

_NT_DIMS = (((1,), (1,)), ((), ()))
_TN_DIMS = (((0,), (0,)), ((), ()))


def _params(*semantics):
    return pltpu.CompilerParams(dimension_semantics=semantics,
                                vmem_limit_bytes=VMEM_LIMIT_BYTES)


def _rms_norm(x, gain):
    return x * lax.rsqrt(jnp.mean(x * x, axis=-1, keepdims=True) + NORM_EPS) * gain


def _silu(g):
    return g * (1.0 / (1.0 + jnp.exp(-g)))


def _shift_rows(u, prev, k):
    rolled = pltpu.roll(u, k, axis=0)
    prev_rolled = pltpu.roll(prev, k, axis=0)
    row = lax.broadcasted_iota(jnp.int32, prev.shape, 0)
    head = jnp.where(row < k, prev_rolled, rolled[:SUBLANES])
    return jnp.concatenate([head, rolled[SUBLANES:]], axis=0)


def _causal_conv3(u, prev, w):
    return (w[0:1] * _shift_rows(u, prev, 2) + w[1:2] * _shift_rows(u, prev, 1)
            + w[2:3] * u)


def _norm_matmul_kernel(x_ref, g_ref, w_ref, o_ref):
    h = _rms_norm(x_ref[...], g_ref[...]).astype(BF16)
    o_ref[...] = jnp.dot(h, w_ref[...], preferred_element_type=F32).astype(o_ref.dtype)


def _norm_matmul(x, gain, w, *, tm, tn):
    m, d = x.shape
    n = w.shape[1]
    return pl.pallas_call(
        _norm_matmul_kernel,
        grid=(n // tn, m // tm),
        in_specs=[
            pl.BlockSpec((tm, d), lambda j, i: (i, 0)),
            pl.BlockSpec((1, d), lambda j, i: (0, 0)),
            pl.BlockSpec((d, tn), lambda j, i: (0, j)),
        ],
        out_specs=pl.BlockSpec((tm, tn), lambda j, i: (i, j)),
        out_shape=jax.ShapeDtypeStruct((m, n), BF16),
        compiler_params=_params("parallel", "parallel"),
        name="norm_matmul",
    )(x, gain.reshape(1, d), w)


def _retention_kernel(lg_ref, q_ref, k_ref, v_ref, g_ref, gn_ref, o_ref, state_ref, *, chunk):
    lg = lg_ref[pl.program_id(1)]
    rows, d = q_ref.shape
    c = chunk
    k_scale = d ** -0.5

    @pl.when(pl.program_id(2) == 0)
    def _():
        state_ref[...] = jnp.zeros_like(state_ref)

    diff = (lax.broadcasted_iota(jnp.int32, (c, c), 0)
            - lax.broadcasted_iota(jnp.int32, (c, c), 1)).astype(F32)
    intra = jnp.where(diff >= 0, jnp.exp(jnp.maximum(diff, 0.0) * lg), 0.0) * k_scale
    pos = lax.broadcasted_iota(jnp.int32, (c, d), 0).astype(F32)
    q_decay = jnp.exp((pos + 1.0) * lg)
    k_decay = jnp.exp((c - 1.0 - pos) * lg) * k_scale
    chunk_decay = jnp.exp(jnp.zeros((1, d), F32) + c * lg)
    gn = gn_ref[...]

    for s in range(rows // c):
        sl = pl.ds(s * c, c)
        q = q_ref[sl, :]
        k = k_ref[sl, :]
        v = v_ref[sl, :]
        scores = lax.dot_general(q, k, _NT_DIMS, preferred_element_type=F32) * intra
        y = jnp.dot(scores.astype(BF16), v, preferred_element_type=F32)
        state = state_ref[...]
        y = y + jnp.dot(q, state.astype(BF16), preferred_element_type=F32) * q_decay
        kd = (k.astype(F32) * k_decay).astype(BF16)
        kv = lax.dot_general(kd, v, _TN_DIMS, preferred_element_type=F32)
        state_ref[...] = state * chunk_decay + kv
        g = g_ref[sl, :].astype(F32)
        o_ref[sl, :] = (_silu(g) * _rms_norm(y, gn)).astype(o_ref.dtype)


def _retention(proj, gn_gain, *, batch, seq, rows):
    m = proj.shape[0]
    d = gn_gain.shape[0] // RET_HEADS
    nt = seq // rows
    lg = np.log1p(-np.exp2(-5.0 - np.arange(RET_HEADS))).astype(np.float32)

    def spec(part):
        return pl.BlockSpec((rows, d), lambda b, h, t, lg_ref: (b * nt + t, part * RET_HEADS + h))

    return pl.pallas_call(
        functools.partial(_retention_kernel, chunk=RET_CHUNK),
        grid_spec=pltpu.PrefetchScalarGridSpec(
            num_scalar_prefetch=1,
            grid=(batch, RET_HEADS, nt),
            in_specs=[spec(0), spec(1), spec(2), spec(3),
                      pl.BlockSpec((1, d), lambda b, h, t, lg_ref: (0, h))],
            out_specs=pl.BlockSpec((rows, d), lambda b, h, t, lg_ref: (b * nt + t, h)),
            scratch_shapes=[pltpu.VMEM((d, d), F32)],
        ),
        out_shape=jax.ShapeDtypeStruct((m, RET_HEADS * d), BF16),
        compiler_params=_params("parallel", "parallel", "arbitrary"),
        name="retention",
    )(jnp.asarray(lg), proj, proj, proj, proj, gn_gain.reshape(1, -1))


def _alibi_slopes(n):
    def pow2_slopes(m):
        start = 2.0 ** (-8.0 / m)
        return [start ** (i + 1) for i in range(m)]
    if math.log2(n).is_integer():
        s = pow2_slopes(n)
    else:
        c = 2 ** int(math.floor(math.log2(n)))
        s = pow2_slopes(c) + list(_alibi_slopes(2 * c))[0::2][: n - c]
    return np.asarray(s, dtype=np.float32)


def _moba_kernel(slope_ref, q_ref, k_ref, v_ref, o_ref, kmean_ref, vt_ref, selb_ref, *, heads):
    qi = pl.program_id(2)
    blk = q_ref.shape[0]
    d = q_ref.shape[1] // heads
    nblk = k_ref.shape[0] // blk
    pair = 2 * blk
    scale = d ** -0.5
    own_pair = qi // 2

    @pl.when(qi == 0)
    def _():
        for t in range(heads):
            cols = slice(t * d, (t + 1) * d)
            for j in range(nblk):
                kmean_ref[t, j:j + 1, :] = jnp.sum(
                    k_ref[j * blk:(j + 1) * blk, cols].astype(F32), axis=0, keepdims=True) * (1.0 / blk)
            for jp in range(nblk // 2):
                vt_ref[t, jp] = v_ref[jp * pair:(jp + 1) * pair, cols].astype(F32).T.astype(BF16)

    dist0 = (lax.broadcasted_iota(jnp.int32, (pair, blk), 1)
             - lax.broadcasted_iota(jnp.int32, (pair, blk), 0)).astype(F32)

    def head_setup(t):
        slope = slope_ref[pl.program_id(1) * heads + t]
        q = q_ref[:, t * d:(t + 1) * d]
        kmean = kmean_ref[t]
        kmean_hi = kmean.astype(BF16)
        kmean_lo = (kmean - kmean_hi.astype(F32)).astype(BF16)
        gate = (lax.dot_general(kmean_hi, q, _NT_DIMS, preferred_element_type=F32)
                + lax.dot_general(kmean_lo, q, _NT_DIMS, preferred_element_type=F32))
        blk_id = lax.broadcasted_iota(jnp.int32, gate.shape, 0)
        rank = jnp.zeros(gate.shape, jnp.int32)
        for other in range(nblk):
            g_other = gate[other:other + 1, :]
            beats = (g_other > gate) | ((g_other == gate) & (other < blk_id))
            rank = rank + jnp.where(other < qi, jnp.where(beats, 1, 0), 0)
        keep = ((blk_id < qi) & (rank < MOBA_TOPK)) | (blk_id == qi)
        selb_ref[t] = jnp.where(keep, 0.0, -jnp.inf)
        return slope, q, slope * dist0

    def pair_logits(t, setup, jp):
        slope, q, bias0 = setup
        kb = k_ref[pl.ds(pl.multiple_of(jp * pair, pair), pair), t * d:(t + 1) * d]
        s = lax.dot_general(kb, q, _NT_DIMS, preferred_element_type=F32)
        pair_bias = slope * ((qi - 2 * jp) * blk).astype(F32)
        lg = s * scale - bias0
        return jnp.concatenate(
            [lg[:blk] + (selb_ref[t, pl.ds(2 * jp, 1), :] - pair_bias),
             lg[blk:] + (selb_ref[t, pl.ds(2 * jp + 1, 1), :] - pair_bias)], axis=0)

    def attend(t, lg, jp, m, l, acc):
        m_new = jnp.max(lg, axis=0, keepdims=True)
        if m is not None:
            m_new = jnp.maximum(m, m_new)
        p = jnp.exp(lg - m_new)
        l_new = jnp.sum(p, axis=0, keepdims=True)
        acc_new = jnp.dot(vt_ref[t, jp], p.astype(BF16), preferred_element_type=F32)
        if m is not None:
            alpha = jnp.exp(m - m_new)
            l_new = alpha * l + l_new
            acc_new = alpha * acc + acc_new
        return m_new, l_new, acc_new

    setups = [head_setup(t) for t in range(heads)]

    causal = dist0 + ((qi - 2 * own_pair) * blk).astype(F32) >= 0
    state = []
    for t in range(heads):
        lg = jnp.where(causal, pair_logits(t, setups[t], own_pair), -jnp.inf)
        state.append(attend(t, lg, own_pair, None, None, None))

    def past_pair(jp, state):
        return tuple(attend(t, pair_logits(t, setups[t], jp), jp, *state[t])
                     for t in range(heads))

    state = lax.fori_loop(0, own_pair, past_pair, tuple(state))
    for t in range(heads):
        _, l, acc = state[t]
        o_ref[:, t * d:(t + 1) * d] = (acc / l).T.astype(o_ref.dtype)


def _moba(proj, *, batch, seq, d, heads):
    m = proj.shape[0]
    blk = MOBA_BLOCK
    assert seq % (2 * blk) == 0 and MOBA_HEADS % heads == 0
    nq = seq // blk
    groups = MOBA_HEADS // heads
    slopes = _alibi_slopes(MOBA_HEADS)
    return pl.pallas_call(
        functools.partial(_moba_kernel, heads=heads),
        grid_spec=pltpu.PrefetchScalarGridSpec(
            num_scalar_prefetch=1,
            grid=(batch, groups, nq),
            in_specs=[
                pl.BlockSpec((blk, heads * d), lambda b, h, i, s_ref: (b * nq + i, h)),
                pl.BlockSpec((seq, heads * d), lambda b, h, i, s_ref: (b, groups + h)),
                pl.BlockSpec((seq, heads * d), lambda b, h, i, s_ref: (b, 2 * groups + h)),
            ],
            out_specs=pl.BlockSpec((blk, heads * d), lambda b, h, i, s_ref: (b * nq + i, h)),
            scratch_shapes=[pltpu.VMEM((heads, nq, d), F32),
                            pltpu.VMEM((heads, nq // 2, d, 2 * blk), BF16),
                            pltpu.VMEM((heads, nq, blk), F32)],
        ),
        out_shape=jax.ShapeDtypeStruct((m, MOBA_HEADS * d), BF16),
        compiler_params=_params("parallel", "parallel", "arbitrary"),
        name="moba",
    )(jnp.asarray(slopes), proj, proj, proj)


def _memory_attention(qm_ref, kv_ref):
    width = qm_ref.shape[1]
    dh = width // MEM_HEADS
    outs = []
    for h in range(MEM_HEADS):
        q = qm_ref[:, h * dh:(h + 1) * dh]
        k = kv_ref[:, h * dh:(h + 1) * dh]
        v = kv_ref[:, width + h * dh:width + (h + 1) * dh]
        s = lax.dot_general(q, k, _NT_DIMS, preferred_element_type=F32) * dh ** -0.5
        p = jnp.exp(s - jnp.max(s, axis=-1, keepdims=True))
        o = jnp.dot(p.astype(BF16), v, preferred_element_type=F32)
        outs.append((o / jnp.sum(p, axis=-1, keepdims=True)).astype(BF16))
    return jnp.concatenate(outs, axis=-1)


def _project_out(tok, mem_out, x_ref, wo_ref, o_ref):
    mix = tok.shape[1]
    o_ref[...] = (x_ref[...]
                  + jnp.dot(tok, wo_ref[:mix, :], preferred_element_type=F32)
                  + jnp.dot(mem_out, wo_ref[mix:, :], preferred_element_type=F32))


def _out_proj_kernel(tok_ref, qm_ref, kv_ref, x_ref, wo_ref, o_ref):
    _project_out(tok_ref[...], _memory_attention(qm_ref, kv_ref), x_ref, wo_ref, o_ref)


def _conv_out_proj_kernel(gb_ref, gc_ref, hh_ref, gc_prev_ref, hh_prev_ref, cw_ref,
                          qm_ref, kv_ref, x_ref, wo_ref, o_ref, *, tiles_per_seq):
    first = pl.program_id(0) % tiles_per_seq == 0
    u = gc_ref[...].astype(F32) * hh_ref[...].astype(F32)
    prev = gc_prev_ref[...].astype(F32) * hh_prev_ref[...].astype(F32)
    prev = jnp.where(first, 0.0, prev)
    tok = (gb_ref[...].astype(F32) * _causal_conv3(u, prev, cw_ref[...])).astype(BF16)
    _project_out(tok, _memory_attention(qm_ref, kv_ref), x_ref, wo_ref, o_ref)


def _out_proj(tok, proj, conv_w, kv, x, wo, *, seq, mem_tokens, tm):
    m, d = x.shape
    mem_width = kv.shape[1] // 2
    mix = d - mem_width
    tiles_per_seq = seq // tm
    qm_block = proj.shape[1] // mem_width - 1
    tail_specs = [
        pl.BlockSpec((tm, mem_width), lambda i: (i, qm_block)),
        pl.BlockSpec((mem_tokens, 2 * mem_width), lambda i: (i // tiles_per_seq, 0)),
        pl.BlockSpec((tm, d), lambda i: (i, 0)),
        pl.BlockSpec((d, d), lambda i: (0, 0)),
    ]
    if tok is not None:
        kernel = _out_proj_kernel
        head_specs = [pl.BlockSpec((tm, mix), lambda i: (i, 0))]
        head_args = (tok,)
    else:
        kernel = functools.partial(_conv_out_proj_kernel, tiles_per_seq=tiles_per_seq)
        halo = tm // SUBLANES

        def prev_rows(part):
            return pl.BlockSpec((SUBLANES, mix), lambda i: (jnp.maximum(i * halo - 1, 0), part))

        head_specs = [pl.BlockSpec((tm, mix), lambda i: (i, 0)),
                      pl.BlockSpec((tm, mix), lambda i: (i, 1)),
                      pl.BlockSpec((tm, mix), lambda i: (i, 2)),
                      prev_rows(1), prev_rows(2),
                      pl.BlockSpec((CONV_WIDTH, mix), lambda i: (0, 0))]
        head_args = (proj, proj, proj, proj, proj, conv_w)
    return pl.pallas_call(
        kernel,
        grid=(m // tm,),
        in_specs=head_specs + tail_specs,
        out_specs=pl.BlockSpec((tm, d), lambda i: (i, 0)),
        out_shape=jax.ShapeDtypeStruct((m, d), F32),
        compiler_params=_params("parallel"),
        name="out_proj",
    )(*head_args, proj, kv, x, wo)


def _conv_ffn_kernel(x_ref, g_ref, wg_ref, wu_ref, cwg_ref, cwu_ref, cbg_ref, cbu_ref, wd_ref,
                     fg_ref, o_ref, h_ref, carry_ref, *, tiles_per_seq, row_strips, final_norm):
    f = pl.program_id(1)
    first = pl.program_id(0) % tiles_per_seq == 0
    tm = x_ref.shape[0]

    @pl.when(f == 0)
    def _():
        x = x_ref[...]
        h_ref[...] = _rms_norm(x, g_ref[...]).astype(BF16)
        o_ref[...] = x

    strip = tm // row_strips
    ups = [[jnp.dot(h_ref[pl.ds(s * strip, strip), :], w_ref[...], preferred_element_type=F32)
            for w_ref in (wg_ref, wu_ref)] for s in range(row_strips)]
    for s in range(row_strips):
        branches = []
        for slot, (cw_ref, cb_ref) in enumerate(((cwg_ref, cbg_ref), (cwu_ref, cbu_ref))):
            if s == 0:
                prev = jnp.where(first, 0.0, carry_ref[f, slot])
            else:
                prev = ups[s - 1][slot][strip - SUBLANES:, :]
            branches.append(_causal_conv3(ups[s][slot], prev, cw_ref[...]) + cb_ref[...])
        act = (_silu(branches[0]) * branches[1]).astype(BF16)
        o_ref[pl.ds(s * strip, strip), :] += jnp.dot(act, wd_ref[...], preferred_element_type=F32)
    for slot in range(2):
        carry_ref[f, slot] = ups[row_strips - 1][slot][strip - SUBLANES:, :]

    if final_norm:
        @pl.when(f == pl.num_programs(1) - 1)
        def _():
            o_ref[...] = _rms_norm(o_ref[...], fg_ref[...])


def _conv_ffn(x, gain, w_up, conv_w, conv_b, w_down, final_gain, *, seq, tm, tf, final_norm):
    m, d = x.shape
    d_ff = w_down.shape[0]
    nf = d_ff // tf
    return pl.pallas_call(
        functools.partial(_conv_ffn_kernel, tiles_per_seq=seq // tm, row_strips=4,
                          final_norm=final_norm),
        grid=(m // tm, nf),
        in_specs=[
            pl.BlockSpec((tm, d), lambda i, f: (i, 0)),
            pl.BlockSpec((1, d), lambda i, f: (0, 0)),
            pl.BlockSpec((d, tf), lambda i, f: (0, f)),
            pl.BlockSpec((d, tf), lambda i, f: (0, nf + f)),
            pl.BlockSpec((CONV_WIDTH, tf), lambda i, f: (0, f)),
            pl.BlockSpec((CONV_WIDTH, tf), lambda i, f: (0, nf + f)),
            pl.BlockSpec((1, tf), lambda i, f: (0, f)),
            pl.BlockSpec((1, tf), lambda i, f: (0, nf + f)),
            pl.BlockSpec((tf, d), lambda i, f: (f, 0)),
            pl.BlockSpec((1, d), lambda i, f: (0, 0)),
        ],
        out_specs=pl.BlockSpec((tm, d), lambda i, f: (i, 0)),
        out_shape=jax.ShapeDtypeStruct((m, d), F32),
        scratch_shapes=[pltpu.VMEM((tm, d), BF16),
                        pltpu.VMEM((nf, 2, SUBLANES, tf), F32)],
        compiler_params=_params("arbitrary", "arbitrary"),
        name="conv_ffn",
    )(x, gain.reshape(1, d), w_up, w_up, conv_w, conv_w,
      conv_b.reshape(1, -1), conv_b.reshape(1, -1), w_down, final_gain.reshape(1, d))


def _half_tile(n):
    return n // 2 if n % 512 == 0 else n


def _trunk(x, mem, mem_norm, layers, final_norm):
    batch, seq, d = x.shape
    mem_tokens = mem.shape[1]
    x = x.reshape(batch * seq, d)
    mem = mem.reshape(batch * mem_tokens, d)
    n_layers = len(layers)
    for i, (norm_mix, w_in, extra, w_mem_kv, w_o, norm_ffn, w_up, cw, cb, w_down) in enumerate(layers):
        kind = i % 3
        mem_width = w_mem_kv.shape[1] // 2
        mix = d - mem_width
        w_in = w_in.astype(BF16)
        proj = _norm_matmul(x, norm_mix, w_in, tm=512, tn=_half_tile(w_in.shape[1]))
        kv = _norm_matmul(mem, mem_norm, w_mem_kv.astype(BF16), tm=mem_tokens,
                          tn=w_mem_kv.shape[1])
        if kind == 0:
            tok = _retention(proj, extra, batch=batch, seq=seq, rows=1024)
        elif kind == 1:
            tok = None
        else:
            tok = _moba(proj, batch=batch, seq=seq, d=mix // MOBA_HEADS, heads=2)
        x = _out_proj(tok, proj, extra, kv, x, w_o.astype(BF16),
                      seq=seq, mem_tokens=mem_tokens, tm=512)
        x = _conv_ffn(x, norm_ffn, w_up.astype(BF16), cw, cb, w_down.astype(BF16), final_norm,
                      seq=seq, tm=512, tf=512, final_norm=(i == n_layers - 1))
    return x.reshape(batch, seq, d)


def kernel(x, mem, mem_norm,
           l0_norm_mix, l0_w_in, l0_ret_gn, l0_w_mem_kv, l0_w_o, l0_norm_ffn, l0_ffn_w_up, l0_ffn_conv_w, l0_ffn_conv_b, l0_ffn_w_down,
           l1_norm_mix, l1_w_in, l1_conv_w, l1_w_mem_kv, l1_w_o, l1_norm_ffn, l1_ffn_w_up, l1_ffn_conv_w, l1_ffn_conv_b, l1_ffn_w_down,
           l2_norm_mix, l2_w_in, l2_w_mem_kv, l2_w_o, l2_norm_ffn, l2_ffn_w_up, l2_ffn_conv_w, l2_ffn_conv_b, l2_ffn_w_down,
           l3_norm_mix, l3_w_in, l3_ret_gn, l3_w_mem_kv, l3_w_o, l3_norm_ffn, l3_ffn_w_up, l3_ffn_conv_w, l3_ffn_conv_b, l3_ffn_w_down,
           final_norm):
    layers = [
        (l0_norm_mix, l0_w_in, l0_ret_gn, l0_w_mem_kv, l0_w_o, l0_norm_ffn, l0_ffn_w_up, l0_ffn_conv_w, l0_ffn_conv_b, l0_ffn_w_down),
        (l1_norm_mix, l1_w_in, l1_conv_w, l1_w_mem_kv, l1_w_o, l1_norm_ffn, l1_ffn_w_up, l1_ffn_conv_w, l1_ffn_conv_b, l1_ffn_w_down),
        (l2_norm_mix, l2_w_in, None, l2_w_mem_kv, l2_w_o, l2_norm_ffn, l2_ffn_w_up, l2_ffn_conv_w, l2_ffn_conv_b, l2_ffn_w_down),
        (l3_norm_mix, l3_w_in, l3_ret_gn, l3_w_mem_kv, l3_w_o, l3_norm_ffn, l3_ffn_w_up, l3_ffn_conv_w, l3_ffn_conv_b, l3_ffn_w_down),
    ]
    return _trunk(x, mem, mem_norm, layers, final_norm)
```

```python
import functools
import math

import jax
import jax.numpy as jnp
import numpy as np
from jax import lax
from jax.experimental import pallas as pl
from jax.experimental.pallas import tpu as pltpu

F32 = jnp.float32
BF16 = jnp.bfloat16

NORM_EPS = 1e-6
MEM_HEADS = 4
RET_HEADS = 6
RET_CHUNK = 256
CONV_WIDTH = 3
MOBA_HEADS = 12
MOBA_BLOCK = 256
MOBA_TOPK = 3

SUBLANES = 8
VMEM_LIMIT_BYTES = 56 * 1024 * 1024

_NT_DIMS = (((1,), (1,)), ((), ()))
_TN_DIMS = (((0,), (0,)), ((), ()))


def _params(*semantics):
    return pltpu.CompilerParams(dimension_semantics=semantics,
                                vmem_limit_bytes=VMEM_LIMIT_BYTES)


def _rms_norm(x, gain):
    return x * lax.rsqrt(jnp.mean(x * x, axis=-1, keepdims=True) + NORM_EPS) * gain


def _silu(g):
    return g * (1.0 / (1.0 + jnp.exp(-g)))


def _shift_rows(u, prev, k):
    rolled = pltpu.roll(u, k, axis=0)
    prev_rolled = pltpu.roll(prev, k, axis=0)
    row = lax.broadcasted_iota(jnp.int32, prev.shape, 0)
    head = jnp.where(row < k, prev_rolled, rolled[:SUBLANES])
    return jnp.concatenate([head, rolled[SUBLANES:]], axis=0)


def _causal_conv3(u, prev, w):
    return (w[0:1] * _shift_rows(u, prev, 2) + w[1:2] * _shift_rows(u, prev, 1)
            + w[2:3] * u)


def _norm_matmul_kernel(x_ref, g_ref, w_ref, o_ref):
    h = _rms_norm(x_ref[...], g_ref[...]).astype(BF16)
    o_ref[...] = jnp.dot(h, w_ref[...], preferred_element_type=F32).astype(o_ref.dtype)


def _norm_matmul(x, gain, w, *, tm, tn):
    m, d = x.shape
    n = w.shape[1]
    return pl.pallas_call(
        _norm_matmul_kernel,
        grid=(n // tn, m // tm),
        in_specs=[
            pl.BlockSpec((tm, d), lambda j, i: (i, 0)),
            pl.BlockSpec((1, d), lambda j, i: (0, 0)),
            pl.BlockSpec((d, tn), lambda j, i: (0, j)),
        ],
        out_specs=pl.BlockSpec((tm, tn), lambda j, i: (i, j)),
        out_shape=jax.ShapeDtypeStruct((m, n), BF16),
        compiler_params=_params("parallel", "parallel"),
        name="norm_matmul",
    )(x, gain.reshape(1, d), w)


def _retention_kernel(lg_ref, q_ref, k_ref, v_ref, g_ref, gn_ref, o_ref, state_ref, *, chunk):
    lg = lg_ref[pl.program_id(1)]
    rows, d = q_ref.shape
    c = chunk
    k_scale = d ** -0.5

    @pl.when(pl.program_id(2) == 0)
    def _():
        state_ref[...] = jnp.zeros_like(state_ref)

    diff = (lax.broadcasted_iota(jnp.int32, (c, c), 0)
            - lax.broadcasted_iota(jnp.int32, (c, c), 1)).astype(F32)
    intra = jnp.where(diff >= 0, jnp.exp(jnp.maximum(diff, 0.0) * lg), 0.0) * k_scale
    pos = lax.broadcasted_iota(jnp.int32, (c, d), 0).astype(F32)
    q_decay = jnp.exp((pos + 1.0) * lg)
    k_decay = jnp.exp((c - 1.0 - pos) * lg) * k_scale
    chunk_decay = jnp.exp(jnp.zeros((1, d), F32) + c * lg)
    gn = gn_ref[...]

    for s in range(rows // c):
        sl = pl.ds(s * c, c)
        q = q_ref[sl, :]
        k = k_ref[sl, :]
        v = v_ref[sl, :]
        scores = lax.dot_general(q, k, _NT_DIMS, preferred_element_type=F32) * intra
        y = jnp.dot(scores.astype(BF16), v, preferred_element_type=F32)
        state = state_ref[...]
        y = y + jnp.dot(q, state.astype(BF16), preferred_element_type=F32) * q_decay
        kd = (k.astype(F32) * k_decay).astype(BF16)
        kv = lax.dot_general(kd, v, _TN_DIMS, preferred_element_type=F32)
        state_ref[...] = state * chunk_decay + kv
        g = g_ref[sl, :].astype(F32)
        o_ref[sl, :] = (_silu(g) * _rms_norm(y, gn)).astype(o_ref.dtype)


def _retention(proj, gn_gain, *, batch, seq, rows):
    m = proj.shape[0]
    d = gn_gain.shape[0] // RET_HEADS
    nt = seq // rows
    lg = np.log1p(-np.exp2(-5.0 - np.arange(RET_HEADS))).astype(np.float32)

    def spec(part):
        return pl.BlockSpec((rows, d), lambda b, h, t, lg_ref: (b * nt + t, part * RET_HEADS + h))

    return pl.pallas_call(
        functools.partial(_retention_kernel, chunk=RET_CHUNK),
        grid_spec=pltpu.PrefetchScalarGridSpec(
            num_scalar_prefetch=1,
            grid=(batch, RET_HEADS, nt),
            in_specs=[spec(0), spec(1), spec(2), spec(3),
                      pl.BlockSpec((1, d), lambda b, h, t, lg_ref: (0, h))],
            out_specs=pl.BlockSpec((rows, d), lambda b, h, t, lg_ref: (b * nt + t, h)),
            scratch_shapes=[pltpu.VMEM((d, d), F32)],
        ),
        out_shape=jax.ShapeDtypeStruct((m, RET_HEADS * d), BF16),
        compiler_params=_params("parallel", "parallel", "arbitrary"),
        name="retention",
    )(jnp.asarray(lg), proj, proj, proj, proj, gn_gain.reshape(1, -1))


def _alibi_slopes(n):
    def pow2_slopes(m):
        start = 2.0 ** (-8.0 / m)
        return [start ** (i + 1) for i in range(m)]
    if math.log2(n).is_integer():
        s = pow2_slopes(n)
    else:
        c = 2 ** int(math.floor(math.log2(n)))
        s = pow2_slopes(c) + list(_alibi_slopes(2 * c))[0::2][: n - c]
    return np.asarray(s, dtype=np.float32)


def _moba_kernel(slope_ref, q_ref, k_ref, v_ref, o_ref, kmean_ref, vt_ref, selb_ref, *, heads):
    qi = pl.program_id(2)
    blk = q_ref.shape[0]
    d = q_ref.shape[1] // heads
    nblk = k_ref.shape[0] // blk
    pair = 2 * blk
    scale = d ** -0.5
    own_pair = qi // 2

    @pl.when(qi == 0)
    def _():
        for t in range(heads):
            cols = slice(t * d, (t + 1) * d)
            for j in range(nblk):
                kmean_ref[t, j:j + 1, :] = jnp.sum(
                    k_ref[j * blk:(j + 1) * blk, cols].astype(F32), axis=0, keepdims=True) * (1.0 / blk)
            for jp in range(nblk // 2):
                vt_ref[t, jp] = v_ref[jp * pair:(jp + 1) * pair, cols].astype(F32).T.astype(BF16)

    dist0 = (lax.broadcasted_iota(jnp.int32, (pair, blk), 1)
             - lax.broadcasted_iota(jnp.int32, (pair, blk), 0)).astype(F32)

    def head_setup(t):
        slope = slope_ref[pl.program_id(1) * heads + t]
        q = q_ref[:, t * d:(t + 1) * d]
        kmean = kmean_ref[t]
        kmean_hi = kmean.astype(BF16)
        kmean_lo = (kmean - kmean_hi.astype(F32)).astype(BF16)
        gate = (lax.dot_general(kmean_hi, q, _NT_DIMS, preferred_element_type=F32)
                + lax.dot_general(kmean_lo, q, _NT_DIMS, preferred_element_type=F32))
        blk_id = lax.broadcasted_iota(jnp.int32, gate.shape, 0)
        rank = jnp.zeros(gate.shape, jnp.int32)
        for other in range(nblk):
            g_other = gate[other:other + 1, :]
            beats = (g_other > gate) | ((g_other == gate) & (other < blk_id))
            rank = rank + jnp.where(other < qi, jnp.where(beats, 1, 0), 0)
        keep = ((blk_id < qi) & (rank < MOBA_TOPK)) | (blk_id == qi)
        selb_ref[t] = jnp.where(keep, 0.0, -jnp.inf)
        return slope, q, slope * dist0

    def pair_logits(t, setup, jp):
        slope, q, bias0 = setup
        kb = k_ref[pl.ds(pl.multiple_of(jp * pair, pair), pair), t * d:(t + 1) * d]
        s = lax.dot_general(kb, q, _NT_DIMS, preferred_element_type=F32)
        pair_bias = slope * ((qi - 2 * jp) * blk).astype(F32)
        lg = s * scale - bias0
        return jnp.concatenate(
            [lg[:blk] + (selb_ref[t, pl.ds(2 * jp, 1), :] - pair_bias),
             lg[blk:] + (selb_ref[t, pl.ds(2 * jp + 1, 1), :] - pair_bias)], axis=0)

    def attend(t, lg, jp, m, l, acc):
        m_new = jnp.max(lg, axis=0, keepdims=True)
        if m is not None:
            m_new = jnp.maximum(m, m_new)
        p = jnp.exp(lg - m_new)
        l_new = jnp.sum(p, axis=0, keepdims=True)
        acc_new = jnp.dot(vt_ref[t, jp], p.astype(BF16), preferred_element_type=F32)
        if m is not None:
            alpha = jnp.exp(m - m_new)
            l_new = alpha * l + l_new
            acc_new = alpha * acc + acc_new
        return m_new, l_new, acc_new

    setups = [head_setup(t) for t in range(heads)]

    causal = dist0 + ((qi - 2 * own_pair) * blk).astype(F32) >= 0
    state = []
    for t in range(heads):
        lg = jnp.where(causal, pair_logits(t, setups[t], own_pair), -jnp.inf)
        state.append(attend(t, lg, own_pair, None, None, None))

    def past_pair(jp, state):
        return tuple(attend(t, pair_logits(t, setups[t], jp), jp, *state[t])
                     for t in range(heads))

    state = lax.fori_loop(0, own_pair, past_pair, tuple(state))
    for t in range(heads):
        _, l, acc = state[t]
        o_ref[:, t * d:(t + 1) * d] = (acc / l).T.astype(o_ref.dtype)


def _moba(proj, *, batch, seq, d, heads):
    m = proj.shape[0]
    blk = MOBA_BLOCK
    assert seq % (2 * blk) == 0 and MOBA_HEADS % heads == 0
    nq = seq // blk
    groups = MOBA_HEADS // heads
    slopes = _alibi_slopes(MOBA_HEADS)
    return pl.pallas_call(
        functools.partial(_moba_kernel, heads=heads),
        grid_spec=pltpu.PrefetchScalarGridSpec(
            num_scalar_prefetch=1,
            grid=(batch, groups, nq),
            in_specs=[
                pl.BlockSpec((blk, heads * d), lambda b, h, i, s_ref: (b * nq + i, h)),
                pl.BlockSpec((seq, heads * d), lambda b, h, i, s_ref: (b, groups + h)),
                pl.BlockSpec((seq, heads * d), lambda b, h, i, s_ref: (b, 2 * groups + h)),
            ],
            out_specs=pl.BlockSpec((blk, heads * d), lambda b, h, i, s_ref: (b * nq + i, h)),
            scratch_shapes=[pltpu.VMEM((heads, nq, d), F32),
                            pltpu.VMEM((heads, nq // 2, d, 2 * blk), BF16),
                            pltpu.VMEM((heads, nq, blk), F32)],
        ),
        out_shape=jax.ShapeDtypeStruct((m, MOBA_HEADS * d), BF16),
        compiler_params=_params("parallel", "parallel", "arbitrary"),
        name="moba",
    )(jnp.asarray(slopes), proj, proj, proj)


def _memory_attention(qm_ref, kv_ref):
    width = qm_ref.shape[1]
    dh = width // MEM_HEADS
    outs = []
    for h in range(MEM_HEADS):
        q = qm_ref[:, h * dh:(h + 1) * dh]
        k = kv_ref[:, h * dh:(h + 1) * dh]
        v = kv_ref[:, width + h * dh:width + (h + 1) * dh]
        s = lax.dot_general(q, k, _NT_DIMS, preferred_element_type=F32) * dh ** -0.5
        p = jnp.exp(s - jnp.max(s, axis=-1, keepdims=True))
        o = jnp.dot(p.astype(BF16), v, preferred_element_type=F32)
        outs.append((o / jnp.sum(p, axis=-1, keepdims=True)).astype(BF16))
    return jnp.concatenate(outs, axis=-1)


def _project_out(tok, mem_out, x_ref, wo_ref, o_ref):
    mix = tok.shape[1]
    o_ref[...] = (x_ref[...]
                  + jnp.dot(tok, wo_ref[:mix, :], preferred_element_type=F32)
                  + jnp.dot(mem_out, wo_ref[mix:, :], preferred_element_type=F32))


def _out_proj_kernel(tok_ref, qm_ref, kv_ref, x_ref, wo_ref, o_ref):
    _project_out(tok_ref[...], _memory_attention(qm_ref, kv_ref), x_ref, wo_ref, o_ref)


def _conv_out_proj_kernel(gb_ref, gc_ref, hh_ref, gc_prev_ref, hh_prev_ref, cw_ref,
                          qm_ref, kv_ref, x_ref, wo_ref, o_ref, *, tiles_per_seq):
    first = pl.program_id(0) % tiles_per_seq == 0
    u = gc_ref[...].astype(F32) * hh_ref[...].astype(F32)
    prev = gc_prev_ref[...].astype(F32) * hh_prev_ref[...].astype(F32)
    prev = jnp.where(first, 0.0, prev)
    tok = (gb_ref[...].astype(F32) * _causal_conv3(u, prev, cw_ref[...])).astype(BF16)
    _project_out(tok, _memory_attention(qm_ref, kv_ref), x_ref, wo_ref, o_ref)


def _out_proj(tok, proj, conv_w, kv, x, wo, *, seq, mem_tokens, tm):
    m, d = x.shape
    mem_width = kv.shape[1] // 2
    mix = d - mem_width
    tiles_per_seq = seq // tm
    qm_block = proj.shape[1] // mem_width - 1
    tail_specs = [
        pl.BlockSpec((tm, mem_width), lambda i: (i, qm_block)),
        pl.BlockSpec((mem_tokens, 2 * mem_width), lambda i: (i // tiles_per_seq, 0)),
        pl.BlockSpec((tm, d), lambda i: (i, 0)),
        pl.BlockSpec((d, d), lambda i: (0, 0)),
    ]
    if tok is not None:
        kernel = _out_proj_kernel
        head_specs = [pl.BlockSpec((tm, mix), lambda i: (i, 0))]
        head_args = (tok,)
    else:
        kernel = functools.partial(_conv_out_proj_kernel, tiles_per_seq=tiles_per_seq)
        halo = tm // SUBLANES

        def prev_rows(part):
            return pl.BlockSpec((SUBLANES, mix), lambda i: (jnp.maximum(i * halo - 1, 0), part))

        head_specs = [pl.BlockSpec((tm, mix), lambda i: (i, 0)),
                      pl.BlockSpec((tm, mix), lambda i: (i, 1)),
                      pl.BlockSpec((tm, mix), lambda i: (i, 2)),
                      prev_rows(1), prev_rows(2),
                      pl.BlockSpec((CONV_WIDTH, mix), lambda i: (0, 0))]
        head_args = (proj, proj, proj, proj, proj, conv_w)
    return pl.pallas_call(
        kernel,
        grid=(m // tm,),
        in_specs=head_specs + tail_specs,
        out_specs=pl.BlockSpec((tm, d), lambda i: (i, 0)),
        out_shape=jax.ShapeDtypeStruct((m, d), F32),
        compiler_params=_params("parallel"),
        name="out_proj",
    )(*head_args, proj, kv, x, wo)


def _conv_ffn_kernel(x_ref, g_ref, wg_ref, wu_ref, cwg_ref, cwu_ref, cbg_ref, cbu_ref, wd_ref,
                     fg_ref, o_ref, h_ref, carry_ref, *, tiles_per_seq, row_strips, final_norm):
    f = pl.program_id(1)
    first = pl.program_id(0) % tiles_per_seq == 0
    tm = x_ref.shape[0]

    @pl.when(f == 0)
    def _():
        x = x_ref[...]
        h_ref[...] = _rms_norm(x, g_ref[...]).astype(BF16)
        o_ref[...] = x

    strip = tm // row_strips
    ups = [[jnp.dot(h_ref[pl.ds(s * strip, strip), :], w_ref[...], preferred_element_type=F32)
            for w_ref in (wg_ref, wu_ref)] for s in range(row_strips)]
    for s in range(row_strips):
        branches = []
        for slot, (cw_ref, cb_ref) in enumerate(((cwg_ref, cbg_ref), (cwu_ref, cbu_ref))):
            if s == 0:
                prev = jnp.where(first, 0.0, carry_ref[f, slot])
            else:
                prev = ups[s - 1][slot][strip - SUBLANES:, :]
            branches.append(_causal_conv3(ups[s][slot], prev, cw_ref[...]) + cb_ref[...])
        act = (_silu(branches[0]) * branches[1]).astype(BF16)
        o_ref[pl.ds(s * strip, strip), :] += jnp.dot(act, wd_ref[...], preferred_element_type=F32)
    for slot in range(2):
        carry_ref[f, slot] = ups[row_strips - 1][slot][strip - SUBLANES:, :]

    if final_norm:
        @pl.when(f == pl.num_programs(1) - 1)
        def _():
            o_ref[...] = _rms_norm(o_ref[...], fg_ref[...])


def _conv_ffn(x, gain, w_up, conv_w, conv_b, w_down, final_gain, *, seq, tm, tf, final_norm):
    m, d = x.shape
    d_ff = w_down.shape[0]
    nf = d_ff // tf
    return pl.pallas_call(
        functools.partial(_conv_ffn_kernel, tiles_per_seq=seq // tm, row_strips=2,
                          final_norm=final_norm),
        grid=(m // tm, nf),
        in_specs=[
            pl.BlockSpec((tm, d), lambda i, f: (i, 0), pipeline_mode=pl.Buffered(1)),
            pl.BlockSpec((1, d), lambda i, f: (0, 0)),
            pl.BlockSpec((d, tf), lambda i, f: (0, f)),
            pl.BlockSpec((d, tf), lambda i, f: (0, nf + f)),
            pl.BlockSpec((CONV_WIDTH, tf), lambda i, f: (0, f)),
            pl.BlockSpec((CONV_WIDTH, tf), lambda i, f: (0, nf + f)),
            pl.BlockSpec((1, tf), lambda i, f: (0, f)),
            pl.BlockSpec((1, tf), lambda i, f: (0, nf + f)),
            pl.BlockSpec((tf, d), lambda i, f: (f, 0)),
            pl.BlockSpec((1, d), lambda i, f: (0, 0)),
        ],
        out_specs=pl.BlockSpec((tm, d), lambda i, f: (i, 0)),
        out_shape=jax.ShapeDtypeStruct((m, d), F32),
        scratch_shapes=[pltpu.VMEM((tm, d), BF16),
                        pltpu.VMEM((nf, 2, SUBLANES, tf), F32)],
        compiler_params=_params("arbitrary", "arbitrary"),
        name="conv_ffn",
    )(x, gain.reshape(1, d), w_up, w_up, conv_w, conv_w,
      conv_b.reshape(1, -1), conv_b.reshape(1, -1), w_down, final_gain.reshape(1, d))


def _half_tile(n):
    return n // 2 if n % 512 == 0 else n


def _trunk(x, mem, mem_norm, layers, final_norm):
    batch, seq, d = x.shape
    mem_tokens = mem.shape[1]
    x = x.reshape(batch * seq, d)
    mem = mem.reshape(batch * mem_tokens, d)
    n_layers = len(layers)
    for i, (norm_mix, w_in, extra, w_mem_kv, w_o, norm_ffn, w_up, cw, cb, w_down) in enumerate(layers):
        kind = i % 3
        mem_width = w_mem_kv.shape[1] // 2
        mix = d - mem_width
        w_in = w_in.astype(BF16)
        proj = _norm_matmul(x, norm_mix, w_in, tm=512, tn=_half_tile(w_in.shape[1]))
        kv = _norm_matmul(mem, mem_norm, w_mem_kv.astype(BF16), tm=mem_tokens,
                          tn=w_mem_kv.shape[1])
        if kind == 0:
            tok = _retention(proj, extra, batch=batch, seq=seq, rows=1024)
        elif kind == 1:
            tok = None
        else:
            tok = _moba(proj, batch=batch, seq=seq, d=mix // MOBA_HEADS, heads=2)
        x = _out_proj(tok, proj, extra, kv, x, w_o.astype(BF16),
                      seq=seq, mem_tokens=mem_tokens, tm=512)
        x = _conv_ffn(x, norm_ffn, w_up.astype(BF16), cw, cb, w_down.astype(BF16), final_norm,
                      seq=seq, tm=1024, tf=512, final_norm=(i == n_layers - 1))
    return x.reshape(batch, seq, d)


def kernel(x, mem, mem_norm,
           l0_norm_mix, l0_w_in, l0_ret_gn, l0_w_mem_kv, l0_w_o, l0_norm_ffn, l0_ffn_w_up, l0_ffn_conv_w, l0_ffn_conv_b, l0_ffn_w_down,
           l1_norm_mix, l1_w_in, l1_conv_w, l1_w_mem_kv, l1_w_o, l1_norm_ffn, l1_ffn_w_up, l1_ffn_conv_w, l1_ffn_conv_b, l1_ffn_w_down,
           l2_norm_mix, l2_w_in, l2_w_mem_kv, l2_w_o, l2_norm_ffn, l2_ffn_w_up, l2_ffn_conv_w, l2_ffn_conv_b, l2_ffn_w_down,
           l3_norm_mix, l3_w_in, l3_ret_gn, l3_w_mem_kv, l3_w_o, l3_norm_ffn, l3_ffn_w_up, l3_ffn_conv_w, l3_ffn_conv_b, l3_ffn_w_down,
           final_norm):
    layers = [
        (l0_norm_mix, l0_w_in, l0_ret_gn, l0_w_mem_kv, l0_w_o, l0_norm_ffn, l0_ffn_w_up, l0_ffn_conv_w, l0_ffn_conv_b, l0_ffn_w_down),
        (l1_norm_mix, l1_w_in, l1_conv_w, l1_w_mem_kv, l1_w_o, l1_norm_ffn, l1_ffn_w_up, l1_ffn_conv_w, l1_ffn_conv_b, l1_ffn_w_down),
        (l2_norm_mix, l2_w_in, None, l2_w_mem_kv, l2_w_o, l2_norm_ffn, l2_ffn_w_up, l2_ffn_conv_w, l2_ffn_conv_b, l2_ffn_w_down),
        (l3_norm_mix, l3_w_in, l3_ret_gn, l3_w_mem_kv, l3_w_o, l3_norm_ffn, l3_ffn_w_up, l3_ffn_conv_w, l3_ffn_conv_b, l3_ffn_w_down),
    ]
    return _trunk(x, mem, mem_norm, layers, final_norm)
```

```python
import functools
import math

import jax
import jax.numpy as jnp
import numpy as np
from jax import lax
from jax.experimental import pallas as pl
from jax.experimental.pallas import tpu as pltpu

F32 = jnp.float32
BF16 = jnp.bfloat16

NORM_EPS = 1e-6
MEM_HEADS = 4
RET_HEADS = 6
RET_CHUNK = 256
CONV_WIDTH = 3
MOBA_HEADS = 12
MOBA_BLOCK = 256
MOBA_TOPK = 3

SUBLANES = 8
VMEM_LIMIT_BYTES = 56 * 1024 * 1024

_NT_DIMS = (((1,), (1,)), ((), ()))
_TN_DIMS = (((0,), (0,)), ((), ()))


def _params(*semantics):
    return pltpu.CompilerParams(dimension_semantics=semantics,
                                vmem_limit_bytes=VMEM_LIMIT_BYTES)


def _rms_norm(x, gain):
    return x * lax.rsqrt(jnp.mean(x * x, axis=-1, keepdims=True) + NORM_EPS) * gain


def _silu(g):
    return g * (1.0 / (1.0 + jnp.exp(-g)))


def _shift_rows(u, prev, k):
    rolled = pltpu.roll(u, k, axis=0)
    prev_rolled = pltpu.roll(prev, k, axis=0)
    row = lax.broadcasted_iota(jnp.int32, prev.shape, 0)
    head = jnp.where(row < k, prev_rolled, rolled[:SUBLANES])
    return jnp.concatenate([head, rolled[SUBLANES:]], axis=0)


def _causal_conv3(u, prev, w):
    return (w[0:1] * _shift_rows(u, prev, 2) + w[1:2] * _shift_rows(u, prev, 1)
            + w[2:3] * u)


def _norm_matmul_kernel(x_ref, g_ref, w_ref, o_ref):
    h = _rms_norm(x_ref[...], g_ref[...]).astype(BF16)
    o_ref[...] = jnp.dot(h, w_ref[...], preferred_element_type=F32).astype(o_ref.dtype)


def _norm_matmul(x, gain, w, *, tm, tn):
    m, d = x.shape
    n = w.shape[1]
    return pl.pallas_call(
        _norm_matmul_kernel,
        grid=(n // tn, m // tm),
        in_specs=[
            pl.BlockSpec((tm, d), lambda j, i: (i, 0)),
            pl.BlockSpec((1, d), lambda j, i: (0, 0)),
            pl.BlockSpec((d, tn), lambda j, i: (0, j)),
        ],
        out_specs=pl.BlockSpec((tm, tn), lambda j, i: (i, j)),
        out_shape=jax.ShapeDtypeStruct((m, n), BF16),
        compiler_params=_params("parallel", "parallel"),
        name="norm_matmul",
    )(x, gain.reshape(1, d), w)


def _retention_kernel(lg_ref, q_ref, k_ref, v_ref, g_ref, gn_ref, o_ref, state_ref, *, chunk):
    lg = lg_ref[pl.program_id(1)]
    rows, d = q_ref.shape
    c = chunk
    k_scale = d ** -0.5

    @pl.when(pl.program_id(2) == 0)
    def _():
        state_ref[...] = jnp.zeros_like(state_ref)

    diff = (lax.broadcasted_iota(jnp.int32, (c, c), 0)
            - lax.broadcasted_iota(jnp.int32, (c, c), 1)).astype(F32)
    intra = jnp.where(diff >= 0, jnp.exp(jnp.maximum(diff, 0.0) * lg), 0.0) * k_scale
    pos = lax.broadcasted_iota(jnp.int32, (c, d), 0).astype(F32)
    q_decay = jnp.exp((pos + 1.0) * lg)
    k_decay = jnp.exp((c - 1.0 - pos) * lg) * k_scale
    chunk_decay = jnp.exp(jnp.zeros((1, d), F32) + c * lg)
    gn = gn_ref[...]

    for s in range(rows // c):
        sl = pl.ds(s * c, c)
        q = q_ref[sl, :]
        k = k_ref[sl, :]
        v = v_ref[sl, :]
        scores = lax.dot_general(q, k, _NT_DIMS, preferred_element_type=F32) * intra
        y = jnp.dot(scores.astype(BF16), v, preferred_element_type=F32)
        state = state_ref[...]
        y = y + jnp.dot(q, state.astype(BF16), preferred_element_type=F32) * q_decay
        kd = (k.astype(F32) * k_decay).astype(BF16)
        kv = lax.dot_general(kd, v, _TN_DIMS, preferred_element_type=F32)
        state_ref[...] = state * chunk_decay + kv
        g = g_ref[sl, :].astype(F32)
        o_ref[sl, :] = (_silu(g) * _rms_norm(y, gn)).astype(o_ref.dtype)


def _retention(proj, gn_gain, *, batch, seq, rows):
    m = proj.shape[0]
    d = gn_gain.shape[0] // RET_HEADS
    nt = seq // rows
    lg = np.log1p(-np.exp2(-5.0 - np.arange(RET_HEADS))).astype(np.float32)

    def spec(part):
        return pl.BlockSpec((rows, d), lambda b, h, t, lg_ref: (b * nt + t, part * RET_HEADS + h))

    return pl.pallas_call(
        functools.partial(_retention_kernel, chunk=RET_CHUNK),
        grid_spec=pltpu.PrefetchScalarGridSpec(
            num_scalar_prefetch=1,
            grid=(batch, RET_HEADS, nt),
            in_specs=[spec(0), spec(1), spec(2), spec(3),
                      pl.BlockSpec((1, d), lambda b, h, t, lg_ref: (0, h))],
            out_specs=pl.BlockSpec((rows, d), lambda b, h, t, lg_ref: (b * nt + t, h)),
            scratch_shapes=[pltpu.VMEM((d, d), F32)],
        ),
        out_shape=jax.ShapeDtypeStruct((m, RET_HEADS * d), BF16),
        compiler_params=_params("parallel", "parallel", "arbitrary"),
        name="retention",
    )(jnp.asarray(lg), proj, proj, proj, proj, gn_gain.reshape(1, -1))


def _alibi_slopes(n):
    def pow2_slopes(m):
        start = 2.0 ** (-8.0 / m)
        return [start ** (i + 1) for i in range(m)]
    if math.log2(n).is_integer():
        s = pow2_slopes(n)
    else:
        c = 2 ** int(math.floor(math.log2(n)))
        s = pow2_slopes(c) + list(_alibi_slopes(2 * c))[0::2][: n - c]
    return np.asarray(s, dtype=np.float32)


MOBA_MASKED = -1e30
AUG_PARTS = 3
AUG_PART_LANE = SUBLANES


def _split_bf16(x):
    parts = []
    for _ in range(AUG_PARTS):
        part = x.astype(BF16).astype(F32)
        parts.append(part)
        x = x - part
    return parts


def _moba_kernel(slope_ref, q_ref, k_ref, v_ref, o_ref, kmean_ref, vt_ref, kaug_ref, *, heads, group):
    qi = pl.program_id(2)
    blk = q_ref.shape[0]
    d = q_ref.shape[1] // heads
    nblk = k_ref.shape[0] // blk
    scale = d ** -0.5
    exp2_scale = scale * math.log2(math.e)
    assert AUG_PART_LANE + AUG_PARTS * nblk <= d

    def head_cols(t):
        return slice(t * d, (t + 1) * d)

    def head_slope(t):
        return slope_ref[pl.program_id(1) * heads + t]

    @pl.when(qi == 0)
    def _():
        lane = lax.broadcasted_iota(jnp.int32, (blk, d), 1)
        row = lax.broadcasted_iota(jnp.int32, (blk, d), 0)
        for t in range(heads):
            key_scale = head_slope(t) * (1.0 / scale)
            for j in range(nblk):
                rows = slice(j * blk, (j + 1) * blk)
                kmean_ref[t, j:j + 1, :] = jnp.sum(k_ref[rows, head_cols(t)].astype(F32), axis=0,
                                                   keepdims=True) * (1.0 / blk)
                vt_ref[t, j] = v_ref[rows, head_cols(t)].astype(F32).T.astype(BF16)
                aug = jnp.zeros((blk, d), F32)
                for p, part in enumerate(_split_bf16((row + j * blk).astype(F32) * key_scale)):
                    aug = jnp.where(lane == p, part, aug)
                    aug = jnp.where(lane == AUG_PART_LANE + p * nblk + j, 1.0, aug)
                kaug_ref[t, j] = aug.astype(BF16)

    def query_operand(t):
        q = q_ref[:, head_cols(t)]
        kmean = kmean_ref[t]
        kmean_hi = kmean.astype(BF16)
        kmean_lo = (kmean - kmean_hi.astype(F32)).astype(BF16)
        gate = (lax.dot_general(kmean_hi, q, _NT_DIMS, preferred_element_type=F32)
                + lax.dot_general(kmean_lo, q, _NT_DIMS, preferred_element_type=F32))
        blk_id = lax.broadcasted_iota(jnp.int32, gate.shape, 0)
        rank = jnp.zeros(gate.shape, jnp.int32)
        for other in range(nblk):
            g_other = gate[other:other + 1, :]
            beats = (g_other > gate) | ((g_other == gate) & (other < blk_id))
            rank = rank + jnp.where(other < qi, jnp.where(beats, 1, 0), 0)
        keep = ((blk_id < qi) & (rank < MOBA_TOPK)) | (blk_id == qi)
        t_query = (qi * blk + lax.broadcasted_iota(jnp.int32, gate.shape, 1)).astype(F32)
        bias = (jnp.where(keep, 0.0, MOBA_MASKED) - head_slope(t) * t_query) * (1.0 / scale)
        ones = jnp.where(lax.broadcasted_iota(jnp.int32, (AUG_PART_LANE, blk), 0) < AUG_PARTS,
                         1.0, 0.0)
        pad = jnp.zeros((d - AUG_PART_LANE - AUG_PARTS * nblk, blk), F32)
        aug = jnp.concatenate([ones] + _split_bf16(bias) + [pad], axis=0)
        return jnp.concatenate([q, aug.T.astype(BF16)], axis=1)

    def scores(t, q_op, j):
        rows = pl.ds(pl.multiple_of(j * blk, blk), blk)
        k_op = jnp.concatenate([k_ref[rows, head_cols(t)], kaug_ref[t, j]], axis=1)
        return lax.dot_general(k_op, q_op, _NT_DIMS, preferred_element_type=F32)

    def attend(t, s_list, j_list, state):
        m_new = functools.reduce(jnp.maximum, [jnp.max(s, axis=0, keepdims=True) for s in s_list])
        if state is not None:
            m, l, acc = state
            m_new = jnp.maximum(m, m_new)
        ps = [jnp.exp2((s - m_new) * exp2_scale) for s in s_list]
        l_new = functools.reduce(jnp.add, [jnp.sum(p, axis=0, keepdims=True) for p in ps])
        p = jnp.concatenate([p.astype(BF16) for p in ps], axis=0)
        vt = jnp.concatenate([vt_ref[t, j] for j in j_list], axis=1)
        acc_new = jnp.dot(vt, p, preferred_element_type=F32)
        if state is not None:
            alpha = jnp.exp2((m - m_new) * exp2_scale)
            l_new = alpha * l + l_new
            acc_new = alpha * acc + acc_new
        return m_new, l_new, acc_new

    q_ops = [query_operand(t) for t in range(heads)]

    own_group = qi // group
    key_minus_query = (lax.broadcasted_iota(jnp.int32, (blk, blk), 0)
                       - lax.broadcasted_iota(jnp.int32, (blk, blk), 1))
    own_js = [own_group * group + i for i in range(group)]
    state = tuple(
        attend(t, [jnp.where(key_minus_query <= (qi - j) * blk, scores(t, q_ops[t], j), MOBA_MASKED)
                   for j in own_js], own_js, None)
        for t in range(heads))

    def past_group(g, state):
        js = [g * group + i for i in range(group)]
        return tuple(attend(t, [scores(t, q_ops[t], j) for j in js], js, state[t])
                     for t in range(heads))

    state = lax.fori_loop(0, own_group, past_group, state)
    for t in range(heads):
        _, l, acc = state[t]
        o_ref[:, head_cols(t)] = (acc / l).T.astype(o_ref.dtype)


def _moba(proj, *, batch, seq, d, heads, group):
    m = proj.shape[0]
    blk = MOBA_BLOCK
    assert seq % (group * blk) == 0 and MOBA_HEADS % heads == 0
    nq = seq // blk
    groups = MOBA_HEADS // heads
    slopes = _alibi_slopes(MOBA_HEADS)
    return pl.pallas_call(
        functools.partial(_moba_kernel, heads=heads, group=group),
        grid_spec=pltpu.PrefetchScalarGridSpec(
            num_scalar_prefetch=1,
            grid=(batch, groups, nq),
            in_specs=[
                pl.BlockSpec((blk, heads * d), lambda b, h, i, s_ref: (b * nq + i, h)),
                pl.BlockSpec((seq, heads * d), lambda b, h, i, s_ref: (b, groups + h)),
                pl.BlockSpec((seq, heads * d), lambda b, h, i, s_ref: (b, 2 * groups + h)),
            ],
            out_specs=pl.BlockSpec((blk, heads * d), lambda b, h, i, s_ref: (b * nq + i, h)),
            scratch_shapes=[pltpu.VMEM((heads, nq, d), F32),
                            pltpu.VMEM((heads, nq, d, blk), BF16),
                            pltpu.VMEM((heads, nq, blk, d), BF16)],
        ),
        out_shape=jax.ShapeDtypeStruct((m, MOBA_HEADS * d), BF16),
        compiler_params=_params("parallel", "parallel", "arbitrary"),
        name="moba",
    )(jnp.asarray(slopes), proj, proj, proj)


def _memory_attention(qm_ref, kv_ref):
    width = qm_ref.shape[1]
    dh = width // MEM_HEADS
    outs = []
    for h in range(MEM_HEADS):
        q = qm_ref[:, h * dh:(h + 1) * dh]
        k = kv_ref[:, h * dh:(h + 1) * dh]
        v = kv_ref[:, width + h * dh:width + (h + 1) * dh]
        s = lax.dot_general(q, k, _NT_DIMS, preferred_element_type=F32) * dh ** -0.5
        p = jnp.exp(s - jnp.max(s, axis=-1, keepdims=True))
        o = jnp.dot(p.astype(BF16), v, preferred_element_type=F32)
        outs.append((o / jnp.sum(p, axis=-1, keepdims=True)).astype(BF16))
    return jnp.concatenate(outs, axis=-1)


def _project_out(tok, mem_out, x_ref, wo_ref, o_ref):
    mix = tok.shape[1]
    o_ref[...] = (x_ref[...]
                  + jnp.dot(tok, wo_ref[:mix, :], preferred_element_type=F32)
                  + jnp.dot(mem_out, wo_ref[mix:, :], preferred_element_type=F32))


def _out_proj_kernel(tok_ref, qm_ref, kv_ref, x_ref, wo_ref, o_ref):
    _project_out(tok_ref[...], _memory_attention(qm_ref, kv_ref), x_ref, wo_ref, o_ref)


def _conv_out_proj_kernel(gb_ref, gc_ref, hh_ref, gc_prev_ref, hh_prev_ref, cw_ref,
                          qm_ref, kv_ref, x_ref, wo_ref, o_ref, *, tiles_per_seq):
    first = pl.program_id(0) % tiles_per_seq == 0
    u = gc_ref[...].astype(F32) * hh_ref[...].astype(F32)
    prev = gc_prev_ref[...].astype(F32) * hh_prev_ref[...].astype(F32)
    prev = jnp.where(first, 0.0, prev)
    tok = (gb_ref[...].astype(F32) * _causal_conv3(u, prev, cw_ref[...])).astype(BF16)
    _project_out(tok, _memory_attention(qm_ref, kv_ref), x_ref, wo_ref, o_ref)


def _out_proj(tok, proj, conv_w, kv, x, wo, *, seq, mem_tokens, tm):
    m, d = x.shape
    mem_width = kv.shape[1] // 2
    mix = d - mem_width
    tiles_per_seq = seq // tm
    qm_block = proj.shape[1] // mem_width - 1
    tail_specs = [
        pl.BlockSpec((tm, mem_width), lambda i: (i, qm_block)),
        pl.BlockSpec((mem_tokens, 2 * mem_width), lambda i: (i // tiles_per_seq, 0)),
        pl.BlockSpec((tm, d), lambda i: (i, 0)),
        pl.BlockSpec((d, d), lambda i: (0, 0)),
    ]
    if tok is not None:
        kernel = _out_proj_kernel
        head_specs = [pl.BlockSpec((tm, mix), lambda i: (i, 0))]
        head_args = (tok,)
    else:
        kernel = functools.partial(_conv_out_proj_kernel, tiles_per_seq=tiles_per_seq)
        halo = tm // SUBLANES

        def prev_rows(part):
            return pl.BlockSpec((SUBLANES, mix), lambda i: (jnp.maximum(i * halo - 1, 0), part))

        head_specs = [pl.BlockSpec((tm, mix), lambda i: (i, 0)),
                      pl.BlockSpec((tm, mix), lambda i: (i, 1)),
                      pl.BlockSpec((tm, mix), lambda i: (i, 2)),
                      prev_rows(1), prev_rows(2),
                      pl.BlockSpec((CONV_WIDTH, mix), lambda i: (0, 0))]
        head_args = (proj, proj, proj, proj, proj, conv_w)
    return pl.pallas_call(
        kernel,
        grid=(m // tm,),
        in_specs=head_specs + tail_specs,
        out_specs=pl.BlockSpec((tm, d), lambda i: (i, 0)),
        out_shape=jax.ShapeDtypeStruct((m, d), F32),
        compiler_params=_params("parallel"),
        name="out_proj",
    )(*head_args, proj, kv, x, wo)


def _conv_ffn_kernel(x_ref, g_ref, wg_ref, wu_ref, conv_ref, wd_ref,
                     fg_ref, o_ref, h_ref, carry_ref, *, tiles_per_seq, row_strips, final_norm):
    f = pl.program_id(1)
    nf = pl.num_programs(1)
    first = pl.program_id(0) % tiles_per_seq == 0
    tm = x_ref.shape[0]

    @pl.when(f == 0)
    def _():
        x = x_ref[...]
        h_ref[...] = _rms_norm(x, g_ref[...]).astype(BF16)
        o_ref[...] = x

    strip = tm // row_strips
    ups = [[jnp.dot(h_ref[pl.ds(s * strip, strip), :], w_ref[...], preferred_element_type=F32)
            for w_ref in (wg_ref, wu_ref)] for s in range(row_strips)]
    conv = [conv_ref[f + slot * nf] for slot in range(2)]
    for s in range(row_strips):
        branches = []
        for slot in range(2):
            if s == 0:
                prev = jnp.where(first, 0.0, carry_ref[f, slot])
            else:
                prev = ups[s - 1][slot][strip - SUBLANES:, :]
            branches.append(_causal_conv3(ups[s][slot], prev, conv[slot][:CONV_WIDTH])
                            + conv[slot][CONV_WIDTH:CONV_WIDTH + 1])
        act = (_silu(branches[0]) * branches[1]).astype(BF16)
        o_ref[pl.ds(s * strip, strip), :] += jnp.dot(act, wd_ref[...], preferred_element_type=F32)
    for slot in range(2):
        carry_ref[f, slot] = ups[row_strips - 1][slot][strip - SUBLANES:, :]

    if final_norm:
        @pl.when(f == pl.num_programs(1) - 1)
        def _():
            o_ref[...] = _rms_norm(o_ref[...], fg_ref[...])


def _conv_ffn(x, gain, w_up, conv_w, conv_b, w_down, final_gain, *, seq, tm, tf, final_norm):
    m, d = x.shape
    d_ff = w_down.shape[0]
    nf = d_ff // tf
    conv = jnp.concatenate(
        [conv_w.reshape(CONV_WIDTH, 2 * nf, tf).transpose(1, 0, 2),
         conv_b.reshape(2 * nf, 1, tf),
         jnp.zeros((2 * nf, SUBLANES - CONV_WIDTH - 1, tf), F32)], axis=1)
    return pl.pallas_call(
        functools.partial(_conv_ffn_kernel, tiles_per_seq=seq // tm, row_strips=2,
                          final_norm=final_norm),
        grid=(m // tm, nf),
        in_specs=[
            pl.BlockSpec((tm, d), lambda i, f: (i, 0)),
            pl.BlockSpec((1, d), lambda i, f: (0, 0)),
            pl.BlockSpec((d, tf), lambda i, f: (0, f)),
            pl.BlockSpec((d, tf), lambda i, f: (0, nf + f)),
            pl.BlockSpec((2 * nf, SUBLANES, tf), lambda i, f: (0, 0, 0)),
            pl.BlockSpec((tf, d), lambda i, f: (f, 0)),
            pl.BlockSpec((1, d), lambda i, f: (0, 0)),
        ],
        out_specs=pl.BlockSpec((tm, d), lambda i, f: (i, 0)),
        out_shape=jax.ShapeDtypeStruct((m, d), F32),
        scratch_shapes=[pltpu.VMEM((tm, d), BF16),
                        pltpu.VMEM((nf, 2, SUBLANES, tf), F32)],
        compiler_params=_params("arbitrary", "arbitrary"),
        name="conv_ffn",
    )(x, gain.reshape(1, d), w_up, w_up, conv, w_down, final_gain.reshape(1, d))


def _half_tile(n):
    return n // 2 if n % 512 == 0 else n


def _trunk(x, mem, mem_norm, layers, final_norm):
    batch, seq, d = x.shape
    mem_tokens = mem.shape[1]
    x = x.reshape(batch * seq, d)
    mem = mem.reshape(batch * mem_tokens, d)
    n_layers = len(layers)
    for i, (norm_mix, w_in, extra, w_mem_kv, w_o, norm_ffn, w_up, cw, cb, w_down) in enumerate(layers):
        kind = i % 3
        mem_width = w_mem_kv.shape[1] // 2
        mix = d - mem_width
        w_in = w_in.astype(BF16)
        proj = _norm_matmul(x, norm_mix, w_in, tm=512, tn=_half_tile(w_in.shape[1]))
        kv = _norm_matmul(mem, mem_norm, w_mem_kv.astype(BF16), tm=mem_tokens,
                          tn=w_mem_kv.shape[1])
        if kind == 0:
            tok = _retention(proj, extra, batch=batch, seq=seq, rows=1024)
        elif kind == 1:
            tok = None
        else:
            tok = _moba(proj, batch=batch, seq=seq, d=mix // MOBA_HEADS, heads=2, group=4)
        x = _out_proj(tok, proj, extra, kv, x, w_o.astype(BF16),
                      seq=seq, mem_tokens=mem_tokens, tm=512)
        x = _conv_ffn(x, norm_ffn, w_up.astype(BF16), cw, cb, w_down.astype(BF16), final_norm,
                      seq=seq, tm=1024, tf=512, final_norm=(i == n_layers - 1))
    return x.reshape(batch, seq, d)


def kernel(x, mem, mem_norm,
           l0_norm_mix, l0_w_in, l0_ret_gn, l0_w_mem_kv, l0_w_o, l0_norm_ffn, l0_ffn_w_up, l0_ffn_conv_w, l0_ffn_conv_b, l0_ffn_w_down,
           l1_norm_mix, l1_w_in, l1_conv_w, l1_w_mem_kv, l1_w_o, l1_norm_ffn, l1_ffn_w_up, l1_ffn_conv_w, l1_ffn_conv_b, l1_ffn_w_down,
           l2_norm_mix, l2_w_in, l2_w_mem_kv, l2_w_o, l2_norm_ffn, l2_ffn_w_up, l2_ffn_conv_w, l2_ffn_conv_b, l2_ffn_w_down,
           l3_norm_mix, l3_w_in, l3_ret_gn, l3_w_mem_kv, l3_w_o, l3_norm_ffn, l3_ffn_w_up, l3_ffn_conv_w, l3_ffn_conv_b, l3_ffn_w_down,
           final_norm):
    layers = [
        (l0_norm_mix, l0_w_in, l0_ret_gn, l0_w_mem_kv, l0_w_o, l0_norm_ffn, l0_ffn_w_up, l0_ffn_conv_w, l0_ffn_conv_b, l0_ffn_w_down),
        (l1_norm_mix, l1_w_in, l1_conv_w, l1_w_mem_kv, l1_w_o, l1_norm_ffn, l1_ffn_w_up, l1_ffn_conv_w, l1_ffn_conv_b, l1_ffn_w_down),
        (l2_norm_mix, l2_w_in, None, l2_w_mem_kv, l2_w_o, l2_norm_ffn, l2_ffn_w_up, l2_ffn_conv_w, l2_ffn_conv_b, l2_ffn_w_down),
        (l3_norm_mix, l3_w_in, l3_ret_gn, l3_w_mem_kv, l3_w_o, l3_norm_ffn, l3_ffn_w_up, l3_ffn_conv_w, l3_ffn_conv_b, l3_ffn_w_down),
    ]
    return _trunk(x, mem, mem_norm, layers, final_norm)
```

```python
import functools
import math

import jax
import jax.numpy as jnp
import numpy as np
from jax import lax
from jax.experimental import pallas as pl
from jax.experimental.pallas import tpu as pltpu

F32 = jnp.float32
BF16 = jnp.bfloat16

NORM_EPS = 1e-6
MEM_HEADS = 4
RET_HEADS = 6
RET_CHUNK = 256
CONV_WIDTH = 3
MOBA_HEADS = 12
MOBA_BLOCK = 256
MOBA_TOPK = 3

SUBLANES = 8
VMEM_LIMIT_BYTES = 56 * 1024 * 1024

_NT_DIMS = (((1,), (1,)), ((), ()))
_TN_DIMS = (((0,), (0,)), ((), ()))


def _params(*semantics):
    return pltpu.CompilerParams(dimension_semantics=semantics,
                                vmem_limit_bytes=VMEM_LIMIT_BYTES)


def _rms_norm(x, gain):
    return x * lax.rsqrt(jnp.mean(x * x, axis=-1, keepdims=True) + NORM_EPS) * gain


def _silu(g):
    return g * (1.0 / (1.0 + jnp.exp(-g)))


def _shift_rows(u, prev, k):
    rolled = pltpu.roll(u, k, axis=0)
    prev_rolled = pltpu.roll(prev, k, axis=0)
    row = lax.broadcasted_iota(jnp.int32, prev.shape, 0)
    head = jnp.where(row < k, prev_rolled, rolled[:SUBLANES])
    return jnp.concatenate([head, rolled[SUBLANES:]], axis=0)


def _causal_conv3(u, prev, w):
    return (w[0:1] * _shift_rows(u, prev, 2) + w[1:2] * _shift_rows(u, prev, 1)
            + w[2:3] * u)


def _norm_matmul_kernel(x_ref, g_ref, w_ref, o_ref):
    h = _rms_norm(x_ref[...], g_ref[...]).astype(BF16)
    o_ref[...] = jnp.dot(h, w_ref[...], preferred_element_type=F32).astype(o_ref.dtype)


def _norm_matmul(x, gain, w, *, tm, tn):
    m, d = x.shape
    n = w.shape[1]
    return pl.pallas_call(
        _norm_matmul_kernel,
        grid=(n // tn, m // tm),
        in_specs=[
            pl.BlockSpec((tm, d), lambda j, i: (i, 0)),
            pl.BlockSpec((1, d), lambda j, i: (0, 0)),
            pl.BlockSpec((d, tn), lambda j, i: (0, j)),
        ],
        out_specs=pl.BlockSpec((tm, tn), lambda j, i: (i, j)),
        out_shape=jax.ShapeDtypeStruct((m, n), BF16),
        compiler_params=_params("parallel", "parallel"),
        name="norm_matmul",
    )(x, gain.reshape(1, d), w)


def _retention_kernel(lg_ref, q_ref, k_ref, v_ref, g_ref, gn_ref, o_ref, state_ref, *, chunk):
    lg = lg_ref[pl.program_id(1)]
    rows, d = q_ref.shape
    c = chunk
    k_scale = d ** -0.5

    @pl.when(pl.program_id(2) == 0)
    def _():
        state_ref[...] = jnp.zeros_like(state_ref)

    diff = (lax.broadcasted_iota(jnp.int32, (c, c), 0)
            - lax.broadcasted_iota(jnp.int32, (c, c), 1)).astype(F32)
    intra = jnp.where(diff >= 0, jnp.exp(jnp.maximum(diff, 0.0) * lg), 0.0) * k_scale
    pos = lax.broadcasted_iota(jnp.int32, (c, d), 0).astype(F32)
    q_decay = jnp.exp((pos + 1.0) * lg)
    k_decay = jnp.exp((c - 1.0 - pos) * lg) * k_scale
    chunk_decay = jnp.exp(jnp.zeros((1, d), F32) + c * lg)
    gn = gn_ref[...]

    for s in range(rows // c):
        sl = pl.ds(s * c, c)
        q = q_ref[sl, :]
        k = k_ref[sl, :]
        v = v_ref[sl, :]
        scores = lax.dot_general(q, k, _NT_DIMS, preferred_element_type=F32) * intra
        y = jnp.dot(scores.astype(BF16), v, preferred_element_type=F32)
        state = state_ref[...]
        y = y + jnp.dot(q, state.astype(BF16), preferred_element_type=F32) * q_decay
        kd = (k.astype(F32) * k_decay).astype(BF16)
        kv = lax.dot_general(kd, v, _TN_DIMS, preferred_element_type=F32)
        state_ref[...] = state * chunk_decay + kv
        g = g_ref[sl, :].astype(F32)
        o_ref[sl, :] = (_silu(g) * _rms_norm(y, gn)).astype(o_ref.dtype)


def _retention(proj, gn_gain, *, batch, seq, rows):
    m = proj.shape[0]
    d = gn_gain.shape[0] // RET_HEADS
    nt = seq // rows
    lg = np.log1p(-np.exp2(-5.0 - np.arange(RET_HEADS))).astype(np.float32)

    def spec(part):
        return pl.BlockSpec((rows, d), lambda b, h, t, lg_ref: (b * nt + t, part * RET_HEADS + h))

    return pl.pallas_call(
        functools.partial(_retention_kernel, chunk=RET_CHUNK),
        grid_spec=pltpu.PrefetchScalarGridSpec(
            num_scalar_prefetch=1,
            grid=(batch, RET_HEADS, nt),
            in_specs=[spec(0), spec(1), spec(2), spec(3),
                      pl.BlockSpec((1, d), lambda b, h, t, lg_ref: (0, h))],
            out_specs=pl.BlockSpec((rows, d), lambda b, h, t, lg_ref: (b * nt + t, h)),
            scratch_shapes=[pltpu.VMEM((d, d), F32)],
        ),
        out_shape=jax.ShapeDtypeStruct((m, RET_HEADS * d), BF16),
        compiler_params=_params("parallel", "parallel", "arbitrary"),
        name="retention",
    )(jnp.asarray(lg), proj, proj, proj, proj, gn_gain.reshape(1, -1))


def _alibi_slopes(n):
    def pow2_slopes(m):
        start = 2.0 ** (-8.0 / m)
        return [start ** (i + 1) for i in range(m)]
    if math.log2(n).is_integer():
        s = pow2_slopes(n)
    else:
        c = 2 ** int(math.floor(math.log2(n)))
        s = pow2_slopes(c) + list(_alibi_slopes(2 * c))[0::2][: n - c]
    return np.asarray(s, dtype=np.float32)


MOBA_MASKED = -1e30
AUG_PARTS = 3
AUG_PART_LANE = SUBLANES


def _split_bf16(x):
    parts = []
    for _ in range(AUG_PARTS):
        part = x.astype(BF16).astype(F32)
        parts.append(part)
        x = x - part
    return parts


def _moba_kernel(slope_ref, q_ref, k_ref, v_ref, o_ref, kmean_ref, vt_ref, kaug_ref, s_ref, *,
                 heads, group):
    qi = pl.program_id(2)
    blk = q_ref.shape[0]
    d = q_ref.shape[1] // heads
    nblk = k_ref.shape[0] // blk
    scale = d ** -0.5
    exp2_scale = scale * math.log2(math.e)
    assert AUG_PART_LANE + AUG_PARTS * nblk <= d

    def head_cols(t):
        return slice(t * d, (t + 1) * d)

    def head_slope(t):
        return slope_ref[pl.program_id(1) * heads + t]

    @pl.when(qi == 0)
    def _():
        lane = lax.broadcasted_iota(jnp.int32, (blk, d), 1)
        row = lax.broadcasted_iota(jnp.int32, (blk, d), 0)
        for t in range(heads):
            key_scale = head_slope(t) * (1.0 / scale)
            for j in range(nblk):
                rows = slice(j * blk, (j + 1) * blk)
                kmean_ref[t, j:j + 1, :] = jnp.sum(k_ref[rows, head_cols(t)].astype(F32), axis=0,
                                                   keepdims=True) * (1.0 / blk)
                vt_ref[t, j] = v_ref[rows, head_cols(t)].astype(F32).T.astype(BF16)
                aug = jnp.zeros((blk, d), F32)
                for p, part in enumerate(_split_bf16((row + j * blk).astype(F32) * key_scale)):
                    aug = jnp.where(lane == p, part, aug)
                    aug = jnp.where(lane == AUG_PART_LANE + p * nblk + j, 1.0, aug)
                kaug_ref[t, j] = aug.astype(BF16)

    def query_operand(t):
        q = q_ref[:, head_cols(t)]
        kmean = kmean_ref[t]
        kmean_hi = kmean.astype(BF16)
        kmean_lo = (kmean - kmean_hi.astype(F32)).astype(BF16)
        gate = (lax.dot_general(kmean_hi, q, _NT_DIMS, preferred_element_type=F32)
                + lax.dot_general(kmean_lo, q, _NT_DIMS, preferred_element_type=F32))
        blk_id = lax.broadcasted_iota(jnp.int32, gate.shape, 0)
        rank = jnp.zeros(gate.shape, jnp.int32)
        for other in range(nblk):
            g_other = gate[other:other + 1, :]
            beats = (g_other > gate) | ((g_other == gate) & (other < blk_id))
            rank = rank + jnp.where(other < qi, jnp.where(beats, 1, 0), 0)
        keep = ((blk_id < qi) & (rank < MOBA_TOPK)) | (blk_id == qi)
        t_query = (qi * blk + lax.broadcasted_iota(jnp.int32, gate.shape, 1)).astype(F32)
        bias = (jnp.where(keep, 0.0, MOBA_MASKED) - head_slope(t) * t_query) * (1.0 / scale)
        ones = jnp.where(lax.broadcasted_iota(jnp.int32, (AUG_PART_LANE, blk), 0) < AUG_PARTS,
                         1.0, 0.0)
        pad = jnp.zeros((d - AUG_PART_LANE - AUG_PARTS * nblk, blk), F32)
        aug = jnp.concatenate([ones] + _split_bf16(bias) + [pad], axis=0)
        return jnp.concatenate([q, aug.T.astype(BF16)], axis=1)

    def scores(t, q_op, j):
        rows = pl.ds(pl.multiple_of(j * blk, blk), blk)
        k_op = jnp.concatenate([k_ref[rows, head_cols(t)], kaug_ref[t, j]], axis=1)
        return lax.dot_general(k_op, q_op, _NT_DIMS, preferred_element_type=F32)

    q_ops = [query_operand(t) for t in range(heads)]

    own_group = qi // group
    key_minus_query = (lax.broadcasted_iota(jnp.int32, (blk, blk), 0)
                       - lax.broadcasted_iota(jnp.int32, (blk, blk), 1))

    def group_scores(t, g, causal):
        best = None
        for i in range(group):
            j = g * group + i
            s = scores(t, q_ops[t], j) * exp2_scale
            if causal:
                s = jnp.where(key_minus_query <= (qi - j) * blk, s, MOBA_MASKED)
            s_ref[t, g, i] = s
            block_max = jnp.max(s, axis=0, keepdims=True)
            best = block_max if best is None else jnp.maximum(best, block_max)
        return best

    def past_scores(g, best):
        return tuple(jnp.maximum(best[t], group_scores(t, g, False)) for t in range(heads))

    best = lax.fori_loop(0, own_group, past_scores,
                         tuple(group_scores(t, own_group, True) for t in range(heads)))

    def group_values(g, carry):
        out = []
        for t in range(heads):
            l, acc = carry[t]
            ps = [jnp.exp2(s_ref[t, g, i] - best[t]) for i in range(group)]
            l = l + functools.reduce(jnp.add, [jnp.sum(p, axis=0, keepdims=True) for p in ps])
            p = jnp.concatenate([p.astype(BF16) for p in ps], axis=0)
            vt = jnp.concatenate([vt_ref[t, g * group + i] for i in range(group)], axis=1)
            out.append((l, acc + jnp.dot(vt, p, preferred_element_type=F32)))
        return tuple(out)

    zero = (jnp.zeros((1, blk), F32), jnp.zeros((d, blk), F32))
    totals = lax.fori_loop(0, own_group + 1, group_values, (zero,) * heads)
    for t in range(heads):
        l, acc = totals[t]
        o_ref[:, head_cols(t)] = (acc / l).T.astype(o_ref.dtype)


def _moba(proj, *, batch, seq, d, heads, group):
    m = proj.shape[0]
    blk = MOBA_BLOCK
    assert seq % (group * blk) == 0 and MOBA_HEADS % heads == 0
    nq = seq // blk
    groups = MOBA_HEADS // heads
    slopes = _alibi_slopes(MOBA_HEADS)
    return pl.pallas_call(
        functools.partial(_moba_kernel, heads=heads, group=group),
        grid_spec=pltpu.PrefetchScalarGridSpec(
            num_scalar_prefetch=1,
            grid=(batch, groups, nq),
            in_specs=[
                pl.BlockSpec((blk, heads * d), lambda b, h, i, s_ref: (b * nq + i, h)),
                pl.BlockSpec((seq, heads * d), lambda b, h, i, s_ref: (b, groups + h)),
                pl.BlockSpec((seq, heads * d), lambda b, h, i, s_ref: (b, 2 * groups + h)),
            ],
            out_specs=pl.BlockSpec((blk, heads * d), lambda b, h, i, s_ref: (b * nq + i, h)),
            scratch_shapes=[pltpu.VMEM((heads, nq, d), F32),
                            pltpu.VMEM((heads, nq, d, blk), BF16),
                            pltpu.VMEM((heads, nq, blk, d), BF16),
                            pltpu.VMEM((heads, nq // group, group, blk, blk), F32)],
        ),
        out_shape=jax.ShapeDtypeStruct((m, MOBA_HEADS * d), BF16),
        compiler_params=_params("parallel", "parallel", "arbitrary"),
        name="moba",
    )(jnp.asarray(slopes), proj, proj, proj)


def _memory_attention(qm_ref, kv_ref):
    width = qm_ref.shape[1]
    dh = width // MEM_HEADS
    outs = []
    for h in range(MEM_HEADS):
        q = qm_ref[:, h * dh:(h + 1) * dh]
        k = kv_ref[:, h * dh:(h + 1) * dh]
        v = kv_ref[:, width + h * dh:width + (h + 1) * dh]
        s = lax.dot_general(q, k, _NT_DIMS, preferred_element_type=F32) * dh ** -0.5
        p = jnp.exp(s - jnp.max(s, axis=-1, keepdims=True))
        o = jnp.dot(p.astype(BF16), v, preferred_element_type=F32)
        outs.append((o / jnp.sum(p, axis=-1, keepdims=True)).astype(BF16))
    return jnp.concatenate(outs, axis=-1)


def _project_out(tok, mem_out, x_ref, wo_ref, o_ref):
    mix = tok.shape[1]
    o_ref[...] = (x_ref[...]
                  + jnp.dot(tok, wo_ref[:mix, :], preferred_element_type=F32)
                  + jnp.dot(mem_out, wo_ref[mix:, :], preferred_element_type=F32))


def _out_proj_kernel(tok_ref, qm_ref, kv_ref, x_ref, wo_ref, o_ref):
    _project_out(tok_ref[...], _memory_attention(qm_ref, kv_ref), x_ref, wo_ref, o_ref)


def _conv_out_proj_kernel(gb_ref, gc_ref, hh_ref, gc_prev_ref, hh_prev_ref, cw_ref,
                          qm_ref, kv_ref, x_ref, wo_ref, o_ref, *, tiles_per_seq):
    first = pl.program_id(0) % tiles_per_seq == 0
    u = gc_ref[...].astype(F32) * hh_ref[...].astype(F32)
    prev = gc_prev_ref[...].astype(F32) * hh_prev_ref[...].astype(F32)
    prev = jnp.where(first, 0.0, prev)
    tok = (gb_ref[...].astype(F32) * _causal_conv3(u, prev, cw_ref[...])).astype(BF16)
    _project_out(tok, _memory_attention(qm_ref, kv_ref), x_ref, wo_ref, o_ref)


def _out_proj(tok, proj, conv_w, kv, x, wo, *, seq, mem_tokens, tm):
    m, d = x.shape
    mem_width = kv.shape[1] // 2
    mix = d - mem_width
    tiles_per_seq = seq // tm
    qm_block = proj.shape[1] // mem_width - 1
    tail_specs = [
        pl.BlockSpec((tm, mem_width), lambda i: (i, qm_block)),
        pl.BlockSpec((mem_tokens, 2 * mem_width), lambda i: (i // tiles_per_seq, 0)),
        pl.BlockSpec((tm, d), lambda i: (i, 0)),
        pl.BlockSpec((d, d), lambda i: (0, 0)),
    ]
    if tok is not None:
        kernel = _out_proj_kernel
        head_specs = [pl.BlockSpec((tm, mix), lambda i: (i, 0))]
        head_args = (tok,)
    else:
        kernel = functools.partial(_conv_out_proj_kernel, tiles_per_seq=tiles_per_seq)
        halo = tm // SUBLANES

        def prev_rows(part):
            return pl.BlockSpec((SUBLANES, mix), lambda i: (jnp.maximum(i * halo - 1, 0), part))

        head_specs = [pl.BlockSpec((tm, mix), lambda i: (i, 0)),
                      pl.BlockSpec((tm, mix), lambda i: (i, 1)),
                      pl.BlockSpec((tm, mix), lambda i: (i, 2)),
                      prev_rows(1), prev_rows(2),
                      pl.BlockSpec((CONV_WIDTH, mix), lambda i: (0, 0))]
        head_args = (proj, proj, proj, proj, proj, conv_w)
    return pl.pallas_call(
        kernel,
        grid=(m // tm,),
        in_specs=head_specs + tail_specs,
        out_specs=pl.BlockSpec((tm, d), lambda i: (i, 0)),
        out_shape=jax.ShapeDtypeStruct((m, d), F32),
        compiler_params=_params("parallel"),
        name="out_proj",
    )(*head_args, proj, kv, x, wo)


def _conv_ffn_kernel(x_ref, g_ref, wg_ref, wu_ref, conv_ref, wd_ref,
                     fg_ref, o_ref, h_ref, carry_ref, *, tiles_per_seq, row_strips, final_norm):
    f = pl.program_id(1)
    nf = pl.num_programs(1)
    first = pl.program_id(0) % tiles_per_seq == 0
    tm = x_ref.shape[0]

    @pl.when(f == 0)
    def _():
        x = x_ref[...]
        h_ref[...] = _rms_norm(x, g_ref[...]).astype(BF16)
        o_ref[...] = x

    strip = tm // row_strips
    ups = [[jnp.dot(h_ref[pl.ds(s * strip, strip), :], w_ref[...], preferred_element_type=F32)
            for w_ref in (wg_ref, wu_ref)] for s in range(row_strips)]
    conv = [conv_ref[f + slot * nf] for slot in range(2)]
    for s in range(row_strips):
        branches = []
        for slot in range(2):
            if s == 0:
                prev = jnp.where(first, 0.0, carry_ref[f, slot])
            else:
                prev = ups[s - 1][slot][strip - SUBLANES:, :]
            branches.append(_causal_conv3(ups[s][slot], prev, conv[slot][:CONV_WIDTH])
                            + conv[slot][CONV_WIDTH:CONV_WIDTH + 1])
        act = (_silu(branches[0]) * branches[1]).astype(BF16)
        o_ref[pl.ds(s * strip, strip), :] += jnp.dot(act, wd_ref[...], preferred_element_type=F32)
    for slot in range(2):
        carry_ref[f, slot] = ups[row_strips - 1][slot][strip - SUBLANES:, :]

    if final_norm:
        @pl.when(f == pl.num_programs(1) - 1)
        def _():
            o_ref[...] = _rms_norm(o_ref[...], fg_ref[...])


def _conv_ffn(x, gain, w_up, conv_w, conv_b, w_down, final_gain, *, seq, tm, tf, final_norm):
    m, d = x.shape
    d_ff = w_down.shape[0]
    nf = d_ff // tf
    conv = jnp.concatenate(
        [conv_w.reshape(CONV_WIDTH, 2 * nf, tf).transpose(1, 0, 2),
         conv_b.reshape(2 * nf, 1, tf),
         jnp.zeros((2 * nf, SUBLANES - CONV_WIDTH - 1, tf), F32)], axis=1)
    return pl.pallas_call(
        functools.partial(_conv_ffn_kernel, tiles_per_seq=seq // tm, row_strips=2,
                          final_norm=final_norm),
        grid=(m // tm, nf),
        in_specs=[
            pl.BlockSpec((tm, d), lambda i, f: (i, 0)),
            pl.BlockSpec((1, d), lambda i, f: (0, 0)),
            pl.BlockSpec((d, tf), lambda i, f: (0, f)),
            pl.BlockSpec((d, tf), lambda i, f: (0, nf + f)),
            pl.BlockSpec((2 * nf, SUBLANES, tf), lambda i, f: (0, 0, 0)),
            pl.BlockSpec((tf, d), lambda i, f: (f, 0)),
            pl.BlockSpec((1, d), lambda i, f: (0, 0)),
        ],
        out_specs=pl.BlockSpec((tm, d), lambda i, f: (i, 0)),
        out_shape=jax.ShapeDtypeStruct((m, d), F32),
        scratch_shapes=[pltpu.VMEM((tm, d), BF16),
                        pltpu.VMEM((nf, 2, SUBLANES, tf), F32)],
        compiler_params=_params("arbitrary", "arbitrary"),
        name="conv_ffn",
    )(x, gain.reshape(1, d), w_up, w_up, conv, w_down, final_gain.reshape(1, d))


def _half_tile(n):
    return n // 2 if n % 512 == 0 else n


def _trunk(x, mem, mem_norm, layers, final_norm):
    batch, seq, d = x.shape
    mem_tokens = mem.shape[1]
    x = x.reshape(batch * seq, d)
    mem = mem.reshape(batch * mem_tokens, d)
    n_layers = len(layers)
    for i, (norm_mix, w_in, extra, w_mem_kv, w_o, norm_ffn, w_up, cw, cb, w_down) in enumerate(layers):
        kind = i % 3
        mem_width = w_mem_kv.shape[1] // 2
        mix = d - mem_width
        w_in = w_in.astype(BF16)
        proj = _norm_matmul(x, norm_mix, w_in, tm=512, tn=_half_tile(w_in.shape[1]))
        kv = _norm_matmul(mem, mem_norm, w_mem_kv.astype(BF16), tm=mem_tokens,
                          tn=w_mem_kv.shape[1])
        if kind == 0:
            tok = _retention(proj, extra, batch=batch, seq=seq, rows=1024)
        elif kind == 1:
            tok = None
        else:
            tok = _moba(proj, batch=batch, seq=seq, d=mix // MOBA_HEADS, heads=4, group=4)
        x = _out_proj(tok, proj, extra, kv, x, w_o.astype(BF16),
                      seq=seq, mem_tokens=mem_tokens, tm=512)
        x = _conv_ffn(x, norm_ffn, w_up.astype(BF16), cw, cb, w_down.astype(BF16), final_norm,
                      seq=seq, tm=1024, tf=512, final_norm=(i == n_layers - 1))
    return x.reshape(batch, seq, d)


def kernel(x, mem, mem_norm,
           l0_norm_mix, l0_w_in, l0_ret_gn, l0_w_mem_kv, l0_w_o, l0_norm_ffn, l0_ffn_w_up, l0_ffn_conv_w, l0_ffn_conv_b, l0_ffn_w_down,
           l1_norm_mix, l1_w_in, l1_conv_w, l1_w_mem_kv, l1_w_o, l1_norm_ffn, l1_ffn_w_up, l1_ffn_conv_w, l1_ffn_conv_b, l1_ffn_w_down,
           l2_norm_mix, l2_w_in, l2_w_mem_kv, l2_w_o, l2_norm_ffn, l2_ffn_w_up, l2_ffn_conv_w, l2_ffn_conv_b, l2_ffn_w_down,
           l3_norm_mix, l3_w_in, l3_ret_gn, l3_w_mem_kv, l3_w_o, l3_norm_ffn, l3_ffn_w_up, l3_ffn_conv_w, l3_ffn_conv_b, l3_ffn_w_down,
           final_norm):
    layers = [
        (l0_norm_mix, l0_w_in, l0_ret_gn, l0_w_mem_kv, l0_w_o, l0_norm_ffn, l0_ffn_w_up, l0_ffn_conv_w, l0_ffn_conv_b, l0_ffn_w_down),
        (l1_norm_mix, l1_w_in, l1_conv_w, l1_w_mem_kv, l1_w_o, l1_norm_ffn, l1_ffn_w_up, l1_ffn_conv_w, l1_ffn_conv_b, l1_ffn_w_down),
        (l2_norm_mix, l2_w_in, None, l2_w_mem_kv, l2_w_o, l2_norm_ffn, l2_ffn_w_up, l2_ffn_conv_w, l2_ffn_conv_b, l2_ffn_w_down),
        (l3_norm_mix, l3_w_in, l3_ret_gn, l3_w_mem_kv, l3_w_o, l3_norm_ffn, l3_ffn_w_up, l3_ffn_conv_w, l3_ffn_conv_b, l3_ffn_w_down),
    ]
    return _trunk(x, mem, mem_norm, layers, final_norm)
```

```python
import functools
import math

import jax
import jax.numpy as jnp
import numpy as np
from jax import lax
from jax.experimental import pallas as pl
from jax.experimental.pallas import tpu as pltpu

F32 = jnp.float32
BF16 = jnp.bfloat16

NORM_EPS = 1e-6
MEM_HEADS = 4
RET_HEADS = 6
RET_CHUNK = 256
CONV_WIDTH = 3
MOBA_HEADS = 12
MOBA_BLOCK = 256
MOBA_TOPK = 3

SUBLANES = 8
VMEM_LIMIT_BYTES = 56 * 1024 * 1024

_NT_DIMS = (((1,), (1,)), ((), ()))
_TN_DIMS = (((0,), (0,)), ((), ()))


def _params(*semantics):
    return pltpu.CompilerParams(dimension_semantics=semantics,
                                vmem_limit_bytes=VMEM_LIMIT_BYTES)


def _rms_norm(x, gain):
    return x * lax.rsqrt(jnp.mean(x * x, axis=-1, keepdims=True) + NORM_EPS) * gain


def _silu(g):
    return g * (1.0 / (1.0 + jnp.exp(-g)))


def _shift_rows(u, prev, k):
    rolled = pltpu.roll(u, k, axis=0)
    prev_rolled = pltpu.roll(prev, k, axis=0)
    row = lax.broadcasted_iota(jnp.int32, prev.shape, 0)
    head = jnp.where(row < k, prev_rolled, rolled[:SUBLANES])
    return jnp.concatenate([head, rolled[SUBLANES:]], axis=0)


def _causal_conv3(u, prev, w):
    return (w[0:1] * _shift_rows(u, prev, 2) + w[1:2] * _shift_rows(u, prev, 1)
            + w[2:3] * u)


BF16_ROWS = 16


def _cast_specs(weights, n_steps, step_index):
    in_specs, out_specs, out_shapes = [], [], []
    for w in weights:
        rows, cols = w.shape
        assert rows % (n_steps * BF16_ROWS) == 0
        block = (rows // n_steps, cols)
        spec = pl.BlockSpec(block, lambda *idx: (step_index(*idx), 0))
        in_specs.append(spec)
        out_specs.append(spec)
        out_shapes.append(jax.ShapeDtypeStruct(w.shape, BF16))
    return in_specs, out_specs, out_shapes


def _cast_blocks(src_refs, dst_refs):
    for src, dst in zip(src_refs, dst_refs):
        dst[...] = src[...].astype(BF16)


def _norm_matmul_kernel(x_ref, g_ref, w_ref, *refs):
    n_casts = (len(refs) - 1) // 2
    o_ref = refs[n_casts]
    h = _rms_norm(x_ref[...], g_ref[...]).astype(BF16)
    o_ref[...] = jnp.dot(h, w_ref[...], preferred_element_type=F32).astype(o_ref.dtype)
    _cast_blocks(refs[:n_casts], refs[n_casts + 1:])


def _norm_matmul(x, gain, w, *, tm, tn, casts=()):
    m, d = x.shape
    n = w.shape[1]
    ni = m // tm
    cast_in, cast_out, cast_shapes = _cast_specs(casts, (n // tn) * ni, lambda j, i: j * ni + i)
    out, *casted = pl.pallas_call(
        _norm_matmul_kernel,
        grid=(n // tn, ni),
        in_specs=[
            pl.BlockSpec((tm, d), lambda j, i: (i, 0)),
            pl.BlockSpec((1, d), lambda j, i: (0, 0)),
            pl.BlockSpec((d, tn), lambda j, i: (0, j)),
        ] + cast_in,
        out_specs=[pl.BlockSpec((tm, tn), lambda j, i: (i, j))] + cast_out,
        out_shape=[jax.ShapeDtypeStruct((m, n), BF16)] + cast_shapes,
        compiler_params=_params("parallel", "parallel"),
        name="norm_matmul",
    )(x, gain.reshape(1, d), w, *casts)
    return out, casted


def _retention_kernel(lg_ref, q_ref, k_ref, v_ref, g_ref, gn_ref, o_ref, state_ref, *, chunk):
    lg = lg_ref[pl.program_id(1)]
    rows, d = q_ref.shape
    c = chunk
    k_scale = d ** -0.5

    @pl.when(pl.program_id(2) == 0)
    def _():
        state_ref[...] = jnp.zeros_like(state_ref)

    diff = (lax.broadcasted_iota(jnp.int32, (c, c), 0)
            - lax.broadcasted_iota(jnp.int32, (c, c), 1)).astype(F32)
    intra = jnp.where(diff >= 0, jnp.exp(jnp.maximum(diff, 0.0) * lg), 0.0) * k_scale
    pos = lax.broadcasted_iota(jnp.int32, (c, d), 0).astype(F32)
    q_decay = jnp.exp((pos + 1.0) * lg)
    k_decay = jnp.exp((c - 1.0 - pos) * lg) * k_scale
    chunk_decay = jnp.exp(jnp.zeros((1, d), F32) + c * lg)
    gn = gn_ref[...]

    for s in range(rows // c):
        sl = pl.ds(s * c, c)
        q = q_ref[sl, :]
        k = k_ref[sl, :]
        v = v_ref[sl, :]
        scores = lax.dot_general(q, k, _NT_DIMS, preferred_element_type=F32) * intra
        y = jnp.dot(scores.astype(BF16), v, preferred_element_type=F32)
        state = state_ref[...]
        y = y + jnp.dot(q, state.astype(BF16), preferred_element_type=F32) * q_decay
        kd = (k.astype(F32) * k_decay).astype(BF16)
        kv = lax.dot_general(kd, v, _TN_DIMS, preferred_element_type=F32)
        state_ref[...] = state * chunk_decay + kv
        g = g_ref[sl, :].astype(F32)
        o_ref[sl, :] = (_silu(g) * _rms_norm(y, gn)).astype(o_ref.dtype)


def _retention(proj, gn_gain, *, batch, seq, rows):
    m = proj.shape[0]
    d = gn_gain.shape[0] // RET_HEADS
    nt = seq // rows
    lg = np.log1p(-np.exp2(-5.0 - np.arange(RET_HEADS))).astype(np.float32)

    def spec(part):
        return pl.BlockSpec((rows, d), lambda b, h, t, lg_ref: (b * nt + t, part * RET_HEADS + h))

    return pl.pallas_call(
        functools.partial(_retention_kernel, chunk=RET_CHUNK),
        grid_spec=pltpu.PrefetchScalarGridSpec(
            num_scalar_prefetch=1,
            grid=(batch, RET_HEADS, nt),
            in_specs=[spec(0), spec(1), spec(2), spec(3),
                      pl.BlockSpec((1, d), lambda b, h, t, lg_ref: (0, h))],
            out_specs=pl.BlockSpec((rows, d), lambda b, h, t, lg_ref: (b * nt + t, h)),
            scratch_shapes=[pltpu.VMEM((d, d), F32)],
        ),
        out_shape=jax.ShapeDtypeStruct((m, RET_HEADS * d), BF16),
        compiler_params=_params("parallel", "parallel", "arbitrary"),
        name="retention",
    )(jnp.asarray(lg), proj, proj, proj, proj, gn_gain.reshape(1, -1))


def _alibi_slopes(n):
    def pow2_slopes(m):
        start = 2.0 ** (-8.0 / m)
        return [start ** (i + 1) for i in range(m)]
    if math.log2(n).is_integer():
        s = pow2_slopes(n)
    else:
        c = 2 ** int(math.floor(math.log2(n)))
        s = pow2_slopes(c) + list(_alibi_slopes(2 * c))[0::2][: n - c]
    return np.asarray(s, dtype=np.float32)


MOBA_MASKED = -1e30
AUG_PARTS = 3
AUG_PART_LANE = SUBLANES


def _split_bf16(x):
    parts = []
    for _ in range(AUG_PARTS):
        part = x.astype(BF16).astype(F32)
        parts.append(part)
        x = x - part
    return parts


def _moba_kernel(slope_ref, q_ref, k_ref, v_ref, o_ref, kmean_ref, vt_ref, kaug_ref, s_ref, *,
                 heads, group):
    qi = pl.program_id(2)
    blk = q_ref.shape[0]
    d = q_ref.shape[1] // heads
    nblk = k_ref.shape[0] // blk
    scale = d ** -0.5
    exp2_scale = scale * math.log2(math.e)
    assert AUG_PART_LANE + AUG_PARTS * nblk <= d

    def head_cols(t):
        return slice(t * d, (t + 1) * d)

    def head_slope(t):
        return slope_ref[pl.program_id(1) * heads + t]

    @pl.when(qi == 0)
    def _():
        lane = lax.broadcasted_iota(jnp.int32, (blk, d), 1)
        row = lax.broadcasted_iota(jnp.int32, (blk, d), 0)
        for t in range(heads):
            key_scale = head_slope(t) * (1.0 / scale)
            for j in range(nblk):
                rows = slice(j * blk, (j + 1) * blk)
                kmean_ref[t, j:j + 1, :] = jnp.sum(k_ref[rows, head_cols(t)].astype(F32), axis=0,
                                                   keepdims=True) * (1.0 / blk)
                vt_ref[t, j] = v_ref[rows, head_cols(t)].astype(F32).T.astype(BF16)
                aug = jnp.zeros((blk, d), F32)
                for p, part in enumerate(_split_bf16((row + j * blk).astype(F32) * key_scale)):
                    aug = jnp.where(lane == p, part, aug)
                    aug = jnp.where(lane == AUG_PART_LANE + p * nblk + j, 1.0, aug)
                kaug_ref[t, j] = aug.astype(BF16)

    def query_operand(t):
        q = q_ref[:, head_cols(t)]
        kmean = kmean_ref[t]
        kmean_hi = kmean.astype(BF16)
        kmean_lo = (kmean - kmean_hi.astype(F32)).astype(BF16)
        gate = (lax.dot_general(kmean_hi, q, _NT_DIMS, preferred_element_type=F32)
                + lax.dot_general(kmean_lo, q, _NT_DIMS, preferred_element_type=F32))
        blk_id = lax.broadcasted_iota(jnp.int32, gate.shape, 0)
        rank = jnp.zeros(gate.shape, jnp.int32)
        for other in range(nblk):
            g_other = gate[other:other + 1, :]
            beats = (g_other > gate) | ((g_other == gate) & (other < blk_id))
            rank = rank + jnp.where(other < qi, jnp.where(beats, 1, 0), 0)
        keep = ((blk_id < qi) & (rank < MOBA_TOPK)) | (blk_id == qi)
        t_query = (qi * blk + lax.broadcasted_iota(jnp.int32, gate.shape, 1)).astype(F32)
        bias = (jnp.where(keep, 0.0, MOBA_MASKED) - head_slope(t) * t_query) * (1.0 / scale)
        ones = jnp.where(lax.broadcasted_iota(jnp.int32, (AUG_PART_LANE, blk), 0) < AUG_PARTS,
                         1.0, 0.0)
        pad = jnp.zeros((d - AUG_PART_LANE - AUG_PARTS * nblk, blk), F32)
        aug = jnp.concatenate([ones] + _split_bf16(bias) + [pad], axis=0)
        return jnp.concatenate([q, aug.T.astype(BF16)], axis=1)

    def scores(t, q_op, j):
        rows = pl.ds(pl.multiple_of(j * blk, blk), blk)
        k_op = jnp.concatenate([k_ref[rows, head_cols(t)], kaug_ref[t, j]], axis=1)
        return lax.dot_general(k_op, q_op, _NT_DIMS, preferred_element_type=F32)

    q_ops = [query_operand(t) for t in range(heads)]

    own_group = qi // group
    key_minus_query = (lax.broadcasted_iota(jnp.int32, (blk, blk), 0)
                       - lax.broadcasted_iota(jnp.int32, (blk, blk), 1))

    def group_scores(t, g, causal):
        best = None
        for i in range(group):
            j = g * group + i
            s = scores(t, q_ops[t], j) * exp2_scale
            if causal:
                s = jnp.where(key_minus_query <= (qi - j) * blk, s, MOBA_MASKED)
            s_ref[t, g, i] = s
            block_max = jnp.max(s, axis=0, keepdims=True)
            best = block_max if best is None else jnp.maximum(best, block_max)
        return best

    def past_scores(g, best):
        return tuple(jnp.maximum(best[t], group_scores(t, g, False)) for t in range(heads))

    best = lax.fori_loop(0, own_group, past_scores,
                         tuple(group_scores(t, own_group, True) for t in range(heads)))

    def group_values(g, carry):
        out = []
        for t in range(heads):
            l, acc = carry[t]
            ps = [jnp.exp2(s_ref[t, g, i] - best[t]) for i in range(group)]
            l = l + functools.reduce(jnp.add, [jnp.sum(p, axis=0, keepdims=True) for p in ps])
            p = jnp.concatenate([p.astype(BF16) for p in ps], axis=0)
            vt = jnp.concatenate([vt_ref[t, g * group + i] for i in range(group)], axis=1)
            out.append((l, acc + jnp.dot(vt, p, preferred_element_type=F32)))
        return tuple(out)

    zero = (jnp.zeros((1, blk), F32), jnp.zeros((d, blk), F32))
    totals = lax.fori_loop(0, own_group + 1, group_values, (zero,) * heads)
    for t in range(heads):
        l, acc = totals[t]
        o_ref[:, head_cols(t)] = (acc / l).T.astype(o_ref.dtype)


def _moba(proj, *, batch, seq, d, heads, group):
    m = proj.shape[0]
    blk = MOBA_BLOCK
    assert seq % (group * blk) == 0 and MOBA_HEADS % heads == 0
    nq = seq // blk
    groups = MOBA_HEADS // heads
    slopes = _alibi_slopes(MOBA_HEADS)
    return pl.pallas_call(
        functools.partial(_moba_kernel, heads=heads, group=group),
        grid_spec=pltpu.PrefetchScalarGridSpec(
            num_scalar_prefetch=1,
            grid=(batch, groups, nq),
            in_specs=[
                pl.BlockSpec((blk, heads * d), lambda b, h, i, s_ref: (b * nq + i, h)),
                pl.BlockSpec((seq, heads * d), lambda b, h, i, s_ref: (b, groups + h)),
                pl.BlockSpec((seq, heads * d), lambda b, h, i, s_ref: (b, 2 * groups + h)),
            ],
            out_specs=pl.BlockSpec((blk, heads * d), lambda b, h, i, s_ref: (b * nq + i, h)),
            scratch_shapes=[pltpu.VMEM((heads, nq, d), F32),
                            pltpu.VMEM((heads, nq, d, blk), BF16),
                            pltpu.VMEM((heads, nq, blk, d), BF16),
                            pltpu.VMEM((heads, nq // group, group, blk, blk), F32)],
        ),
        out_shape=jax.ShapeDtypeStruct((m, MOBA_HEADS * d), BF16),
        compiler_params=_params("parallel", "parallel", "arbitrary"),
        name="moba",
    )(jnp.asarray(slopes), proj, proj, proj)


def _memory_attention(qm_ref, kv_ref):
    width = qm_ref.shape[1]
    dh = width // MEM_HEADS
    outs = []
    for h in range(MEM_HEADS):
        q = qm_ref[:, h * dh:(h + 1) * dh]
        k = kv_ref[:, h * dh:(h + 1) * dh]
        v = kv_ref[:, width + h * dh:width + (h + 1) * dh]
        s = lax.dot_general(q, k, _NT_DIMS, preferred_element_type=F32) * dh ** -0.5
        p = jnp.exp(s - jnp.max(s, axis=-1, keepdims=True))
        o = jnp.dot(p.astype(BF16), v, preferred_element_type=F32)
        outs.append((o / jnp.sum(p, axis=-1, keepdims=True)).astype(BF16))
    return jnp.concatenate(outs, axis=-1)


def _project_out(tok, mem_out, x_ref, wo_ref, o_ref):
    mix = tok.shape[1]
    o_ref[...] = (x_ref[...]
                  + jnp.dot(tok, wo_ref[:mix, :], preferred_element_type=F32)
                  + jnp.dot(mem_out, wo_ref[mix:, :], preferred_element_type=F32))


def _out_proj_kernel(tok_ref, qm_ref, kv_ref, x_ref, wo_ref, *refs):
    n_casts = (len(refs) - 1) // 2
    _project_out(tok_ref[...], _memory_attention(qm_ref, kv_ref), x_ref, wo_ref, refs[n_casts])
    _cast_blocks(refs[:n_casts], refs[n_casts + 1:])


def _conv_out_proj_kernel(gb_ref, gc_ref, hh_ref, gc_prev_ref, hh_prev_ref, cw_ref,
                          qm_ref, kv_ref, x_ref, wo_ref, *refs, tiles_per_seq):
    n_casts = (len(refs) - 1) // 2
    first = pl.program_id(0) % tiles_per_seq == 0
    u = gc_ref[...].astype(F32) * hh_ref[...].astype(F32)
    prev = gc_prev_ref[...].astype(F32) * hh_prev_ref[...].astype(F32)
    prev = jnp.where(first, 0.0, prev)
    tok = (gb_ref[...].astype(F32) * _causal_conv3(u, prev, cw_ref[...])).astype(BF16)
    _project_out(tok, _memory_attention(qm_ref, kv_ref), x_ref, wo_ref, refs[n_casts])
    _cast_blocks(refs[:n_casts], refs[n_casts + 1:])


def _out_proj(tok, proj, conv_w, kv, x, wo, *, seq, mem_tokens, tm, casts=()):
    m, d = x.shape
    mem_width = kv.shape[1] // 2
    mix = d - mem_width
    tiles_per_seq = seq // tm
    qm_block = proj.shape[1] // mem_width - 1
    cast_in, cast_out, cast_shapes = _cast_specs(casts, m // tm, lambda i: i)
    tail_specs = [
        pl.BlockSpec((tm, mem_width), lambda i: (i, qm_block)),
        pl.BlockSpec((mem_tokens, 2 * mem_width), lambda i: (i // tiles_per_seq, 0)),
        pl.BlockSpec((tm, d), lambda i: (i, 0)),
        pl.BlockSpec((d, d), lambda i: (0, 0), pipeline_mode=pl.Buffered(1)),
    ] + cast_in
    if tok is not None:
        kernel = _out_proj_kernel
        head_specs = [pl.BlockSpec((tm, mix), lambda i: (i, 0))]
        head_args = (tok,)
    else:
        kernel = functools.partial(_conv_out_proj_kernel, tiles_per_seq=tiles_per_seq)
        halo = tm // SUBLANES

        def prev_rows(part):
            return pl.BlockSpec((SUBLANES, mix), lambda i: (jnp.maximum(i * halo - 1, 0), part))

        head_specs = [pl.BlockSpec((tm, mix), lambda i: (i, 0)),
                      pl.BlockSpec((tm, mix), lambda i: (i, 1)),
                      pl.BlockSpec((tm, mix), lambda i: (i, 2)),
                      prev_rows(1), prev_rows(2),
                      pl.BlockSpec((CONV_WIDTH, mix), lambda i: (0, 0))]
        head_args = (proj, proj, proj, proj, proj, conv_w)
    out, *casted = pl.pallas_call(
        kernel,
        grid=(m // tm,),
        in_specs=head_specs + tail_specs,
        out_specs=[pl.BlockSpec((tm, d), lambda i: (i, 0))] + cast_out,
        out_shape=[jax.ShapeDtypeStruct((m, d), F32)] + cast_shapes,
        compiler_params=_params("parallel"),
        name="out_proj",
    )(*head_args, proj, kv, x, wo, *casts)
    return out, casted


def _conv_ffn_kernel(x_ref, g_ref, wg_ref, wu_ref, conv_ref, wd_ref,
                     fg_ref, o_ref, h_ref, carry_ref, *, tiles_per_seq, row_strips, final_norm):
    f = pl.program_id(1)
    nf = pl.num_programs(1)
    first = pl.program_id(0) % tiles_per_seq == 0
    tm = x_ref.shape[0]

    @pl.when(f == 0)
    def _():
        x = x_ref[...]
        h_ref[...] = _rms_norm(x, g_ref[...]).astype(BF16)
        o_ref[...] = x

    strip = tm // row_strips
    ups = [[jnp.dot(h_ref[pl.ds(s * strip, strip), :], w_ref[...], preferred_element_type=F32)
            for w_ref in (wg_ref, wu_ref)] for s in range(row_strips)]
    conv = [conv_ref[f + slot * nf] for slot in range(2)]
    for s in range(row_strips):
        branches = []
        for slot in range(2):
            if s == 0:
                prev = jnp.where(first, 0.0, carry_ref[f, slot])
            else:
                prev = ups[s - 1][slot][strip - SUBLANES:, :]
            branches.append(_causal_conv3(ups[s][slot], prev, conv[slot][:CONV_WIDTH])
                            + conv[slot][CONV_WIDTH:CONV_WIDTH + 1])
        act = (_silu(branches[0]) * branches[1]).astype(BF16)
        o_ref[pl.ds(s * strip, strip), :] += jnp.dot(act, wd_ref[...], preferred_element_type=F32)
    for slot in range(2):
        carry_ref[f, slot] = ups[row_strips - 1][slot][strip - SUBLANES:, :]

    if final_norm:
        @pl.when(f == pl.num_programs(1) - 1)
        def _():
            o_ref[...] = _rms_norm(o_ref[...], fg_ref[...])


def _conv_ffn(x, gain, w_up, conv_w, conv_b, w_down, final_gain, *, seq, tm, tf, final_norm):
    m, d = x.shape
    d_ff = w_down.shape[0]
    nf = d_ff // tf
    conv = jnp.concatenate(
        [conv_w.reshape(CONV_WIDTH, 2 * nf, tf).transpose(1, 0, 2),
         conv_b.reshape(2 * nf, 1, tf),
         jnp.zeros((2 * nf, SUBLANES - CONV_WIDTH - 1, tf), F32)], axis=1)
    return pl.pallas_call(
        functools.partial(_conv_ffn_kernel, tiles_per_seq=seq // tm, row_strips=2,
                          final_norm=final_norm),
        grid=(m // tm, nf),
        in_specs=[
            pl.BlockSpec((tm, d), lambda i, f: (i, 0)),
            pl.BlockSpec((1, d), lambda i, f: (0, 0)),
            pl.BlockSpec((d, tf), lambda i, f: (0, f)),
            pl.BlockSpec((d, tf), lambda i, f: (0, nf + f)),
            pl.BlockSpec((2 * nf, SUBLANES, tf), lambda i, f: (0, 0, 0)),
            pl.BlockSpec((tf, d), lambda i, f: (f, 0)),
            pl.BlockSpec((1, d), lambda i, f: (0, 0)),
        ],
        out_specs=pl.BlockSpec((tm, d), lambda i, f: (i, 0)),
        out_shape=jax.ShapeDtypeStruct((m, d), F32),
        scratch_shapes=[pltpu.VMEM((tm, d), BF16),
                        pltpu.VMEM((nf, 2, SUBLANES, tf), F32)],
        compiler_params=_params("arbitrary", "arbitrary"),
        name="conv_ffn",
    )(x, gain.reshape(1, d), w_up, w_up, conv, w_down, final_gain.reshape(1, d))


def _half_tile(n):
    return n // 2 if n % 512 == 0 else n


def _trunk(x, mem, mem_norm, layers, final_norm):
    batch, seq, d = x.shape
    mem_tokens = mem.shape[1]
    x = x.reshape(batch * seq, d)
    mem = mem.reshape(batch * mem_tokens, d)
    n_layers = len(layers)
    w_in = layers[0][1].astype(BF16)
    w_mem_kv = layers[0][3].astype(BF16)
    for i, (norm_mix, _, extra, _, w_o, norm_ffn, w_up, cw, cb, w_down) in enumerate(layers):
        kind = i % 3
        mem_width = w_mem_kv.shape[1] // 2
        mix = d - mem_width
        proj, (w_up, w_o) = _norm_matmul(x, norm_mix, w_in, tm=512, tn=_half_tile(w_in.shape[1]),
                                         casts=(w_up, w_o))
        kv, _ = _norm_matmul(mem, mem_norm, w_mem_kv, tm=mem_tokens, tn=w_mem_kv.shape[1])
        if kind == 0:
            tok = _retention(proj, extra, batch=batch, seq=seq, rows=1024)
        elif kind == 1:
            tok = None
        else:
            tok = _moba(proj, batch=batch, seq=seq, d=mix // MOBA_HEADS, heads=4, group=4)
        next_weights = (layers[i + 1][1], layers[i + 1][3]) if i + 1 < n_layers else ()
        x, (w_down, *next_weights) = _out_proj(tok, proj, extra, kv, x, w_o, seq=seq,
                                               mem_tokens=mem_tokens, tm=512,
                                               casts=(w_down,) + next_weights)
        if next_weights:
            w_in, w_mem_kv = next_weights
        x = _conv_ffn(x, norm_ffn, w_up, cw, cb, w_down, final_norm,
                      seq=seq, tm=1024, tf=512, final_norm=(i == n_layers - 1))
    return x.reshape(batch, seq, d)


def kernel(x, mem, mem_norm,
           l0_norm_mix, l0_w_in, l0_ret_gn, l0_w_mem_kv, l0_w_o, l0_norm_ffn, l0_ffn_w_up, l0_ffn_conv_w, l0_ffn_conv_b, l0_ffn_w_down,
           l1_norm_mix, l1_w_in, l1_conv_w, l1_w_mem_kv, l1_w_o, l1_norm_ffn, l1_ffn_w_up, l1_ffn_conv_w, l1_ffn_conv_b, l1_ffn_w_down,
           l2_norm_mix, l2_w_in, l2_w_mem_kv, l2_w_o, l2_norm_ffn, l2_ffn_w_up, l2_ffn_conv_w, l2_ffn_conv_b, l2_ffn_w_down,
           l3_norm_mix, l3_w_in, l3_ret_gn, l3_w_mem_kv, l3_w_o, l3_norm_ffn, l3_ffn_w_up, l3_ffn_conv_w, l3_ffn_conv_b, l3_ffn_w_down,
           final_norm):
    layers = [
        (l0_norm_mix, l0_w_in, l0_ret_gn, l0_w_mem_kv, l0_w_o, l0_norm_ffn, l0_ffn_w_up, l0_ffn_conv_w, l0_ffn_conv_b, l0_ffn_w_down),
        (l1_norm_mix, l1_w_in, l1_conv_w, l1_w_mem_kv, l1_w_o, l1_norm_ffn, l1_ffn_w_up, l1_ffn_conv_w, l1_ffn_conv_b, l1_ffn_w_down),
        (l2_norm_mix, l2_w_in, None, l2_w_mem_kv, l2_w_o, l2_norm_ffn, l2_ffn_w_up, l2_ffn_conv_w, l2_ffn_conv_b, l2_ffn_w_down),
        (l3_norm_mix, l3_w_in, l3_ret_gn, l3_w_mem_kv, l3_w_o, l3_norm_ffn, l3_ffn_w_up, l3_ffn_conv_w, l3_ffn_conv_b, l3_ffn_w_down),
    ]
    return _trunk(x, mem, mem_norm, layers, final_norm)
```

```python
import functools
import math

import jax
import jax.numpy as jnp
import numpy as np
from jax import lax
from jax.experimental import pallas as pl
from jax.experimental.pallas import tpu as pltpu

F32 = jnp.float32
BF16 = jnp.bfloat16

NORM_EPS = 1e-6
MEM_HEADS = 4
RET_HEADS = 6
RET_CHUNK = 256
CONV_WIDTH = 3
MOBA_HEADS = 12
MOBA_BLOCK = 256
MOBA_TOPK = 3

SUBLANES = 8
VMEM_LIMIT_BYTES = 56 * 1024 * 1024

_NT_DIMS = (((1,), (1,)), ((), ()))
_TN_DIMS = (((0,), (0,)), ((), ()))


def _params(*semantics):
    return pltpu.CompilerParams(dimension_semantics=semantics,
                                vmem_limit_bytes=VMEM_LIMIT_BYTES)


def _rms_norm(x, gain):
    return x * lax.rsqrt(jnp.mean(x * x, axis=-1, keepdims=True) + NORM_EPS) * gain


def _silu(g):
    return g * (1.0 / (1.0 + jnp.exp(-g)))


def _shift_rows(u, prev, k):
    rolled = pltpu.roll(u, k, axis=0)
    prev_rolled = pltpu.roll(prev, k, axis=0)
    row = lax.broadcasted_iota(jnp.int32, prev.shape, 0)
    head = jnp.where(row < k, prev_rolled, rolled[:SUBLANES])
    return jnp.concatenate([head, rolled[SUBLANES:]], axis=0)


def _causal_conv3(u, prev, w):
    return (w[0:1] * _shift_rows(u, prev, 2) + w[1:2] * _shift_rows(u, prev, 1)
            + w[2:3] * u)


BF16_ROWS = 16


def _cast_specs(weights, n_steps, step_index):
    in_specs, out_specs, out_shapes = [], [], []
    for w in weights:
        rows, cols = w.shape
        assert rows % (n_steps * BF16_ROWS) == 0
        block = (rows // n_steps, cols)
        spec = pl.BlockSpec(block, lambda *idx: (step_index(*idx), 0))
        in_specs.append(spec)
        out_specs.append(spec)
        out_shapes.append(jax.ShapeDtypeStruct(w.shape, BF16))
    return in_specs, out_specs, out_shapes


def _cast_blocks(src_refs, dst_refs):
    for src, dst in zip(src_refs, dst_refs):
        dst[...] = src[...].astype(BF16)


def _norm_matmul_kernel(x_ref, g_ref, w_ref, *refs):
    n_casts = (len(refs) - 1) // 2
    o_ref = refs[n_casts]
    h = _rms_norm(x_ref[...], g_ref[...]).astype(BF16)
    o_ref[...] = jnp.dot(h, w_ref[...], preferred_element_type=F32).astype(o_ref.dtype)
    _cast_blocks(refs[:n_casts], refs[n_casts + 1:])


def _norm_matmul(x, gain, w, *, tm, tn, casts=()):
    m, d = x.shape
    n = w.shape[1]
    ni = m // tm
    cast_in, cast_out, cast_shapes = _cast_specs(casts, (n // tn) * ni, lambda j, i: j * ni + i)
    out, *casted = pl.pallas_call(
        _norm_matmul_kernel,
        grid=(n // tn, ni),
        in_specs=[
            pl.BlockSpec((tm, d), lambda j, i: (i, 0)),
            pl.BlockSpec((1, d), lambda j, i: (0, 0)),
            pl.BlockSpec((d, tn), lambda j, i: (0, j)),
        ] + cast_in,
        out_specs=[pl.BlockSpec((tm, tn), lambda j, i: (i, j))] + cast_out,
        out_shape=[jax.ShapeDtypeStruct((m, n), BF16)] + cast_shapes,
        compiler_params=_params("parallel", "parallel"),
        name="norm_matmul",
    )(x, gain.reshape(1, d), w, *casts)
    return out, casted


def _retention_kernel(lg_ref, q_ref, k_ref, v_ref, g_ref, gn_ref, o_ref, state_ref, *, chunk):
    lg = lg_ref[pl.program_id(1)]
    rows, d = q_ref.shape
    c = chunk
    k_scale = d ** -0.5

    @pl.when(pl.program_id(2) == 0)
    def _():
        state_ref[...] = jnp.zeros_like(state_ref)

    diff = (lax.broadcasted_iota(jnp.int32, (c, c), 0)
            - lax.broadcasted_iota(jnp.int32, (c, c), 1)).astype(F32)
    intra = jnp.where(diff >= 0, jnp.exp(jnp.maximum(diff, 0.0) * lg), 0.0) * k_scale
    pos = lax.broadcasted_iota(jnp.int32, (c, d), 0).astype(F32)
    q_decay = jnp.exp((pos + 1.0) * lg)
    k_decay = jnp.exp((c - 1.0 - pos) * lg) * k_scale
    chunk_decay = jnp.exp(jnp.zeros((1, d), F32) + c * lg)
    gn = gn_ref[...]

    for s in range(rows // c):
        sl = pl.ds(s * c, c)
        q = q_ref[sl, :]
        k = k_ref[sl, :]
        v = v_ref[sl, :]
        scores = lax.dot_general(q, k, _NT_DIMS, preferred_element_type=F32) * intra
        y = jnp.dot(scores.astype(BF16), v, preferred_element_type=F32)
        state = state_ref[...]
        y = y + jnp.dot(q, state.astype(BF16), preferred_element_type=F32) * q_decay
        kd = (k.astype(F32) * k_decay).astype(BF16)
        kv = lax.dot_general(kd, v, _TN_DIMS, preferred_element_type=F32)
        state_ref[...] = state * chunk_decay + kv
        g = g_ref[sl, :].astype(F32)
        o_ref[sl, :] = (_silu(g) * _rms_norm(y, gn)).astype(o_ref.dtype)


def _retention(proj, gn_gain, *, batch, seq, rows):
    m = proj.shape[0]
    d = gn_gain.shape[0] // RET_HEADS
    nt = seq // rows
    lg = np.log1p(-np.exp2(-5.0 - np.arange(RET_HEADS))).astype(np.float32)

    def spec(part):
        return pl.BlockSpec((rows, d), lambda b, h, t, lg_ref: (b * nt + t, part * RET_HEADS + h))

    return pl.pallas_call(
        functools.partial(_retention_kernel, chunk=RET_CHUNK),
        grid_spec=pltpu.PrefetchScalarGridSpec(
            num_scalar_prefetch=1,
            grid=(batch, RET_HEADS, nt),
            in_specs=[spec(0), spec(1), spec(2), spec(3),
                      pl.BlockSpec((1, d), lambda b, h, t, lg_ref: (0, h))],
            out_specs=pl.BlockSpec((rows, d), lambda b, h, t, lg_ref: (b * nt + t, h)),
            scratch_shapes=[pltpu.VMEM((d, d), F32)],
        ),
        out_shape=jax.ShapeDtypeStruct((m, RET_HEADS * d), BF16),
        compiler_params=_params("parallel", "parallel", "arbitrary"),
        name="retention",
    )(jnp.asarray(lg), proj, proj, proj, proj, gn_gain.reshape(1, -1))


def _alibi_slopes(n):
    def pow2_slopes(m):
        start = 2.0 ** (-8.0 / m)
        return [start ** (i + 1) for i in range(m)]
    if math.log2(n).is_integer():
        s = pow2_slopes(n)
    else:
        c = 2 ** int(math.floor(math.log2(n)))
        s = pow2_slopes(c) + list(_alibi_slopes(2 * c))[0::2][: n - c]
    return np.asarray(s, dtype=np.float32)


MOBA_MASKED = -1e30
AUG_PARTS = 3
AUG_PART_LANE = SUBLANES


def _split_bf16(x):
    parts = []
    for _ in range(AUG_PARTS):
        part = x.astype(BF16).astype(F32)
        parts.append(part)
        x = x - part
    return parts


def _moba_kernel(slope_ref, q_ref, k_ref, v_ref, o_ref, kmean_ref, vt_ref, kaug_ref, s_ref, *,
                 heads, group):
    qi = pl.program_id(2)
    blk = q_ref.shape[0]
    d = q_ref.shape[1] // heads
    nblk = k_ref.shape[0] // blk
    scale = d ** -0.5
    exp2_scale = scale * math.log2(math.e)
    assert AUG_PART_LANE + AUG_PARTS * nblk <= d

    def head_cols(t):
        return slice(t * d, (t + 1) * d)

    def head_slope(t):
        return slope_ref[pl.program_id(0) * heads + t]

    @pl.when((qi == 0) & (pl.program_id(1) == 0))
    def _():
        lane = lax.broadcasted_iota(jnp.int32, (blk, d), 1)
        row = lax.broadcasted_iota(jnp.int32, (blk, d), 0)
        for t in range(heads):
            key_scale = head_slope(t) * (1.0 / scale)
            for j in range(nblk):
                aug = jnp.zeros((blk, d), F32)
                for p, part in enumerate(_split_bf16((row + j * blk).astype(F32) * key_scale)):
                    aug = jnp.where(lane == p, part, aug)
                    aug = jnp.where(lane == AUG_PART_LANE + p * nblk + j, 1.0, aug)
                kaug_ref[t, j] = aug.astype(BF16)

    @pl.when(qi == 0)
    def _():
        for t in range(heads):
            for j in range(nblk):
                rows = slice(j * blk, (j + 1) * blk)
                kmean_ref[t, j:j + 1, :] = jnp.sum(k_ref[rows, head_cols(t)].astype(F32), axis=0,
                                                   keepdims=True) * (1.0 / blk)
                vt_ref[t, j] = v_ref[rows, head_cols(t)].astype(F32).T.astype(BF16)

    def query_operand(t):
        q = q_ref[:, head_cols(t)]
        kmean = kmean_ref[t]
        kmean_hi = kmean.astype(BF16)
        kmean_lo = (kmean - kmean_hi.astype(F32)).astype(BF16)
        gate = (lax.dot_general(kmean_hi, q, _NT_DIMS, preferred_element_type=F32)
                + lax.dot_general(kmean_lo, q, _NT_DIMS, preferred_element_type=F32))
        blk_id = lax.broadcasted_iota(jnp.int32, gate.shape, 0)
        rank = jnp.zeros(gate.shape, jnp.int32)
        for other in range(nblk):
            g_other = gate[other:other + 1, :]
            beats = (g_other > gate) | ((g_other == gate) & (other < blk_id))
            rank = rank + jnp.where(other < qi, jnp.where(beats, 1, 0), 0)
        keep = ((blk_id < qi) & (rank < MOBA_TOPK)) | (blk_id == qi)
        t_query = (qi * blk + lax.broadcasted_iota(jnp.int32, gate.shape, 1)).astype(F32)
        bias = (jnp.where(keep, 0.0, MOBA_MASKED) - head_slope(t) * t_query) * (1.0 / scale)
        ones = jnp.where(lax.broadcasted_iota(jnp.int32, (AUG_PART_LANE, blk), 0) < AUG_PARTS,
                         1.0, 0.0)
        pad = jnp.zeros((d - AUG_PART_LANE - AUG_PARTS * nblk, blk), F32)
        aug = jnp.concatenate([ones] + _split_bf16(bias) + [pad], axis=0)
        return jnp.concatenate([q, aug.T.astype(BF16)], axis=1)

    def scores(t, q_op, j):
        rows = pl.ds(pl.multiple_of(j * blk, blk), blk)
        k_op = jnp.concatenate([k_ref[rows, head_cols(t)], kaug_ref[t, j]], axis=1)
        return lax.dot_general(k_op, q_op, _NT_DIMS, preferred_element_type=F32)

    q_ops = [query_operand(t) for t in range(heads)]

    own_group = qi // group
    key_minus_query = (lax.broadcasted_iota(jnp.int32, (blk, blk), 0)
                       - lax.broadcasted_iota(jnp.int32, (blk, blk), 1))

    def group_scores(t, g, causal):
        best = None
        for i in range(group):
            j = g * group + i
            s = scores(t, q_ops[t], j) * exp2_scale
            if causal:
                s = jnp.where(key_minus_query <= (qi - j) * blk, s, MOBA_MASKED)
            s_ref[t, g, i] = s
            block_max = jnp.max(s, axis=0, keepdims=True)
            best = block_max if best is None else jnp.maximum(best, block_max)
        return best

    def past_scores(g, best):
        return tuple(jnp.maximum(best[t], group_scores(t, g, False)) for t in range(heads))

    best = lax.fori_loop(0, own_group, past_scores,
                         tuple(group_scores(t, own_group, True) for t in range(heads)))

    def group_values(g, carry):
        out = []
        for t in range(heads):
            l, acc = carry[t]
            ps = [jnp.exp2(s_ref[t, g, i] - best[t]) for i in range(group)]
            l = l + functools.reduce(jnp.add, [jnp.sum(p, axis=0, keepdims=True) for p in ps])
            p = jnp.concatenate([p.astype(BF16) for p in ps], axis=0)
            vt = jnp.concatenate([vt_ref[t, g * group + i] for i in range(group)], axis=1)
            out.append((l, acc + jnp.dot(vt, p, preferred_element_type=F32)))
        return tuple(out)

    zero = (jnp.zeros((1, blk), F32), jnp.zeros((d, blk), F32))
    totals = lax.fori_loop(0, own_group + 1, group_values, (zero,) * heads)
    for t in range(heads):
        l, acc = totals[t]
        o_ref[:, head_cols(t)] = (acc / l).T.astype(o_ref.dtype)


def _moba(proj, *, batch, seq, d, heads, group):
    m = proj.shape[0]
    blk = MOBA_BLOCK
    assert seq % (group * blk) == 0 and MOBA_HEADS % heads == 0
    nq = seq // blk
    groups = MOBA_HEADS // heads
    slopes = _alibi_slopes(MOBA_HEADS)
    return pl.pallas_call(
        functools.partial(_moba_kernel, heads=heads, group=group),
        grid_spec=pltpu.PrefetchScalarGridSpec(
            num_scalar_prefetch=1,
            grid=(groups, batch, nq),
            in_specs=[
                pl.BlockSpec((blk, heads * d), lambda h, b, i, s_ref: (b * nq + i, h)),
                pl.BlockSpec((seq, heads * d), lambda h, b, i, s_ref: (b, groups + h)),
                pl.BlockSpec((seq, heads * d), lambda h, b, i, s_ref: (b, 2 * groups + h)),
            ],
            out_specs=pl.BlockSpec((blk, heads * d), lambda h, b, i, s_ref: (b * nq + i, h)),
            scratch_shapes=[pltpu.VMEM((heads, nq, d), F32),
                            pltpu.VMEM((heads, nq, d, blk), BF16),
                            pltpu.VMEM((heads, nq, blk, d), BF16),
                            pltpu.VMEM((heads, nq // group, group, blk, blk), F32)],
        ),
        out_shape=jax.ShapeDtypeStruct((m, MOBA_HEADS * d), BF16),
        compiler_params=_params("parallel", "arbitrary", "arbitrary"),
        name="moba",
    )(jnp.asarray(slopes), proj, proj, proj)


def _memory_attention(qm_ref, kv_ref):
    width = qm_ref.shape[1]
    dh = width // MEM_HEADS
    outs = []
    for h in range(MEM_HEADS):
        q = qm_ref[:, h * dh:(h + 1) * dh]
        k = kv_ref[:, h * dh:(h + 1) * dh]
        v = kv_ref[:, width + h * dh:width + (h + 1) * dh]
        s = lax.dot_general(q, k, _NT_DIMS, preferred_element_type=F32) * dh ** -0.5
        p = jnp.exp(s - jnp.max(s, axis=-1, keepdims=True))
        o = jnp.dot(p.astype(BF16), v, preferred_element_type=F32)
        outs.append((o / jnp.sum(p, axis=-1, keepdims=True)).astype(BF16))
    return jnp.concatenate(outs, axis=-1)


def _project_out(tok, mem_out, x_ref, wo_ref, o_ref):
    mix = tok.shape[1]
    o_ref[...] = (x_ref[...]
                  + jnp.dot(tok, wo_ref[:mix, :], preferred_element_type=F32)
                  + jnp.dot(mem_out, wo_ref[mix:, :], preferred_element_type=F32))


def _out_proj_kernel(tok_ref, qm_ref, kv_ref, x_ref, wo_ref, *refs):
    n_casts = (len(refs) - 1) // 2
    _project_out(tok_ref[...], _memory_attention(qm_ref, kv_ref), x_ref, wo_ref, refs[n_casts])
    _cast_blocks(refs[:n_casts], refs[n_casts + 1:])


def _conv_out_proj_kernel(gb_ref, gc_ref, hh_ref, gc_prev_ref, hh_prev_ref, cw_ref,
                          qm_ref, kv_ref, x_ref, wo_ref, *refs, tiles_per_seq):
    n_casts = (len(refs) - 1) // 2
    first = pl.program_id(0) % tiles_per_seq == 0
    u = gc_ref[...].astype(F32) * hh_ref[...].astype(F32)
    prev = gc_prev_ref[...].astype(F32) * hh_prev_ref[...].astype(F32)
    prev = jnp.where(first, 0.0, prev)
    tok = (gb_ref[...].astype(F32) * _causal_conv3(u, prev, cw_ref[...])).astype(BF16)
    _project_out(tok, _memory_attention(qm_ref, kv_ref), x_ref, wo_ref, refs[n_casts])
    _cast_blocks(refs[:n_casts], refs[n_casts + 1:])


def _out_proj(tok, proj, conv_w, kv, x, wo, *, seq, mem_tokens, tm, casts=()):
    m, d = x.shape
    mem_width = kv.shape[1] // 2
    mix = d - mem_width
    tiles_per_seq = seq // tm
    qm_block = proj.shape[1] // mem_width - 1
    cast_in, cast_out, cast_shapes = _cast_specs(casts, m // tm, lambda i: i)
    tail_specs = [
        pl.BlockSpec((tm, mem_width), lambda i: (i, qm_block)),
        pl.BlockSpec((mem_tokens, 2 * mem_width), lambda i: (i // tiles_per_seq, 0)),
        pl.BlockSpec((tm, d), lambda i: (i, 0)),
        pl.BlockSpec((d, d), lambda i: (0, 0), pipeline_mode=pl.Buffered(1)),
    ] + cast_in
    if tok is not None:
        kernel = _out_proj_kernel
        head_specs = [pl.BlockSpec((tm, mix), lambda i: (i, 0))]
        head_args = (tok,)
    else:
        kernel = functools.partial(_conv_out_proj_kernel, tiles_per_seq=tiles_per_seq)
        halo = tm // SUBLANES

        def prev_rows(part):
            return pl.BlockSpec((SUBLANES, mix), lambda i: (jnp.maximum(i * halo - 1, 0), part))

        head_specs = [pl.BlockSpec((tm, mix), lambda i: (i, 0)),
                      pl.BlockSpec((tm, mix), lambda i: (i, 1)),
                      pl.BlockSpec((tm, mix), lambda i: (i, 2)),
                      prev_rows(1), prev_rows(2),
                      pl.BlockSpec((CONV_WIDTH, mix), lambda i: (0, 0))]
        head_args = (proj, proj, proj, proj, proj, conv_w)
    out, *casted = pl.pallas_call(
        kernel,
        grid=(m // tm,),
        in_specs=head_specs + tail_specs,
        out_specs=[pl.BlockSpec((tm, d), lambda i: (i, 0))] + cast_out,
        out_shape=[jax.ShapeDtypeStruct((m, d), F32)] + cast_shapes,
        compiler_params=_params("parallel"),
        name="out_proj",
    )(*head_args, proj, kv, x, wo, *casts)
    return out, casted


def _conv_ffn_kernel(x_ref, g_ref, wg_ref, wu_ref, conv_ref, wd_ref,
                     fg_ref, o_ref, h_ref, carry_ref, *, tiles_per_seq, row_strips, final_norm):
    f = pl.program_id(1)
    nf = pl.num_programs(1)
    first = pl.program_id(0) % tiles_per_seq == 0
    tm = x_ref.shape[0]

    @pl.when(f == 0)
    def _():
        x = x_ref[...]
        h_ref[...] = _rms_norm(x, g_ref[...]).astype(BF16)
        o_ref[...] = x

    strip = tm // row_strips
    ups = [[jnp.dot(h_ref[pl.ds(s * strip, strip), :], w_ref[...], preferred_element_type=F32)
            for w_ref in (wg_ref, wu_ref)] for s in range(row_strips)]
    conv = [conv_ref[f + slot * nf] for slot in range(2)]
    for s in range(row_strips):
        branches = []
        for slot in range(2):
            if s == 0:
                prev = jnp.where(first, 0.0, carry_ref[f, slot])
            else:
                prev = ups[s - 1][slot][strip - SUBLANES:, :]
            branches.append(_causal_conv3(ups[s][slot], prev, conv[slot][:CONV_WIDTH])
                            + conv[slot][CONV_WIDTH:CONV_WIDTH + 1])
        act = (_silu(branches[0]) * branches[1]).astype(BF16)
        o_ref[pl.ds(s * strip, strip), :] += jnp.dot(act, wd_ref[...], preferred_element_type=F32)
    for slot in range(2):
        carry_ref[f, slot] = ups[row_strips - 1][slot][strip - SUBLANES:, :]

    if final_norm:
        @pl.when(f == pl.num_programs(1) - 1)
        def _():
            o_ref[...] = _rms_norm(o_ref[...], fg_ref[...])


def _conv_ffn(x, gain, w_up, conv_w, conv_b, w_down, final_gain, *, seq, tm, tf, final_norm):
    m, d = x.shape
    d_ff = w_down.shape[0]
    nf = d_ff // tf
    conv = jnp.concatenate(
        [conv_w.reshape(CONV_WIDTH, 2 * nf, tf).transpose(1, 0, 2),
         conv_b.reshape(2 * nf, 1, tf),
         jnp.zeros((2 * nf, SUBLANES - CONV_WIDTH - 1, tf), F32)], axis=1)
    return pl.pallas_call(
        functools.partial(_conv_ffn_kernel, tiles_per_seq=seq // tm, row_strips=2,
                          final_norm=final_norm),
        grid=(m // tm, nf),
        in_specs=[
            pl.BlockSpec((tm, d), lambda i, f: (i, 0)),
            pl.BlockSpec((1, d), lambda i, f: (0, 0)),
            pl.BlockSpec((d, tf), lambda i, f: (0, f)),
            pl.BlockSpec((d, tf), lambda i, f: (0, nf + f)),
            pl.BlockSpec((2 * nf, SUBLANES, tf), lambda i, f: (0, 0, 0)),
            pl.BlockSpec((tf, d), lambda i, f: (f, 0)),
            pl.BlockSpec((1, d), lambda i, f: (0, 0)),
        ],
        out_specs=pl.BlockSpec((tm, d), lambda i, f: (i, 0)),
        out_shape=jax.ShapeDtypeStruct((m, d), F32),
        scratch_shapes=[pltpu.VMEM((tm, d), BF16),
                        pltpu.VMEM((nf, 2, SUBLANES, tf), F32)],
        compiler_params=_params("arbitrary", "arbitrary"),
        name="conv_ffn",
    )(x, gain.reshape(1, d), w_up, w_up, conv, w_down, final_gain.reshape(1, d))


def _half_tile(n):
    return n // 2 if n % 512 == 0 else n


def _trunk(x, mem, mem_norm, layers, final_norm):
    batch, seq, d = x.shape
    mem_tokens = mem.shape[1]
    x = x.reshape(batch * seq, d)
    mem = mem.reshape(batch * mem_tokens, d)
    n_layers = len(layers)
    w_in = layers[0][1].astype(BF16)
    w_mem_kv = layers[0][3].astype(BF16)
    for i, (norm_mix, _, extra, _, w_o, norm_ffn, w_up, cw, cb, w_down) in enumerate(layers):
        kind = i % 3
        mem_width = w_mem_kv.shape[1] // 2
        mix = d - mem_width
        proj, (w_up, w_o) = _norm_matmul(x, norm_mix, w_in, tm=512, tn=_half_tile(w_in.shape[1]),
                                         casts=(w_up, w_o))
        kv, _ = _norm_matmul(mem, mem_norm, w_mem_kv, tm=mem_tokens, tn=w_mem_kv.shape[1])
        if kind == 0:
            tok = _retention(proj, extra, batch=batch, seq=seq, rows=2048)
        elif kind == 1:
            tok = None
        else:
            tok = _moba(proj, batch=batch, seq=seq, d=mix // MOBA_HEADS, heads=4, group=4)
        next_weights = (layers[i + 1][1], layers[i + 1][3]) if i + 1 < n_layers else ()
        x, (w_down, *next_weights) = _out_proj(tok, proj, extra, kv, x, w_o, seq=seq,
                                               mem_tokens=mem_tokens, tm=512,
                                               casts=(w_down,) + next_weights)
        if next_weights:
            w_in, w_mem_kv = next_weights
        x = _conv_ffn(x, norm_ffn, w_up, cw, cb, w_down, final_norm,
                      seq=seq, tm=1024, tf=512, final_norm=(i == n_layers - 1))
    return x.reshape(batch, seq, d)


def kernel(x, mem, mem_norm,
           l0_norm_mix, l0_w_in, l0_ret_gn, l0_w_mem_kv, l0_w_o, l0_norm_ffn, l0_ffn_w_up, l0_ffn_conv_w, l0_ffn_conv_b, l0_ffn_w_down,
           l1_norm_mix, l1_w_in, l1_conv_w, l1_w_mem_kv, l1_w_o, l1_norm_ffn, l1_ffn_w_up, l1_ffn_conv_w, l1_ffn_conv_b, l1_ffn_w_down,
           l2_norm_mix, l2_w_in, l2_w_mem_kv, l2_w_o, l2_norm_ffn, l2_ffn_w_up, l2_ffn_conv_w, l2_ffn_conv_b, l2_ffn_w_down,
           l3_norm_mix, l3_w_in, l3_ret_gn, l3_w_mem_kv, l3_w_o, l3_norm_ffn, l3_ffn_w_up, l3_ffn_conv_w, l3_ffn_conv_b, l3_ffn_w_down,
           final_norm):
    layers = [
        (l0_norm_mix, l0_w_in, l0_ret_gn, l0_w_mem_kv, l0_w_o, l0_norm_ffn, l0_ffn_w_up, l0_ffn_conv_w, l0_ffn_conv_b, l0_ffn_w_down),
        (l1_norm_mix, l1_w_in, l1_conv_w, l1_w_mem_kv, l1_w_o, l1_norm_ffn, l1_ffn_w_up, l1_ffn_conv_w, l1_ffn_conv_b, l1_ffn_w_down),
        (l2_norm_mix, l2_w_in, None, l2_w_mem_kv, l2_w_o, l2_norm_ffn, l2_ffn_w_up, l2_ffn_conv_w, l2_ffn_conv_b, l2_ffn_w_down),
        (l3_norm_mix, l3_w_in, l3_ret_gn, l3_w_mem_kv, l3_w_o, l3_norm_ffn, l3_ffn_w_up, l3_ffn_conv_w, l3_ffn_conv_b, l3_ffn_w_down),
    ]
    return _trunk(x, mem, mem_norm, layers, final_norm)
```

```python
import functools
import math

import jax
import jax.numpy as jnp
import numpy as np
from jax import lax
from jax.experimental import pallas as pl
from jax.experimental.pallas import tpu as pltpu

F32 = jnp.float32
BF16 = jnp.bfloat16

NORM_EPS = 1e-6
MEM_HEADS = 4
RET_HEADS = 6
RET_CHUNK = 256
CONV_WIDTH = 3
MOBA_HEADS = 12
MOBA_BLOCK = 256
MOBA_TOPK = 3

SUBLANES = 8
VMEM_LIMIT_BYTES = 56 * 1024 * 1024

_NT_DIMS = (((1,), (1,)), ((), ()))
_TN_DIMS = (((0,), (0,)), ((), ()))


def _params(*semantics):
    return pltpu.CompilerParams(dimension_semantics=semantics,
                                vmem_limit_bytes=VMEM_LIMIT_BYTES)


def _rms_norm(x, gain):
    return x * lax.rsqrt(jnp.mean(x * x, axis=-1, keepdims=True) + NORM_EPS) * gain


def _silu(g):
    return g * (1.0 / (1.0 + jnp.exp(-g)))


def _shift_rows(u, prev, k):
    rolled = pltpu.roll(u, k, axis=0)
    prev_rolled = pltpu.roll(prev, k, axis=0)
    row = lax.broadcasted_iota(jnp.int32, prev.shape, 0)
    head = jnp.where(row < k, prev_rolled, rolled[:SUBLANES])
    return jnp.concatenate([head, rolled[SUBLANES:]], axis=0)


def _causal_conv3(u, prev, w):
    return (w[0:1] * _shift_rows(u, prev, 2) + w[1:2] * _shift_rows(u, prev, 1)
            + w[2:3] * u)


BF16_ROWS = 16


def _cast_specs(weights, n_steps, step_index):
    in_specs, out_specs, out_shapes = [], [], []
    for w in weights:
        rows, cols = w.shape
        assert rows % (n_steps * BF16_ROWS) == 0
        block = (rows // n_steps, cols)
        spec = pl.BlockSpec(block, lambda *idx: (step_index(*idx), 0))
        in_specs.append(spec)
        out_specs.append(spec)
        out_shapes.append(jax.ShapeDtypeStruct(w.shape, BF16))
    return in_specs, out_specs, out_shapes


def _cast_blocks(src_refs, dst_refs):
    for src, dst in zip(src_refs, dst_refs):
        dst[...] = src[...].astype(BF16)


def _norm_matmul_kernel(x_ref, g_ref, w_ref, *refs):
    n_casts = (len(refs) - 1) // 2
    o_ref = refs[n_casts]
    h = _rms_norm(x_ref[...], g_ref[...]).astype(BF16)
    o_ref[...] = jnp.dot(h, w_ref[...], preferred_element_type=F32).astype(o_ref.dtype)
    _cast_blocks(refs[:n_casts], refs[n_casts + 1:])


def _norm_matmul(x, gain, w, *, tm, tn, casts=()):
    m, d = x.shape
    n = w.shape[1]
    ni = m // tm
    cast_in, cast_out, cast_shapes = _cast_specs(casts, (n // tn) * ni, lambda j, i: j * ni + i)
    out, *casted = pl.pallas_call(
        _norm_matmul_kernel,
        grid=(n // tn, ni),
        in_specs=[
            pl.BlockSpec((tm, d), lambda j, i: (i, 0)),
            pl.BlockSpec((1, d), lambda j, i: (0, 0)),
            pl.BlockSpec((d, tn), lambda j, i: (0, j)),
        ] + cast_in,
        out_specs=[pl.BlockSpec((tm, tn), lambda j, i: (i, j))] + cast_out,
        out_shape=[jax.ShapeDtypeStruct((m, n), BF16)] + cast_shapes,
        compiler_params=_params("parallel", "parallel"),
        name="norm_matmul",
    )(x, gain.reshape(1, d), w, *casts)
    return out, casted


def _retention_kernel(lg_ref, q_ref, k_ref, v_ref, g_ref, gn_ref, o_ref, state_ref, *, chunk):
    lg = lg_ref[pl.program_id(1)]
    rows, d = q_ref.shape
    c = chunk
    k_scale = d ** -0.5

    @pl.when(pl.program_id(2) == 0)
    def _():
        state_ref[...] = jnp.zeros_like(state_ref)

    diff = (lax.broadcasted_iota(jnp.int32, (c, c), 0)
            - lax.broadcasted_iota(jnp.int32, (c, c), 1)).astype(F32)
    intra = jnp.where(diff >= 0, jnp.exp(jnp.maximum(diff, 0.0) * lg), 0.0) * k_scale
    pos = lax.broadcasted_iota(jnp.int32, (c, d), 0).astype(F32)
    q_decay = jnp.exp((pos + 1.0) * lg)
    k_decay = jnp.exp((c - 1.0 - pos) * lg) * k_scale
    chunk_decay = jnp.exp(jnp.zeros((1, d), F32) + c * lg)
    gn = gn_ref[...]

    for s in range(rows // c):
        sl = pl.ds(s * c, c)
        q = q_ref[sl, :]
        k = k_ref[sl, :]
        v = v_ref[sl, :]
        scores = lax.dot_general(q, k, _NT_DIMS, preferred_element_type=F32) * intra
        y = jnp.dot(scores.astype(BF16), v, preferred_element_type=F32)
        state = state_ref[...]
        y = y + jnp.dot(q, state.astype(BF16), preferred_element_type=F32) * q_decay
        kd = (k.astype(F32) * k_decay).astype(BF16)
        kv = lax.dot_general(kd, v, _TN_DIMS, preferred_element_type=F32)
        state_ref[...] = state * chunk_decay + kv
        g = g_ref[sl, :].astype(F32)
        o_ref[sl, :] = (_silu(g) * _rms_norm(y, gn)).astype(o_ref.dtype)


def _retention(proj, gn_gain, *, batch, seq, rows):
    m = proj.shape[0]
    d = gn_gain.shape[0] // RET_HEADS
    nt = seq // rows
    lg = np.log1p(-np.exp2(-5.0 - np.arange(RET_HEADS))).astype(np.float32)

    def spec(part):
        return pl.BlockSpec((rows, d), lambda b, h, t, lg_ref: (b * nt + t, part * RET_HEADS + h))

    return pl.pallas_call(
        functools.partial(_retention_kernel, chunk=RET_CHUNK),
        grid_spec=pltpu.PrefetchScalarGridSpec(
            num_scalar_prefetch=1,
            grid=(batch, RET_HEADS, nt),
            in_specs=[spec(0), spec(1), spec(2), spec(3),
                      pl.BlockSpec((1, d), lambda b, h, t, lg_ref: (0, h))],
            out_specs=pl.BlockSpec((rows, d), lambda b, h, t, lg_ref: (b * nt + t, h)),
            scratch_shapes=[pltpu.VMEM((d, d), F32)],
        ),
        out_shape=jax.ShapeDtypeStruct((m, RET_HEADS * d), BF16),
        compiler_params=_params("parallel", "parallel", "arbitrary"),
        name="retention",
    )(jnp.asarray(lg), proj, proj, proj, proj, gn_gain.reshape(1, -1))


def _alibi_slopes(n):
    def pow2_slopes(m):
        start = 2.0 ** (-8.0 / m)
        return [start ** (i + 1) for i in range(m)]
    if math.log2(n).is_integer():
        s = pow2_slopes(n)
    else:
        c = 2 ** int(math.floor(math.log2(n)))
        s = pow2_slopes(c) + list(_alibi_slopes(2 * c))[0::2][: n - c]
    return np.asarray(s, dtype=np.float32)


MOBA_MASKED = -1e30
AUG_PARTS = 3
AUG_PART_LANE = SUBLANES


def _split_bf16(x):
    parts = []
    for _ in range(AUG_PARTS):
        part = x.astype(BF16).astype(F32)
        parts.append(part)
        x = x - part
    return parts


def _moba_kernel(slope_ref, q_ref, k_ref, v_ref, o_ref, kmean_ref, vt_ref, kaug_ref, s_ref, *,
                 heads, group):
    qi = pl.program_id(2)
    blk = q_ref.shape[0]
    d = q_ref.shape[1] // heads
    nblk = k_ref.shape[0] // blk
    scale = d ** -0.5
    exp2_scale = scale * math.log2(math.e)
    assert AUG_PART_LANE + AUG_PARTS * nblk <= d

    def head_cols(t):
        return slice(t * d, (t + 1) * d)

    def head_slope(t):
        return slope_ref[pl.program_id(0) * heads + t]

    @pl.when((qi == 0) & (pl.program_id(1) == 0))
    def _():
        lane = lax.broadcasted_iota(jnp.int32, (blk, d), 1)
        row = lax.broadcasted_iota(jnp.int32, (blk, d), 0)
        for t in range(heads):
            key_scale = head_slope(t) * (1.0 / scale)
            for j in range(nblk):
                aug = jnp.zeros((blk, d), F32)
                for p, part in enumerate(_split_bf16((row + j * blk).astype(F32) * key_scale)):
                    aug = jnp.where(lane == p, part, aug)
                    aug = jnp.where(lane == AUG_PART_LANE + p * nblk + j, 1.0, aug)
                kaug_ref[t, j] = aug.astype(BF16)

    @pl.when(qi == 0)
    def _():
        for t in range(heads):
            for j in range(nblk):
                rows = slice(j * blk, (j + 1) * blk)
                kmean_ref[t, j:j + 1, :] = jnp.sum(k_ref[rows, head_cols(t)].astype(F32), axis=0,
                                                   keepdims=True) * (1.0 / blk)
                vt_ref[t, j] = v_ref[rows, head_cols(t)].astype(F32).T.astype(BF16)

    def query_operand(t):
        q = q_ref[:, head_cols(t)]
        kmean = kmean_ref[t]
        kmean_hi = kmean.astype(BF16)
        kmean_lo = (kmean - kmean_hi.astype(F32)).astype(BF16)
        gate = (lax.dot_general(kmean_hi, q, _NT_DIMS, preferred_element_type=F32)
                + lax.dot_general(kmean_lo, q, _NT_DIMS, preferred_element_type=F32))
        blk_id = lax.broadcasted_iota(jnp.int32, gate.shape, 0)
        rank = jnp.zeros(gate.shape, jnp.int32)
        for other in range(nblk):
            g_other = gate[other:other + 1, :]
            beats = (g_other > gate) | ((g_other == gate) & (other < blk_id))
            rank = rank + jnp.where(other < qi, jnp.where(beats, 1, 0), 0)
        keep = ((blk_id < qi) & (rank < MOBA_TOPK)) | (blk_id == qi)
        t_query = (qi * blk + lax.broadcasted_iota(jnp.int32, gate.shape, 1)).astype(F32)
        bias = (jnp.where(keep, 0.0, MOBA_MASKED) - head_slope(t) * t_query) * (1.0 / scale)
        ones = jnp.where(lax.broadcasted_iota(jnp.int32, (AUG_PART_LANE, blk), 0) < AUG_PARTS,
                         1.0, 0.0)
        pad = jnp.zeros((d - AUG_PART_LANE - AUG_PARTS * nblk, blk), F32)
        aug = jnp.concatenate([ones] + _split_bf16(bias) + [pad], axis=0)
        return jnp.concatenate([q, aug.T.astype(BF16)], axis=1)

    def scores(t, q_op, j):
        rows = pl.ds(pl.multiple_of(j * blk, blk), blk)
        k_op = jnp.concatenate([k_ref[rows, head_cols(t)], kaug_ref[t, j]], axis=1)
        return lax.dot_general(k_op, q_op, _NT_DIMS, preferred_element_type=F32)

    q_ops = [query_operand(t) for t in range(heads)]

    own_group = qi // group
    key_minus_query = (lax.broadcasted_iota(jnp.int32, (blk, blk), 0)
                       - lax.broadcasted_iota(jnp.int32, (blk, blk), 1))

    def group_scores(t, g, causal):
        best = None
        for i in range(group):
            j = g * group + i
            s = scores(t, q_ops[t], j) * exp2_scale
            if causal:
                s = jnp.where(key_minus_query <= (qi - j) * blk, s, MOBA_MASKED)
            s_ref[t, g, i] = s
            block_max = jnp.max(s, axis=0, keepdims=True)
            best = block_max if best is None else jnp.maximum(best, block_max)
        return best

    def past_scores(g, best):
        return tuple(jnp.maximum(best[t], group_scores(t, g, False)) for t in range(heads))

    best = lax.fori_loop(0, own_group, past_scores,
                         tuple(group_scores(t, own_group, True) for t in range(heads)))

    def group_values(g, carry):
        out = []
        for t in range(heads):
            l, acc = carry[t]
            ps = [jnp.exp2(s_ref[t, g, i] - best[t]) for i in range(group)]
            l = l + functools.reduce(jnp.add, [jnp.sum(p, axis=0, keepdims=True) for p in ps])
            p = jnp.concatenate([p.astype(BF16) for p in ps], axis=0)
            vt = jnp.concatenate([vt_ref[t, g * group + i] for i in range(group)], axis=1)
            out.append((l, acc + jnp.dot(vt, p, preferred_element_type=F32)))
        return tuple(out)

    zero = (jnp.zeros((1, blk), F32), jnp.zeros((d, blk), F32))
    totals = lax.fori_loop(0, own_group + 1, group_values, (zero,) * heads)
    for t in range(heads):
        l, acc = totals[t]
        o_ref[:, head_cols(t)] = (acc / l).T.astype(o_ref.dtype)


def _moba(proj, *, batch, seq, d, heads, group):
    m = proj.shape[0]
    blk = MOBA_BLOCK
    assert seq % (group * blk) == 0 and MOBA_HEADS % heads == 0
    nq = seq // blk
    groups = MOBA_HEADS // heads
    slopes = _alibi_slopes(MOBA_HEADS)
    return pl.pallas_call(
        functools.partial(_moba_kernel, heads=heads, group=group),
        grid_spec=pltpu.PrefetchScalarGridSpec(
            num_scalar_prefetch=1,
            grid=(groups, batch, nq),
            in_specs=[
                pl.BlockSpec((blk, heads * d), lambda h, b, i, s_ref: (b * nq + i, h)),
                pl.BlockSpec((seq, heads * d), lambda h, b, i, s_ref: (b, groups + h)),
                pl.BlockSpec((seq, heads * d), lambda h, b, i, s_ref: (b, 2 * groups + h)),
            ],
            out_specs=pl.BlockSpec((blk, heads * d), lambda h, b, i, s_ref: (b * nq + i, h)),
            scratch_shapes=[pltpu.VMEM((heads, nq, d), F32),
                            pltpu.VMEM((heads, nq, d, blk), BF16),
                            pltpu.VMEM((heads, nq, blk, d), BF16),
                            pltpu.VMEM((heads, nq // group, group, blk, blk), F32)],
        ),
        out_shape=jax.ShapeDtypeStruct((m, MOBA_HEADS * d), BF16),
        compiler_params=_params("parallel", "arbitrary", "arbitrary"),
        name="moba",
    )(jnp.asarray(slopes), proj, proj, proj)


def _memory_attention(qm_ref, kv_ref):
    width = qm_ref.shape[1]
    dh = width // MEM_HEADS
    outs = []
    for h in range(MEM_HEADS):
        q = qm_ref[:, h * dh:(h + 1) * dh]
        k = kv_ref[:, h * dh:(h + 1) * dh]
        v = kv_ref[:, width + h * dh:width + (h + 1) * dh]
        s = lax.dot_general(q, k, _NT_DIMS, preferred_element_type=F32) * dh ** -0.5
        p = jnp.exp(s - jnp.max(s, axis=-1, keepdims=True))
        o = jnp.dot(p.astype(BF16), v, preferred_element_type=F32)
        outs.append((o / jnp.sum(p, axis=-1, keepdims=True)).astype(BF16))
    return jnp.concatenate(outs, axis=-1)


def _project_out(tok, mem_out, x_ref, wo_ref, o_ref):
    mix = tok.shape[1]
    o_ref[...] = (x_ref[...]
                  + jnp.dot(tok, wo_ref[:mix, :], preferred_element_type=F32)
                  + jnp.dot(mem_out, wo_ref[mix:, :], preferred_element_type=F32))


def _out_proj_kernel(tok_ref, qm_ref, kv_ref, x_ref, wo_ref, *refs):
    n_casts = (len(refs) - 1) // 2
    _project_out(tok_ref[...], _memory_attention(qm_ref, kv_ref), x_ref, wo_ref, refs[n_casts])
    _cast_blocks(refs[:n_casts], refs[n_casts + 1:])


def _conv_out_proj_kernel(gb_ref, gc_ref, hh_ref, gc_prev_ref, hh_prev_ref, cw_ref,
                          qm_ref, kv_ref, x_ref, wo_ref, *refs, tiles_per_seq):
    n_casts = (len(refs) - 1) // 2
    first = pl.program_id(0) % tiles_per_seq == 0
    u = gc_ref[...].astype(F32) * hh_ref[...].astype(F32)
    prev = gc_prev_ref[...].astype(F32) * hh_prev_ref[...].astype(F32)
    prev = jnp.where(first, 0.0, prev)
    tok = (gb_ref[...].astype(F32) * _causal_conv3(u, prev, cw_ref[...])).astype(BF16)
    _project_out(tok, _memory_attention(qm_ref, kv_ref), x_ref, wo_ref, refs[n_casts])
    _cast_blocks(refs[:n_casts], refs[n_casts + 1:])


def _out_proj(tok, proj, conv_w, kv, x, wo, *, seq, mem_tokens, tm, casts=()):
    m, d = x.shape
    mem_width = kv.shape[1] // 2
    mix = d - mem_width
    tiles_per_seq = seq // tm
    qm_block = proj.shape[1] // mem_width - 1
    cast_in, cast_out, cast_shapes = _cast_specs(casts, m // tm, lambda i: i)
    tail_specs = [
        pl.BlockSpec((tm, mem_width), lambda i: (i, qm_block)),
        pl.BlockSpec((mem_tokens, 2 * mem_width), lambda i: (i // tiles_per_seq, 0)),
        pl.BlockSpec((tm, d), lambda i: (i, 0)),
        pl.BlockSpec((d, d), lambda i: (0, 0), pipeline_mode=pl.Buffered(1)),
    ] + cast_in
    if tok is not None:
        kernel = _out_proj_kernel
        head_specs = [pl.BlockSpec((tm, mix), lambda i: (i, 0))]
        head_args = (tok,)
    else:
        kernel = functools.partial(_conv_out_proj_kernel, tiles_per_seq=tiles_per_seq)
        halo = tm // SUBLANES

        def prev_rows(part):
            return pl.BlockSpec((SUBLANES, mix), lambda i: (jnp.maximum(i * halo - 1, 0), part))

        head_specs = [pl.BlockSpec((tm, mix), lambda i: (i, 0)),
                      pl.BlockSpec((tm, mix), lambda i: (i, 1)),
                      pl.BlockSpec((tm, mix), lambda i: (i, 2)),
                      prev_rows(1), prev_rows(2),
                      pl.BlockSpec((CONV_WIDTH, mix), lambda i: (0, 0))]
        head_args = (proj, proj, proj, proj, proj, conv_w)
    out, *casted = pl.pallas_call(
        kernel,
        grid=(m // tm,),
        in_specs=head_specs + tail_specs,
        out_specs=[pl.BlockSpec((tm, d), lambda i: (i, 0))] + cast_out,
        out_shape=[jax.ShapeDtypeStruct((m, d), F32)] + cast_shapes,
        compiler_params=_params("parallel"),
        name="out_proj",
    )(*head_args, proj, kv, x, wo, *casts)
    return out, casted


def _conv_ffn_kernel(x_ref, g_ref, wg_ref, wu_ref, conv_ref, wd_ref,
                     fg_ref, o_ref, h_ref, carry_ref, *, tiles_per_seq, row_strips, final_norm):
    f = pl.program_id(1)
    nf = pl.num_programs(1)
    first = pl.program_id(0) % tiles_per_seq == 0
    tm = x_ref.shape[0]

    @pl.when(f == 0)
    def _():
        x = x_ref[...]
        h_ref[...] = _rms_norm(x, g_ref[...]).astype(BF16)
        o_ref[...] = x

    @pl.when(first)
    def _():
        carry_ref[f] = jnp.zeros(carry_ref.shape[1:], F32)

    strip = tm // row_strips
    ups = [[jnp.dot(h_ref[pl.ds(s * strip, strip), :], w_ref[...], preferred_element_type=F32)
            for w_ref in (wg_ref, wu_ref)] for s in range(row_strips)]
    conv = [conv_ref[f + slot * nf] for slot in range(2)]
    for s in range(row_strips):
        branches = []
        for slot in range(2):
            if s == 0:
                prev = carry_ref[f, slot]
            else:
                prev = ups[s - 1][slot][strip - SUBLANES:, :]
            branches.append(_causal_conv3(ups[s][slot], prev, conv[slot][:CONV_WIDTH])
                            + conv[slot][CONV_WIDTH:CONV_WIDTH + 1])
        act = (_silu(branches[0]) * branches[1]).astype(BF16)
        o_ref[pl.ds(s * strip, strip), :] += jnp.dot(act, wd_ref[...], preferred_element_type=F32)
    for slot in range(2):
        carry_ref[f, slot] = ups[row_strips - 1][slot][strip - SUBLANES:, :]

    if final_norm:
        @pl.when(f == pl.num_programs(1) - 1)
        def _():
            o_ref[...] = _rms_norm(o_ref[...], fg_ref[...])


def _conv_ffn(x, gain, w_up, conv_w, conv_b, w_down, final_gain, *, seq, tm, tf, final_norm):
    m, d = x.shape
    d_ff = w_down.shape[0]
    nf = d_ff // tf
    conv = jnp.concatenate(
        [conv_w.reshape(CONV_WIDTH, 2 * nf, tf).transpose(1, 0, 2),
         conv_b.reshape(2 * nf, 1, tf),
         jnp.zeros((2 * nf, SUBLANES - CONV_WIDTH - 1, tf), F32)], axis=1)
    return pl.pallas_call(
        functools.partial(_conv_ffn_kernel, tiles_per_seq=seq // tm, row_strips=FFN_ROW_STRIPS,
                          final_norm=final_norm),
        grid=(m // tm, nf),
        in_specs=[
            pl.BlockSpec((tm, d), lambda i, f: (i, 0)),
            pl.BlockSpec((1, d), lambda i, f: (0, 0)),
            pl.BlockSpec((d, tf), lambda i, f: (0, f)),
            pl.BlockSpec((d, tf), lambda i, f: (0, nf + f)),
            pl.BlockSpec((2 * nf, SUBLANES, tf), lambda i, f: (0, 0, 0)),
            pl.BlockSpec((tf, d), lambda i, f: (f, 0)),
            pl.BlockSpec((1, d), lambda i, f: (0, 0)),
        ],
        out_specs=pl.BlockSpec((tm, d), lambda i, f: (i, 0)),
        out_shape=jax.ShapeDtypeStruct((m, d), F32),
        scratch_shapes=[pltpu.VMEM((tm, d), BF16),
                        pltpu.VMEM((nf, 2, SUBLANES, tf), F32)],
        compiler_params=_params("arbitrary", "arbitrary"),
        name="conv_ffn",
    )(x, gain.reshape(1, d), w_up, w_up, conv, w_down, final_gain.reshape(1, d))


MXU_WIDTH = 256

PROJ_ROWS = 512
FFN_ROWS = 1024
FFN_COLS = 512
FFN_ROW_STRIPS = 2
MOBA_HEADS_PER_STEP = 4
MOBA_GROUP = 4


def _half_tile(n):
    return n // 2 if n % (2 * MXU_WIDTH) == 0 else n


def _trunk(x, mem, mem_norm, layers, final_norm):
    batch, seq, d = x.shape
    mem_tokens = mem.shape[1]
    x = x.reshape(batch * seq, d)
    mem = mem.reshape(batch * mem_tokens, d)
    n_layers = len(layers)
    w_in = layers[0][1].astype(BF16)
    w_mem_kv = layers[0][3].astype(BF16)
    for i, (norm_mix, _, extra, _, w_o, norm_ffn, w_up, cw, cb, w_down) in enumerate(layers):
        kind = i % 3
        mem_width = w_mem_kv.shape[1] // 2
        mix = d - mem_width
        proj, (w_up, w_o) = _norm_matmul(x, norm_mix, w_in, tm=PROJ_ROWS,
                                         tn=_half_tile(w_in.shape[1]), casts=(w_up, w_o))
        kv, _ = _norm_matmul(mem, mem_norm, w_mem_kv, tm=mem_tokens, tn=w_mem_kv.shape[1])
        if kind == 0:
            tok = _retention(proj, extra, batch=batch, seq=seq, rows=seq)
        elif kind == 1:
            tok = None
        else:
            tok = _moba(proj, batch=batch, seq=seq, d=mix // MOBA_HEADS,
                        heads=MOBA_HEADS_PER_STEP, group=MOBA_GROUP)
        next_weights = (layers[i + 1][1], layers[i + 1][3]) if i + 1 < n_layers else ()
        x, (w_down, *next_weights) = _out_proj(tok, proj, extra, kv, x, w_o, seq=seq,
                                               mem_tokens=mem_tokens, tm=PROJ_ROWS,
                                               casts=(w_down,) + next_weights)
        if next_weights:
            w_in, w_mem_kv = next_weights
        x = _conv_ffn(x, norm_ffn, w_up, cw, cb, w_down, final_norm, seq=seq, tm=FFN_ROWS,
                      tf=FFN_COLS, final_norm=(i == n_layers - 1))
    return x.reshape(batch, seq, d)


def kernel(x, mem, mem_norm,
           l0_norm_mix, l0_w_in, l0_ret_gn, l0_w_mem_kv, l0_w_o, l0_norm_ffn, l0_ffn_w_up, l0_ffn_conv_w, l0_ffn_conv_b, l0_ffn_w_down,
           l1_norm_mix, l1_w_in, l1_conv_w, l1_w_mem_kv, l1_w_o, l1_norm_ffn, l1_ffn_w_up, l1_ffn_conv_w, l1_ffn_conv_b, l1_ffn_w_down,
           l2_norm_mix, l2_w_in, l2_w_mem_kv, l2_w_o, l2_norm_ffn, l2_ffn_w_up, l2_ffn_conv_w, l2_ffn_conv_b, l2_ffn_w_down,
           l3_norm_mix, l3_w_in, l3_ret_gn, l3_w_mem_kv, l3_w_o, l3_norm_ffn, l3_ffn_w_up, l3_ffn_conv_w, l3_ffn_conv_b, l3_ffn_w_down,
           final_norm):
    layers = [
        (l0_norm_mix, l0_w_in, l0_ret_gn, l0_w_mem_kv, l0_w_o, l0_norm_ffn, l0_ffn_w_up, l0_ffn_conv_w, l0_ffn_conv_b, l0_ffn_w_down),
        (l1_norm_mix, l1_w_in, l1_conv_w, l1_w_mem_kv, l1_w_o, l1_norm_ffn, l1_ffn_w_up, l1_ffn_conv_w, l1_ffn_conv_b, l1_ffn_w_down),
        (l2_norm_mix, l2_w_in, None, l2_w_mem_kv, l2_w_o, l2_norm_ffn, l2_ffn_w_up, l2_ffn_conv_w, l2_ffn_conv_b, l2_ffn_w_down),
        (l3_norm_mix, l3_w_in, l3_ret_gn, l3_w_mem_kv, l3_w_o, l3_norm_ffn, l3_ffn_w_up, l3_ffn_conv_w, l3_ffn_conv_b, l3_ffn_w_down),
    ]
    return _trunk(x, mem, mem_norm, layers, final_norm)
```

```python
import functools
import math

import jax
import jax.numpy as jnp
import numpy as np
from jax import lax
from jax.experimental import pallas as pl
from jax.experimental.pallas import tpu as pltpu

F32 = jnp.float32
BF16 = jnp.bfloat16

NORM_EPS = 1e-6
MEM_HEADS = 4
RET_HEADS = 6
RET_CHUNK = 256
CONV_WIDTH = 3
MOBA_HEADS = 12
MOBA_BLOCK = 256
MOBA_TOPK = 3

SUBLANES = 8
VMEM_LIMIT_BYTES = 56 * 1024 * 1024

_NT_DIMS = (((1,), (1,)), ((), ()))
_TN_DIMS = (((0,), (0,)), ((), ()))


def _params(*semantics):
    return pltpu.CompilerParams(dimension_semantics=semantics,
                                vmem_limit_bytes=VMEM_LIMIT_BYTES)


def _rms_norm(x, gain):
    return x * lax.rsqrt(jnp.mean(x * x, axis=-1, keepdims=True) + NORM_EPS) * gain


def _silu(g):
    return g * (1.0 / (1.0 + jnp.exp(-g)))


def _shift_rows(u, prev, k):
    rolled = pltpu.roll(u, k, axis=0)
    prev_rolled = pltpu.roll(prev, k, axis=0)
    row = lax.broadcasted_iota(jnp.int32, prev.shape, 0)
    head = jnp.where(row < k, prev_rolled, rolled[:SUBLANES])
    return jnp.concatenate([head, rolled[SUBLANES:]], axis=0)


def _causal_conv3(u, prev, w):
    return (w[0:1] * _shift_rows(u, prev, 2) + w[1:2] * _shift_rows(u, prev, 1)
            + w[2:3] * u)


BF16_ROWS = 16


def _cast_specs(weights, n_steps, step_index):
    in_specs, out_specs, out_shapes = [], [], []
    for w in weights:
        rows, cols = w.shape
        assert rows % (n_steps * BF16_ROWS) == 0
        block = (rows // n_steps, cols)
        spec = pl.BlockSpec(block, lambda *idx: (step_index(*idx), 0))
        in_specs.append(spec)
        out_specs.append(spec)
        out_shapes.append(jax.ShapeDtypeStruct(w.shape, BF16))
    return in_specs, out_specs, out_shapes


def _cast_blocks(src_refs, dst_refs):
    for src, dst in zip(src_refs, dst_refs):
        dst[...] = src[...].astype(BF16)


def _norm_matmul_kernel(x_ref, g_ref, w_ref, *refs):
    n_casts = (len(refs) - 1) // 2
    o_ref = refs[n_casts]
    h = _rms_norm(x_ref[...], g_ref[...]).astype(BF16)
    o_ref[...] = jnp.dot(h, w_ref[...], preferred_element_type=F32).astype(o_ref.dtype)
    _cast_blocks(refs[:n_casts], refs[n_casts + 1:])


def _norm_matmul(x, gain, w, *, tm, tn, casts=()):
    m, d = x.shape
    n = w.shape[1]
    ni = m // tm
    cast_in, cast_out, cast_shapes = _cast_specs(casts, (n // tn) * ni, lambda j, i: j * ni + i)
    out, *casted = pl.pallas_call(
        _norm_matmul_kernel,
        grid=(n // tn, ni),
        in_specs=[
            pl.BlockSpec((tm, d), lambda j, i: (i, 0)),
            pl.BlockSpec((1, d), lambda j, i: (0, 0)),
            pl.BlockSpec((d, tn), lambda j, i: (0, j)),
        ] + cast_in,
        out_specs=[pl.BlockSpec((tm, tn), lambda j, i: (i, j))] + cast_out,
        out_shape=[jax.ShapeDtypeStruct((m, n), BF16)] + cast_shapes,
        compiler_params=_params("parallel", "parallel"),
        name="norm_matmul",
    )(x, gain.reshape(1, d), w, *casts)
    return out, casted


def _retention_kernel(lg_ref, q_ref, k_ref, v_ref, g_ref, gn_ref, o_ref, state_ref, *, chunk):
    lg = lg_ref[pl.program_id(1)]
    rows, d = q_ref.shape
    c = chunk
    k_scale = d ** -0.5

    @pl.when(pl.program_id(2) == 0)
    def _():
        state_ref[...] = jnp.zeros_like(state_ref)

    diff = (lax.broadcasted_iota(jnp.int32, (c, c), 0)
            - lax.broadcasted_iota(jnp.int32, (c, c), 1)).astype(F32)
    intra = jnp.where(diff >= 0, jnp.exp(jnp.maximum(diff, 0.0) * lg), 0.0) * k_scale
    pos = lax.broadcasted_iota(jnp.int32, (c, d), 0).astype(F32)
    q_decay = jnp.exp((pos + 1.0) * lg)
    k_decay = jnp.exp((c - 1.0 - pos) * lg) * k_scale
    chunk_decay = jnp.exp(jnp.zeros((1, d), F32) + c * lg)
    gn = gn_ref[...]

    for s in range(rows // c):
        sl = pl.ds(s * c, c)
        q = q_ref[sl, :]
        k = k_ref[sl, :]
        v = v_ref[sl, :]
        scores = lax.dot_general(q, k, _NT_DIMS, preferred_element_type=F32) * intra
        y = jnp.dot(scores.astype(BF16), v, preferred_element_type=F32)
        state = state_ref[...]
        y = y + jnp.dot(q, state.astype(BF16), preferred_element_type=F32) * q_decay
        kd = (k.astype(F32) * k_decay).astype(BF16)
        kv = lax.dot_general(kd, v, _TN_DIMS, preferred_element_type=F32)
        state_ref[...] = state * chunk_decay + kv
        g = g_ref[sl, :].astype(F32)
        o_ref[sl, :] = (_silu(g) * _rms_norm(y, gn)).astype(o_ref.dtype)


def _retention(proj, gn_gain, *, batch, seq, rows):
    m = proj.shape[0]
    d = gn_gain.shape[0] // RET_HEADS
    nt = seq // rows
    lg = np.log1p(-np.exp2(-5.0 - np.arange(RET_HEADS))).astype(np.float32)

    def spec(part):
        return pl.BlockSpec((rows, d), lambda b, h, t, lg_ref: (b * nt + t, part * RET_HEADS + h))

    return pl.pallas_call(
        functools.partial(_retention_kernel, chunk=RET_CHUNK),
        grid_spec=pltpu.PrefetchScalarGridSpec(
            num_scalar_prefetch=1,
            grid=(batch, RET_HEADS, nt),
            in_specs=[spec(0), spec(1), spec(2), spec(3),
                      pl.BlockSpec((1, d), lambda b, h, t, lg_ref: (0, h))],
            out_specs=pl.BlockSpec((rows, d), lambda b, h, t, lg_ref: (b * nt + t, h)),
            scratch_shapes=[pltpu.VMEM((d, d), F32)],
        ),
        out_shape=jax.ShapeDtypeStruct((m, RET_HEADS * d), BF16),
        compiler_params=_params("parallel", "parallel", "arbitrary"),
        name="retention",
    )(jnp.asarray(lg), proj, proj, proj, proj, gn_gain.reshape(1, -1))


def _alibi_slopes(n):
    def pow2_slopes(m):
        start = 2.0 ** (-8.0 / m)
        return [start ** (i + 1) for i in range(m)]
    if math.log2(n).is_integer():
        s = pow2_slopes(n)
    else:
        c = 2 ** int(math.floor(math.log2(n)))
        s = pow2_slopes(c) + list(_alibi_slopes(2 * c))[0::2][: n - c]
    return np.asarray(s, dtype=np.float32)


MOBA_MASKED = -1e30
AUG_PARTS = 3
AUG_PART_LANE = SUBLANES


def _split_bf16(x):
    parts = []
    for _ in range(AUG_PARTS):
        part = x.astype(BF16).astype(F32)
        parts.append(part)
        x = x - part
    return parts


def _moba_kernel(slope_ref, q_ref, k_ref, v_ref, o_ref, kmean_ref, vt_ref, kaug_ref, s_ref, *,
                 heads, group):
    qi = pl.program_id(2)
    blk = q_ref.shape[0]
    d = q_ref.shape[1] // heads
    nblk = k_ref.shape[0] // blk
    scale = d ** -0.5
    exp2_scale = scale * math.log2(math.e)
    assert AUG_PART_LANE + AUG_PARTS * nblk <= d

    def head_cols(t):
        return slice(t * d, (t + 1) * d)

    def head_slope(t):
        return slope_ref[pl.program_id(0) * heads + t]

    @pl.when((qi == 0) & (pl.program_id(1) == 0))
    def _():
        lane = lax.broadcasted_iota(jnp.int32, (blk, d), 1)
        row = lax.broadcasted_iota(jnp.int32, (blk, d), 0)
        for t in range(heads):
            key_scale = head_slope(t) * (1.0 / scale)
            for j in range(nblk):
                aug = jnp.zeros((blk, d), F32)
                for p, part in enumerate(_split_bf16((row + j * blk).astype(F32) * key_scale)):
                    aug = jnp.where(lane == p, part, aug)
                    aug = jnp.where(lane == AUG_PART_LANE + p * nblk + j, 1.0, aug)
                kaug_ref[t, j] = aug.astype(BF16)

    @pl.when(qi == 0)
    def _():
        for t in range(heads):
            for j in range(nblk):
                rows = slice(j * blk, (j + 1) * blk)
                kmean_ref[t, j:j + 1, :] = jnp.sum(k_ref[rows, head_cols(t)].astype(F32), axis=0,
                                                   keepdims=True) * (1.0 / blk)
                vt_ref[t, j] = v_ref[rows, head_cols(t)].astype(F32).T.astype(BF16)

    def query_operand(t):
        q = q_ref[:, head_cols(t)]
        kmean = kmean_ref[t]
        kmean_hi = kmean.astype(BF16)
        kmean_lo = (kmean - kmean_hi.astype(F32)).astype(BF16)
        gate = (lax.dot_general(kmean_hi, q, _NT_DIMS, preferred_element_type=F32)
                + lax.dot_general(kmean_lo, q, _NT_DIMS, preferred_element_type=F32))
        blk_id = lax.broadcasted_iota(jnp.int32, gate.shape, 0)
        rank = jnp.zeros(gate.shape, jnp.int32)
        for other in range(nblk):
            g_other = gate[other:other + 1, :]
            beats = (g_other > gate) | ((g_other == gate) & (other < blk_id))
            rank = rank + jnp.where(other < qi, jnp.where(beats, 1, 0), 0)
        keep = ((blk_id < qi) & (rank < MOBA_TOPK)) | (blk_id == qi)
        t_query = (qi * blk + lax.broadcasted_iota(jnp.int32, gate.shape, 1)).astype(F32)
        bias = (jnp.where(keep, 0.0, MOBA_MASKED) - head_slope(t) * t_query) * (1.0 / scale)
        ones = jnp.where(lax.broadcasted_iota(jnp.int32, (AUG_PART_LANE, blk), 0) < AUG_PARTS,
                         1.0, 0.0)
        pad = jnp.zeros((d - AUG_PART_LANE - AUG_PARTS * nblk, blk), F32)
        aug = jnp.concatenate([ones] + _split_bf16(bias) + [pad], axis=0)
        return jnp.concatenate([q, aug.T.astype(BF16)], axis=1)

    def scores(t, q_op, j):
        rows = pl.ds(pl.multiple_of(j * blk, blk), blk)
        k_op = jnp.concatenate([k_ref[rows, head_cols(t)], kaug_ref[t, j]], axis=1)
        return lax.dot_general(k_op, q_op, _NT_DIMS, preferred_element_type=F32)

    q_ops = [query_operand(t) for t in range(heads)]

    own_group = qi // group
    key_minus_query = (lax.broadcasted_iota(jnp.int32, (blk, blk), 0)
                       - lax.broadcasted_iota(jnp.int32, (blk, blk), 1))

    def group_scores(t, g, causal):
        best = None
        for i in range(group):
            j = g * group + i
            s = scores(t, q_ops[t], j) * exp2_scale
            if causal:
                s = jnp.where(key_minus_query <= (qi - j) * blk, s, MOBA_MASKED)
            s_ref[t, g, i] = s
            block_max = jnp.max(s, axis=0, keepdims=True)
            best = block_max if best is None else jnp.maximum(best, block_max)
        return best

    def past_scores(g, best):
        return tuple(jnp.maximum(best[t], group_scores(t, g, False)) for t in range(heads))

    best = lax.fori_loop(0, own_group, past_scores,
                         tuple(group_scores(t, own_group, True) for t in range(heads)))

    def group_values(g, carry):
        out = []
        for t in range(heads):
            l, acc = carry[t]
            ps = [jnp.exp2(s_ref[t, g, i] - best[t]) for i in range(group)]
            l = l + functools.reduce(jnp.add, [jnp.sum(p, axis=0, keepdims=True) for p in ps])
            p = jnp.concatenate([p.astype(BF16) for p in ps], axis=0)
            vt = jnp.concatenate([vt_ref[t, g * group + i] for i in range(group)], axis=1)
            out.append((l, acc + jnp.dot(vt, p, preferred_element_type=F32)))
        return tuple(out)

    zero = (jnp.zeros((1, blk), F32), jnp.zeros((d, blk), F32))
    totals = lax.fori_loop(0, own_group + 1, group_values, (zero,) * heads)
    for t in range(heads):
        l, acc = totals[t]
        o_ref[:, head_cols(t)] = (acc / l).T.astype(o_ref.dtype)


def _moba(proj, *, batch, seq, d, heads, group):
    m = proj.shape[0]
    blk = MOBA_BLOCK
    assert seq % (group * blk) == 0 and MOBA_HEADS % heads == 0
    nq = seq // blk
    groups = MOBA_HEADS // heads
    slopes = _alibi_slopes(MOBA_HEADS)
    return pl.pallas_call(
        functools.partial(_moba_kernel, heads=heads, group=group),
        grid_spec=pltpu.PrefetchScalarGridSpec(
            num_scalar_prefetch=1,
            grid=(groups, batch, nq),
            in_specs=[
                pl.BlockSpec((blk, heads * d), lambda h, b, i, s_ref: (b * nq + i, h)),
                pl.BlockSpec((seq, heads * d), lambda h, b, i, s_ref: (b, groups + h)),
                pl.BlockSpec((seq, heads * d), lambda h, b, i, s_ref: (b, 2 * groups + h)),
            ],
            out_specs=pl.BlockSpec((blk, heads * d), lambda h, b, i, s_ref: (b * nq + i, h)),
            scratch_shapes=[pltpu.VMEM((heads, nq, d), F32),
                            pltpu.VMEM((heads, nq, d, blk), BF16),
                            pltpu.VMEM((heads, nq, blk, d), BF16),
                            pltpu.VMEM((heads, nq // group, group, blk, blk), F32)],
        ),
        out_shape=jax.ShapeDtypeStruct((m, MOBA_HEADS * d), BF16),
        compiler_params=_params("parallel", "arbitrary", "arbitrary"),
        name="moba",
    )(jnp.asarray(slopes), proj, proj, proj)


def _memory_attention(qm_ref, kv_ref):
    width = qm_ref.shape[1]
    dh = width // MEM_HEADS
    outs = []
    for h in range(MEM_HEADS):
        q = qm_ref[:, h * dh:(h + 1) * dh]
        k = kv_ref[:, h * dh:(h + 1) * dh]
        v = kv_ref[:, width + h * dh:width + (h + 1) * dh]
        s = lax.dot_general(q, k, _NT_DIMS, preferred_element_type=F32) * dh ** -0.5
        p = jnp.exp(s - jnp.max(s, axis=-1, keepdims=True))
        o = jnp.dot(p.astype(BF16), v, preferred_element_type=F32)
        outs.append((o / jnp.sum(p, axis=-1, keepdims=True)).astype(BF16))
    return jnp.concatenate(outs, axis=-1)


def _project_out(tok, mem_out, x_ref, wo_ref, o_ref):
    mix = tok.shape[1]
    o_ref[...] = (x_ref[...]
                  + jnp.dot(tok, wo_ref[:mix, :], preferred_element_type=F32)
                  + jnp.dot(mem_out, wo_ref[mix:, :], preferred_element_type=F32))


def _out_proj_kernel(tok_ref, qm_ref, kv_ref, x_ref, wo_ref, *refs):
    n_casts = (len(refs) - 1) // 2
    _project_out(tok_ref[...], _memory_attention(qm_ref, kv_ref), x_ref, wo_ref, refs[n_casts])
    _cast_blocks(refs[:n_casts], refs[n_casts + 1:])


def _conv_out_proj_kernel(gb_ref, gc_ref, hh_ref, gc_prev_ref, hh_prev_ref, cw_ref,
                          qm_ref, kv_ref, x_ref, wo_ref, *refs, tiles_per_seq):
    n_casts = (len(refs) - 1) // 2
    first = pl.program_id(0) % tiles_per_seq == 0
    u = gc_ref[...].astype(F32) * hh_ref[...].astype(F32)
    prev = gc_prev_ref[...].astype(F32) * hh_prev_ref[...].astype(F32)
    prev = jnp.where(first, 0.0, prev)
    tok = (gb_ref[...].astype(F32) * _causal_conv3(u, prev, cw_ref[...])).astype(BF16)
    _project_out(tok, _memory_attention(qm_ref, kv_ref), x_ref, wo_ref, refs[n_casts])
    _cast_blocks(refs[:n_casts], refs[n_casts + 1:])


def _out_proj(tok, proj, conv_w, kv, x, wo, *, seq, mem_tokens, tm, casts=()):
    m, d = x.shape
    mem_width = kv.shape[1] // 2
    mix = d - mem_width
    tiles_per_seq = seq // tm
    qm_block = proj.shape[1] // mem_width - 1
    cast_in, cast_out, cast_shapes = _cast_specs(casts, m // tm, lambda i: i)
    tail_specs = [
        pl.BlockSpec((tm, mem_width), lambda i: (i, qm_block)),
        pl.BlockSpec((mem_tokens, 2 * mem_width), lambda i: (i // tiles_per_seq, 0)),
        pl.BlockSpec((tm, d), lambda i: (i, 0)),
        pl.BlockSpec((d, d), lambda i: (0, 0), pipeline_mode=pl.Buffered(1)),
    ] + cast_in
    if tok is not None:
        kernel = _out_proj_kernel
        head_specs = [pl.BlockSpec((tm, mix), lambda i: (i, 0))]
        head_args = (tok,)
    else:
        kernel = functools.partial(_conv_out_proj_kernel, tiles_per_seq=tiles_per_seq)
        halo = tm // SUBLANES

        def prev_rows(part):
            return pl.BlockSpec((SUBLANES, mix), lambda i: (jnp.maximum(i * halo - 1, 0), part))

        head_specs = [pl.BlockSpec((tm, mix), lambda i: (i, 0)),
                      pl.BlockSpec((tm, mix), lambda i: (i, 1)),
                      pl.BlockSpec((tm, mix), lambda i: (i, 2)),
                      prev_rows(1), prev_rows(2),
                      pl.BlockSpec((CONV_WIDTH, mix), lambda i: (0, 0))]
        head_args = (proj, proj, proj, proj, proj, conv_w)
    out, *casted = pl.pallas_call(
        kernel,
        grid=(m // tm,),
        in_specs=head_specs + tail_specs,
        out_specs=[pl.BlockSpec((tm, d), lambda i: (i, 0))] + cast_out,
        out_shape=[jax.ShapeDtypeStruct((m, d), F32)] + cast_shapes,
        compiler_params=_params("parallel"),
        name="out_proj",
    )(*head_args, proj, kv, x, wo, *casts)
    return out, casted


def _conv_ffn_kernel(x_ref, g_ref, wg_ref, wu_ref, conv_ref, wd_ref,
                     fg_ref, o_ref, h_ref, carry_ref, up_ref, *,
                     tiles_per_seq, row_strips, final_norm):
    f = pl.program_id(1)
    nf = pl.num_programs(1)
    first = pl.program_id(0) % tiles_per_seq == 0
    tm = x_ref.shape[0]

    @pl.when(f == 0)
    def _():
        x = x_ref[...]
        h_ref[...] = _rms_norm(x, g_ref[...]).astype(BF16)
        o_ref[...] = x

    @pl.when(first)
    def _():
        carry_ref[f] = jnp.zeros(carry_ref.shape[1:], F32)

    strip = tm // row_strips
    for slot in range(2):
        up_ref[slot, :SUBLANES, :] = carry_ref[f, slot]
    for s in range(row_strips):
        for slot, w_ref in enumerate((wg_ref, wu_ref)):
            up_ref[slot, pl.ds(SUBLANES + s * strip, strip), :] = jnp.dot(
                h_ref[pl.ds(s * strip, strip), :], w_ref[...], preferred_element_type=F32)
    conv = [conv_ref[f + slot * nf] for slot in range(2)]
    for s in range(row_strips):
        branches = []
        for slot in range(2):
            taps = [up_ref[slot, pl.ds(SUBLANES + s * strip - (CONV_WIDTH - 1 - k), strip), :]
                    for k in range(CONV_WIDTH)]
            branches.append(conv[slot][0:1] * taps[0] + conv[slot][1:2] * taps[1]
                            + conv[slot][2:3] * taps[2] + conv[slot][CONV_WIDTH:CONV_WIDTH + 1])
        act = (_silu(branches[0]) * branches[1]).astype(BF16)
        o_ref[pl.ds(s * strip, strip), :] += jnp.dot(act, wd_ref[...], preferred_element_type=F32)
    for slot in range(2):
        carry_ref[f, slot] = up_ref[slot, pl.ds(tm, SUBLANES), :]

    if final_norm:
        @pl.when(f == pl.num_programs(1) - 1)
        def _():
            o_ref[...] = _rms_norm(o_ref[...], fg_ref[...])


def _conv_ffn(x, gain, w_up, conv_w, conv_b, w_down, final_gain, *, seq, tm, tf, final_norm):
    m, d = x.shape
    d_ff = w_down.shape[0]
    nf = d_ff // tf
    conv = jnp.concatenate(
        [conv_w.reshape(CONV_WIDTH, 2 * nf, tf).transpose(1, 0, 2),
         conv_b.reshape(2 * nf, 1, tf),
         jnp.zeros((2 * nf, SUBLANES - CONV_WIDTH - 1, tf), F32)], axis=1)
    return pl.pallas_call(
        functools.partial(_conv_ffn_kernel, tiles_per_seq=seq // tm, row_strips=FFN_ROW_STRIPS,
                          final_norm=final_norm),
        grid=(m // tm, nf),
        in_specs=[
            pl.BlockSpec((tm, d), lambda i, f: (i, 0)),
            pl.BlockSpec((1, d), lambda i, f: (0, 0)),
            pl.BlockSpec((d, tf), lambda i, f: (0, f)),
            pl.BlockSpec((d, tf), lambda i, f: (0, nf + f)),
            pl.BlockSpec((2 * nf, SUBLANES, tf), lambda i, f: (0, 0, 0)),
            pl.BlockSpec((tf, d), lambda i, f: (f, 0)),
            pl.BlockSpec((1, d), lambda i, f: (0, 0)),
        ],
        out_specs=pl.BlockSpec((tm, d), lambda i, f: (i, 0)),
        out_shape=jax.ShapeDtypeStruct((m, d), F32),
        scratch_shapes=[pltpu.VMEM((tm, d), BF16),
                        pltpu.VMEM((nf, 2, SUBLANES, tf), F32),
                        pltpu.VMEM((2, SUBLANES + tm, tf), F32)],
        compiler_params=_params("arbitrary", "arbitrary"),
        name="conv_ffn",
    )(x, gain.reshape(1, d), w_up, w_up, conv, w_down, final_gain.reshape(1, d))


MXU_WIDTH = 256

PROJ_ROWS = 512
FFN_ROWS = 1024
FFN_COLS = 512
FFN_ROW_STRIPS = 2
MOBA_HEADS_PER_STEP = 4
MOBA_GROUP = 4


def _half_tile(n):
    return n // 2 if n % (2 * MXU_WIDTH) == 0 else n


def _trunk(x, mem, mem_norm, layers, final_norm):
    batch, seq, d = x.shape
    mem_tokens = mem.shape[1]
    x = x.reshape(batch * seq, d)
    mem = mem.reshape(batch * mem_tokens, d)
    n_layers = len(layers)
    w_in = layers[0][1].astype(BF16)
    w_mem_kv = layers[0][3].astype(BF16)
    for i, (norm_mix, _, extra, _, w_o, norm_ffn, w_up, cw, cb, w_down) in enumerate(layers):
        kind = i % 3
        mem_width = w_mem_kv.shape[1] // 2
        mix = d - mem_width
        proj, (w_up, w_o) = _norm_matmul(x, norm_mix, w_in, tm=PROJ_ROWS,
                                         tn=_half_tile(w_in.shape[1]), casts=(w_up, w_o))
        kv, _ = _norm_matmul(mem, mem_norm, w_mem_kv, tm=mem_tokens, tn=w_mem_kv.shape[1])
        if kind == 0:
            tok = _retention(proj, extra, batch=batch, seq=seq, rows=seq)
        elif kind == 1:
            tok = None
        else:
            tok = _moba(proj, batch=batch, seq=seq, d=mix // MOBA_HEADS,
                        heads=MOBA_HEADS_PER_STEP, group=MOBA_GROUP)
        next_weights = (layers[i + 1][1], layers[i + 1][3]) if i + 1 < n_layers else ()
        x, (w_down, *next_weights) = _out_proj(tok, proj, extra, kv, x, w_o, seq=seq,
                                               mem_tokens=mem_tokens, tm=PROJ_ROWS,
                                               casts=(w_down,) + next_weights)
        if next_weights:
            w_in, w_mem_kv = next_weights
        x = _conv_ffn(x, norm_ffn, w_up, cw, cb, w_down, final_norm, seq=seq, tm=FFN_ROWS,
                      tf=FFN_COLS, final_norm=(i == n_layers - 1))
    return x.reshape(batch, seq, d)


def kernel(x, mem, mem_norm,
           l0_norm_mix, l0_w_in, l0_ret_gn, l0_w_mem_kv, l0_w_o, l0_norm_ffn, l0_ffn_w_up, l0_ffn_conv_w, l0_ffn_conv_b, l0_ffn_w_down,
           l1_norm_mix, l1_w_in, l1_conv_w, l1_w_mem_kv, l1_w_o, l1_norm_ffn, l1_ffn_w_up, l1_ffn_conv_w, l1_ffn_conv_b, l1_ffn_w_down,
           l2_norm_mix, l2_w_in, l2_w_mem_kv, l2_w_o, l2_norm_ffn, l2_ffn_w_up, l2_ffn_conv_w, l2_ffn_conv_b, l2_ffn_w_down,
           l3_norm_mix, l3_w_in, l3_ret_gn, l3_w_mem_kv, l3_w_o, l3_norm_ffn, l3_ffn_w_up, l3_ffn_conv_w, l3_ffn_conv_b, l3_ffn_w_down,
           final_norm):
    layers = [
        (l0_norm_mix, l0_w_in, l0_ret_gn, l0_w_mem_kv, l0_w_o, l0_norm_ffn, l0_ffn_w_up, l0_ffn_conv_w, l0_ffn_conv_b, l0_ffn_w_down),
        (l1_norm_mix, l1_w_in, l1_conv_w, l1_w_mem_kv, l1_w_o, l1_norm_ffn, l1_ffn_w_up, l1_ffn_conv_w, l1_ffn_conv_b, l1_ffn_w_down),
        (l2_norm_mix, l2_w_in, None, l2_w_mem_kv, l2_w_o, l2_norm_ffn, l2_ffn_w_up, l2_ffn_conv_w, l2_ffn_conv_b, l2_ffn_w_down),
        (l3_norm_mix, l3_w_in, l3_ret_gn, l3_w_mem_kv, l3_w_o, l3_norm_ffn, l3_ffn_w_up, l3_ffn_conv_w, l3_ffn_conv_b, l3_ffn_w_down),
    ]
    return _trunk(x, mem, mem_norm, layers, final_norm)
```

```python
import functools
import math

import jax
import jax.numpy as jnp
import numpy as np
from jax import lax
from jax.experimental import pallas as pl
from jax.experimental.pallas import tpu as pltpu

F32 = jnp.float32
BF16 = jnp.bfloat16

NORM_EPS = 1e-6
MEM_HEADS = 4
RET_HEADS = 6
RET_CHUNK = 256
CONV_WIDTH = 3
MOBA_HEADS = 12
MOBA_BLOCK = 256
MOBA_TOPK = 3

SUBLANES = 8
VMEM_LIMIT_BYTES = 56 * 1024 * 1024

_NT_DIMS = (((1,), (1,)), ((), ()))
_TN_DIMS = (((0,), (0,)), ((), ()))


def _params(*semantics):
    return pltpu.CompilerParams(dimension_semantics=semantics,
                                vmem_limit_bytes=VMEM_LIMIT_BYTES)


def _rms_norm(x, gain):
    return x * lax.rsqrt(jnp.mean(x * x, axis=-1, keepdims=True) + NORM_EPS) * gain


def _silu(g):
    return g * (1.0 / (1.0 + jnp.exp(-g)))


def _causal_conv3(u_ref, start, rows, taps):
    out = None
    for k in range(CONV_WIDTH):
        term = taps[k:k + 1] * u_ref[pl.ds(start - (CONV_WIDTH - 1 - k), rows), :]
        out = term if out is None else out + term
    return out


BF16_ROWS = 16


def _cast_specs(weights, n_steps, step_index):
    in_specs, out_specs, out_shapes = [], [], []
    for w in weights:
        rows, cols = w.shape
        assert rows % (n_steps * BF16_ROWS) == 0
        block = (rows // n_steps, cols)
        spec = pl.BlockSpec(block, lambda *idx: (step_index(*idx), 0))
        in_specs.append(spec)
        out_specs.append(spec)
        out_shapes.append(jax.ShapeDtypeStruct(w.shape, BF16))
    return in_specs, out_specs, out_shapes


def _cast_blocks(src_refs, dst_refs):
    for src, dst in zip(src_refs, dst_refs):
        dst[...] = src[...].astype(BF16)


def _norm_matmul_kernel(x_ref, g_ref, w_ref, *refs):
    n_casts = (len(refs) - 1) // 2
    o_ref = refs[n_casts]
    h = _rms_norm(x_ref[...], g_ref[...]).astype(BF16)
    o_ref[...] = jnp.dot(h, w_ref[...], preferred_element_type=F32).astype(o_ref.dtype)
    _cast_blocks(refs[:n_casts], refs[n_casts + 1:])


def _norm_matmul(x, gain, w, *, tm, tn, casts=()):
    m, d = x.shape
    n = w.shape[1]
    ni = m // tm
    cast_in, cast_out, cast_shapes = _cast_specs(casts, (n // tn) * ni, lambda j, i: j * ni + i)
    out, *casted = pl.pallas_call(
        _norm_matmul_kernel,
        grid=(n // tn, ni),
        in_specs=[
            pl.BlockSpec((tm, d), lambda j, i: (i, 0)),
            pl.BlockSpec((1, d), lambda j, i: (0, 0)),
            pl.BlockSpec((d, tn), lambda j, i: (0, j)),
        ] + cast_in,
        out_specs=[pl.BlockSpec((tm, tn), lambda j, i: (i, j))] + cast_out,
        out_shape=[jax.ShapeDtypeStruct((m, n), BF16)] + cast_shapes,
        compiler_params=_params("parallel", "parallel"),
        name="norm_matmul",
    )(x, gain.reshape(1, d), w, *casts)
    return out, casted


def _retention_kernel(lg_ref, q_ref, k_ref, v_ref, g_ref, gn_ref, o_ref, state_ref, *, chunk):
    lg = lg_ref[pl.program_id(1)]
    rows, d = q_ref.shape
    c = chunk
    k_scale = d ** -0.5

    @pl.when(pl.program_id(2) == 0)
    def _():
        state_ref[...] = jnp.zeros_like(state_ref)

    diff = (lax.broadcasted_iota(jnp.int32, (c, c), 0)
            - lax.broadcasted_iota(jnp.int32, (c, c), 1)).astype(F32)
    intra = jnp.where(diff >= 0, jnp.exp(jnp.maximum(diff, 0.0) * lg), 0.0) * k_scale
    pos = lax.broadcasted_iota(jnp.int32, (c, d), 0).astype(F32)
    q_decay = jnp.exp((pos + 1.0) * lg)
    k_decay = jnp.exp((c - 1.0 - pos) * lg) * k_scale
    chunk_decay = jnp.exp(jnp.zeros((1, d), F32) + c * lg)
    gn = gn_ref[...]

    for s in range(rows // c):
        sl = pl.ds(s * c, c)
        q = q_ref[sl, :]
        k = k_ref[sl, :]
        v = v_ref[sl, :]
        scores = lax.dot_general(q, k, _NT_DIMS, preferred_element_type=F32) * intra
        y = jnp.dot(scores.astype(BF16), v, preferred_element_type=F32)
        state = state_ref[...]
        y = y + jnp.dot(q, state.astype(BF16), preferred_element_type=F32) * q_decay
        kd = (k.astype(F32) * k_decay).astype(BF16)
        kv = lax.dot_general(kd, v, _TN_DIMS, preferred_element_type=F32)
        state_ref[...] = state * chunk_decay + kv
        g = g_ref[sl, :].astype(F32)
        o_ref[sl, :] = (_silu(g) * _rms_norm(y, gn)).astype(o_ref.dtype)


def _retention(proj, gn_gain, *, batch, seq, rows):
    m = proj.shape[0]
    d = gn_gain.shape[0] // RET_HEADS
    nt = seq // rows
    lg = np.log1p(-np.exp2(-5.0 - np.arange(RET_HEADS))).astype(np.float32)

    def spec(part):
        return pl.BlockSpec((rows, d), lambda b, h, t, lg_ref: (b * nt + t, part * RET_HEADS + h))

    return pl.pallas_call(
        functools.partial(_retention_kernel, chunk=RET_CHUNK),
        grid_spec=pltpu.PrefetchScalarGridSpec(
            num_scalar_prefetch=1,
            grid=(batch, RET_HEADS, nt),
            in_specs=[spec(0), spec(1), spec(2), spec(3),
                      pl.BlockSpec((1, d), lambda b, h, t, lg_ref: (0, h))],
            out_specs=pl.BlockSpec((rows, d), lambda b, h, t, lg_ref: (b * nt + t, h)),
            scratch_shapes=[pltpu.VMEM((d, d), F32)],
        ),
        out_shape=jax.ShapeDtypeStruct((m, RET_HEADS * d), BF16),
        compiler_params=_params("parallel", "parallel", "arbitrary"),
        name="retention",
    )(jnp.asarray(lg), proj, proj, proj, proj, gn_gain.reshape(1, -1))


def _alibi_slopes(n):
    def pow2_slopes(m):
        start = 2.0 ** (-8.0 / m)
        return [start ** (i + 1) for i in range(m)]
    if math.log2(n).is_integer():
        s = pow2_slopes(n)
    else:
        c = 2 ** int(math.floor(math.log2(n)))
        s = pow2_slopes(c) + list(_alibi_slopes(2 * c))[0::2][: n - c]
    return np.asarray(s, dtype=np.float32)


MOBA_MASKED = -1e30
AUG_PARTS = 3
AUG_PART_LANE = SUBLANES


def _split_bf16(x):
    parts = []
    for _ in range(AUG_PARTS):
        part = x.astype(BF16).astype(F32)
        parts.append(part)
        x = x - part
    return parts


def _moba_kernel(slope_ref, q_ref, k_ref, v_ref, o_ref, kmean_ref, vt_ref, kaug_ref, s_ref, *,
                 heads, group):
    qi = pl.program_id(2)
    blk = q_ref.shape[0]
    d = q_ref.shape[1] // heads
    nblk = k_ref.shape[0] // blk
    scale = d ** -0.5
    exp2_scale = scale * math.log2(math.e)
    assert AUG_PART_LANE + AUG_PARTS * nblk <= d

    def head_cols(t):
        return slice(t * d, (t + 1) * d)

    def head_slope(t):
        return slope_ref[pl.program_id(0) * heads + t]

    @pl.when((qi == 0) & (pl.program_id(1) == 0))
    def _():
        lane = lax.broadcasted_iota(jnp.int32, (blk, d), 1)
        row = lax.broadcasted_iota(jnp.int32, (blk, d), 0)
        for t in range(heads):
            key_scale = head_slope(t) * (1.0 / scale)
            for j in range(nblk):
                aug = jnp.zeros((blk, d), F32)
                for p, part in enumerate(_split_bf16((row + j * blk).astype(F32) * key_scale)):
                    aug = jnp.where(lane == p, part, aug)
                    aug = jnp.where(lane == AUG_PART_LANE + p * nblk + j, 1.0, aug)
                kaug_ref[t, j] = aug.astype(BF16)

    @pl.when(qi == 0)
    def _():
        for t in range(heads):
            for j in range(nblk):
                rows = slice(j * blk, (j + 1) * blk)
                kmean_ref[t, j:j + 1, :] = jnp.sum(k_ref[rows, head_cols(t)].astype(F32), axis=0,
                                                   keepdims=True) * (1.0 / blk)
                vt_ref[t, j] = v_ref[rows, head_cols(t)].astype(F32).T.astype(BF16)

    def query_operand(t):
        q = q_ref[:, head_cols(t)]
        kmean = kmean_ref[t]
        kmean_hi = kmean.astype(BF16)
        kmean_lo = (kmean - kmean_hi.astype(F32)).astype(BF16)
        gate = (lax.dot_general(kmean_hi, q, _NT_DIMS, preferred_element_type=F32)
                + lax.dot_general(kmean_lo, q, _NT_DIMS, preferred_element_type=F32))
        blk_id = lax.broadcasted_iota(jnp.int32, gate.shape, 0)
        rank = jnp.zeros(gate.shape, jnp.int32)
        for other in range(nblk):
            g_other = gate[other:other + 1, :]
            beats = (g_other > gate) | ((g_other == gate) & (other < blk_id))
            rank = rank + jnp.where(other < qi, jnp.where(beats, 1, 0), 0)
        keep = ((blk_id < qi) & (rank < MOBA_TOPK)) | (blk_id == qi)
        t_query = (qi * blk + lax.broadcasted_iota(jnp.int32, gate.shape, 1)).astype(F32)
        bias = (jnp.where(keep, 0.0, MOBA_MASKED) - head_slope(t) * t_query) * (1.0 / scale)
        ones = jnp.where(lax.broadcasted_iota(jnp.int32, (AUG_PART_LANE, blk), 0) < AUG_PARTS,
                         1.0, 0.0)
        pad = jnp.zeros((d - AUG_PART_LANE - AUG_PARTS * nblk, blk), F32)
        aug = jnp.concatenate([ones] + _split_bf16(bias) + [pad], axis=0)
        return jnp.concatenate([q, aug.T.astype(BF16)], axis=1)

    def scores(t, q_op, j):
        rows = pl.ds(pl.multiple_of(j * blk, blk), blk)
        k_op = jnp.concatenate([k_ref[rows, head_cols(t)], kaug_ref[t, j]], axis=1)
        return lax.dot_general(k_op, q_op, _NT_DIMS, preferred_element_type=F32)

    q_ops = [query_operand(t) for t in range(heads)]

    own_group = qi // group
    key_minus_query = (lax.broadcasted_iota(jnp.int32, (blk, blk), 0)
                       - lax.broadcasted_iota(jnp.int32, (blk, blk), 1))

    def group_scores(t, g, causal):
        best = None
        for i in range(group):
            j = g * group + i
            s = scores(t, q_ops[t], j) * exp2_scale
            if causal:
                s = jnp.where(key_minus_query <= (qi - j) * blk, s, MOBA_MASKED)
            s_ref[t, g, i] = s
            block_max = jnp.max(s, axis=0, keepdims=True)
            best = block_max if best is None else jnp.maximum(best, block_max)
        return best

    def past_scores(g, best):
        return tuple(jnp.maximum(best[t], group_scores(t, g, False)) for t in range(heads))

    best = lax.fori_loop(0, own_group, past_scores,
                         tuple(group_scores(t, own_group, True) for t in range(heads)))

    def group_values(g, carry):
        out = []
        for t in range(heads):
            l, acc = carry[t]
            ps = [jnp.exp2(s_ref[t, g, i] - best[t]) for i in range(group)]
            l = l + functools.reduce(jnp.add, [jnp.sum(p, axis=0, keepdims=True) for p in ps])
            p = jnp.concatenate([p.astype(BF16) for p in ps], axis=0)
            vt = jnp.concatenate([vt_ref[t, g * group + i] for i in range(group)], axis=1)
            out.append((l, acc + jnp.dot(vt, p, preferred_element_type=F32)))
        return tuple(out)

    zero = (jnp.zeros((1, blk), F32), jnp.zeros((d, blk), F32))
    totals = lax.fori_loop(0, own_group + 1, group_values, (zero,) * heads)
    for t in range(heads):
        l, acc = totals[t]
        o_ref[:, head_cols(t)] = (acc / l).T.astype(o_ref.dtype)


def _moba(proj, *, batch, seq, d, heads, group):
    m = proj.shape[0]
    blk = MOBA_BLOCK
    assert seq % (group * blk) == 0 and MOBA_HEADS % heads == 0
    nq = seq // blk
    groups = MOBA_HEADS // heads
    slopes = _alibi_slopes(MOBA_HEADS)
    return pl.pallas_call(
        functools.partial(_moba_kernel, heads=heads, group=group),
        grid_spec=pltpu.PrefetchScalarGridSpec(
            num_scalar_prefetch=1,
            grid=(groups, batch, nq),
            in_specs=[
                pl.BlockSpec((blk, heads * d), lambda h, b, i, s_ref: (b * nq + i, h)),
                pl.BlockSpec((seq, heads * d), lambda h, b, i, s_ref: (b, groups + h)),
                pl.BlockSpec((seq, heads * d), lambda h, b, i, s_ref: (b, 2 * groups + h)),
            ],
            out_specs=pl.BlockSpec((blk, heads * d), lambda h, b, i, s_ref: (b * nq + i, h)),
            scratch_shapes=[pltpu.VMEM((heads, nq, d), F32),
                            pltpu.VMEM((heads, nq, d, blk), BF16),
                            pltpu.VMEM((heads, nq, blk, d), BF16),
                            pltpu.VMEM((heads, nq // group, group, blk, blk), F32)],
        ),
        out_shape=jax.ShapeDtypeStruct((m, MOBA_HEADS * d), BF16),
        compiler_params=_params("parallel", "arbitrary", "arbitrary"),
        name="moba",
    )(jnp.asarray(slopes), proj, proj, proj)


def _memory_attention(qm_ref, kv_ref):
    width = qm_ref.shape[1]
    dh = width // MEM_HEADS
    outs = []
    for h in range(MEM_HEADS):
        q = qm_ref[:, h * dh:(h + 1) * dh]
        k = kv_ref[:, h * dh:(h + 1) * dh]
        v = kv_ref[:, width + h * dh:width + (h + 1) * dh]
        s = lax.dot_general(q, k, _NT_DIMS, preferred_element_type=F32) * dh ** -0.5
        p = jnp.exp(s - jnp.max(s, axis=-1, keepdims=True))
        o = jnp.dot(p.astype(BF16), v, preferred_element_type=F32)
        outs.append((o / jnp.sum(p, axis=-1, keepdims=True)).astype(BF16))
    return jnp.concatenate(outs, axis=-1)


def _project_out(tok, mem_out, x_ref, wo_ref, o_ref):
    mix = tok.shape[1]
    o_ref[...] = (x_ref[...]
                  + jnp.dot(tok, wo_ref[:mix, :], preferred_element_type=F32)
                  + jnp.dot(mem_out, wo_ref[mix:, :], preferred_element_type=F32))


def _out_proj_kernel(tok_ref, qm_ref, kv_ref, x_ref, wo_ref, *refs):
    n_casts = (len(refs) - 1) // 2
    _project_out(tok_ref[...], _memory_attention(qm_ref, kv_ref), x_ref, wo_ref, refs[n_casts])
    _cast_blocks(refs[:n_casts], refs[n_casts + 1:])


def _conv_out_proj_kernel(gb_ref, gc_ref, hh_ref, gc_prev_ref, hh_prev_ref, cw_ref,
                          qm_ref, kv_ref, x_ref, wo_ref, *refs, tiles_per_seq):
    *refs, u_ref = refs
    n_casts = (len(refs) - 1) // 2
    first = pl.program_id(0) % tiles_per_seq == 0
    tm = gc_ref.shape[0]
    prev = gc_prev_ref[...].astype(F32) * hh_prev_ref[...].astype(F32)
    u_ref[:SUBLANES, :] = jnp.where(first, 0.0, prev)
    u_ref[SUBLANES:, :] = gc_ref[...].astype(F32) * hh_ref[...].astype(F32)
    tok = (gb_ref[...].astype(F32)
           * _causal_conv3(u_ref, SUBLANES, tm, cw_ref[...])).astype(BF16)
    _project_out(tok, _memory_attention(qm_ref, kv_ref), x_ref, wo_ref, refs[n_casts])
    _cast_blocks(refs[:n_casts], refs[n_casts + 1:])


def _out_proj(tok, proj, conv_w, kv, x, wo, *, seq, mem_tokens, tm, casts=()):
    m, d = x.shape
    mem_width = kv.shape[1] // 2
    mix = d - mem_width
    tiles_per_seq = seq // tm
    qm_block = proj.shape[1] // mem_width - 1
    cast_in, cast_out, cast_shapes = _cast_specs(casts, m // tm, lambda i: i)
    tail_specs = [
        pl.BlockSpec((tm, mem_width), lambda i: (i, qm_block)),
        pl.BlockSpec((mem_tokens, 2 * mem_width), lambda i: (i // tiles_per_seq, 0)),
        pl.BlockSpec((tm, d), lambda i: (i, 0)),
        pl.BlockSpec((d, d), lambda i: (0, 0), pipeline_mode=pl.Buffered(1)),
    ] + cast_in
    if tok is not None:
        kernel = _out_proj_kernel
        head_specs = [pl.BlockSpec((tm, mix), lambda i: (i, 0))]
        head_args = (tok,)
        scratch = []
    else:
        kernel = functools.partial(_conv_out_proj_kernel, tiles_per_seq=tiles_per_seq)
        scratch = [pltpu.VMEM((SUBLANES + tm, mix), F32)]
        halo = tm // SUBLANES

        def prev_rows(part):
            return pl.BlockSpec((SUBLANES, mix), lambda i: (jnp.maximum(i * halo - 1, 0), part))

        head_specs = [pl.BlockSpec((tm, mix), lambda i: (i, 0)),
                      pl.BlockSpec((tm, mix), lambda i: (i, 1)),
                      pl.BlockSpec((tm, mix), lambda i: (i, 2)),
                      prev_rows(1), prev_rows(2),
                      pl.BlockSpec((CONV_WIDTH, mix), lambda i: (0, 0))]
        head_args = (proj, proj, proj, proj, proj, conv_w)
    out, *casted = pl.pallas_call(
        kernel,
        grid=(m // tm,),
        in_specs=head_specs + tail_specs,
        out_specs=[pl.BlockSpec((tm, d), lambda i: (i, 0))] + cast_out,
        out_shape=[jax.ShapeDtypeStruct((m, d), F32)] + cast_shapes,
        scratch_shapes=scratch,
        compiler_params=_params("parallel"),
        name="out_proj",
    )(*head_args, proj, kv, x, wo, *casts)
    return out, casted


def _conv_ffn_kernel(x_ref, g_ref, wg_ref, wu_ref, conv_ref, wd_ref,
                     fg_ref, o_ref, h_ref, carry_ref, up_ref, *,
                     tiles_per_seq, row_strips, final_norm):
    f = pl.program_id(1)
    nf = pl.num_programs(1)
    first = pl.program_id(0) % tiles_per_seq == 0
    tm = x_ref.shape[0]

    @pl.when(f == 0)
    def _():
        x = x_ref[...]
        h_ref[...] = _rms_norm(x, g_ref[...]).astype(BF16)
        o_ref[...] = x

    @pl.when(first)
    def _():
        carry_ref[f] = jnp.zeros(carry_ref.shape[1:], F32)

    strip = tm // row_strips
    for slot in range(2):
        up_ref[slot, :SUBLANES, :] = carry_ref[f, slot]
    for s in range(row_strips):
        for slot, w_ref in enumerate((wg_ref, wu_ref)):
            up_ref[slot, pl.ds(SUBLANES + s * strip, strip), :] = jnp.dot(
                h_ref[pl.ds(s * strip, strip), :], w_ref[...], preferred_element_type=F32)
    conv = [conv_ref[f + slot * nf] for slot in range(2)]
    for s in range(row_strips):
        branches = []
        for slot in range(2):
            branches.append(_causal_conv3(up_ref.at[slot], SUBLANES + s * strip, strip, conv[slot])
                            + conv[slot][CONV_WIDTH:CONV_WIDTH + 1])
        act = (_silu(branches[0]) * branches[1]).astype(BF16)
        o_ref[pl.ds(s * strip, strip), :] += jnp.dot(act, wd_ref[...], preferred_element_type=F32)
    for slot in range(2):
        carry_ref[f, slot] = up_ref[slot, pl.ds(tm, SUBLANES), :]

    if final_norm:
        @pl.when(f == pl.num_programs(1) - 1)
        def _():
            o_ref[...] = _rms_norm(o_ref[...], fg_ref[...])


def _conv_ffn(x, gain, w_up, conv_w, conv_b, w_down, final_gain, *, seq, tm, tf, final_norm):
    m, d = x.shape
    d_ff = w_down.shape[0]
    nf = d_ff // tf
    conv = jnp.concatenate(
        [conv_w.reshape(CONV_WIDTH, 2 * nf, tf).transpose(1, 0, 2),
         conv_b.reshape(2 * nf, 1, tf),
         jnp.zeros((2 * nf, SUBLANES - CONV_WIDTH - 1, tf), F32)], axis=1)
    return pl.pallas_call(
        functools.partial(_conv_ffn_kernel, tiles_per_seq=seq // tm, row_strips=FFN_ROW_STRIPS,
                          final_norm=final_norm),
        grid=(m // tm, nf),
        in_specs=[
            pl.BlockSpec((tm, d), lambda i, f: (i, 0)),
            pl.BlockSpec((1, d), lambda i, f: (0, 0)),
            pl.BlockSpec((d, tf), lambda i, f: (0, f)),
            pl.BlockSpec((d, tf), lambda i, f: (0, nf + f)),
            pl.BlockSpec((2 * nf, SUBLANES, tf), lambda i, f: (0, 0, 0)),
            pl.BlockSpec((tf, d), lambda i, f: (f, 0)),
            pl.BlockSpec((1, d), lambda i, f: (0, 0)),
        ],
        out_specs=pl.BlockSpec((tm, d), lambda i, f: (i, 0)),
        out_shape=jax.ShapeDtypeStruct((m, d), F32),
        scratch_shapes=[pltpu.VMEM((tm, d), BF16),
                        pltpu.VMEM((nf, 2, SUBLANES, tf), F32),
                        pltpu.VMEM((2, SUBLANES + tm, tf), F32)],
        compiler_params=_params("arbitrary", "arbitrary"),
        name="conv_ffn",
    )(x, gain.reshape(1, d), w_up, w_up, conv, w_down, final_gain.reshape(1, d))


MXU_WIDTH = 256

PROJ_ROWS = 512
FFN_ROWS = 1024
FFN_COLS = 512
FFN_ROW_STRIPS = 2
MOBA_HEADS_PER_STEP = 4
MOBA_GROUP = 4


def _half_tile(n):
    return n // 2 if n % (2 * MXU_WIDTH) == 0 else n


def _trunk(x, mem, mem_norm, layers, final_norm):
    batch, seq, d = x.shape
    mem_tokens = mem.shape[1]
    x = x.reshape(batch * seq, d)
    mem = mem.reshape(batch * mem_tokens, d)
    n_layers = len(layers)
    w_in = layers[0][1].astype(BF16)
    w_mem_kv = layers[0][3].astype(BF16)
    for i, (norm_mix, _, extra, _, w_o, norm_ffn, w_up, cw, cb, w_down) in enumerate(layers):
        kind = i % 3
        mem_width = w_mem_kv.shape[1] // 2
        mix = d - mem_width
        proj, (w_up, w_o) = _norm_matmul(x, norm_mix, w_in, tm=PROJ_ROWS,
                                         tn=_half_tile(w_in.shape[1]), casts=(w_up, w_o))
        kv, _ = _norm_matmul(mem, mem_norm, w_mem_kv, tm=mem_tokens, tn=w_mem_kv.shape[1])
        if kind == 0:
            tok = _retention(proj, extra, batch=batch, seq=seq, rows=seq)
        elif kind == 1:
            tok = None
        else:
            tok = _moba(proj, batch=batch, seq=seq, d=mix // MOBA_HEADS,
                        heads=MOBA_HEADS_PER_STEP, group=MOBA_GROUP)
        next_weights = (layers[i + 1][1], layers[i + 1][3]) if i + 1 < n_layers else ()
        x, (w_down, *next_weights) = _out_proj(tok, proj, extra, kv, x, w_o, seq=seq,
                                               mem_tokens=mem_tokens, tm=PROJ_ROWS,
                                               casts=(w_down,) + next_weights)
        if next_weights:
            w_in, w_mem_kv = next_weights
        x = _conv_ffn(x, norm_ffn, w_up, cw, cb, w_down, final_norm, seq=seq, tm=FFN_ROWS,
                      tf=FFN_COLS, final_norm=(i == n_layers - 1))
    return x.reshape(batch, seq, d)


def kernel(x, mem, mem_norm,
           l0_norm_mix, l0_w_in, l0_ret_gn, l0_w_mem_kv, l0_w_o, l0_norm_ffn, l0_ffn_w_up, l0_ffn_conv_w, l0_ffn_conv_b, l0_ffn_w_down,
           l1_norm_mix, l1_w_in, l1_conv_w, l1_w_mem_kv, l1_w_o, l1_norm_ffn, l1_ffn_w_up, l1_ffn_conv_w, l1_ffn_conv_b, l1_ffn_w_down,
           l2_norm_mix, l2_w_in, l2_w_mem_kv, l2_w_o, l2_norm_ffn, l2_ffn_w_up, l2_ffn_conv_w, l2_ffn_conv_b, l2_ffn_w_down,
           l3_norm_mix, l3_w_in, l3_ret_gn, l3_w_mem_kv, l3_w_o, l3_norm_ffn, l3_ffn_w_up, l3_ffn_conv_w, l3_ffn_conv_b, l3_ffn_w_down,
           final_norm):
    layers = [
        (l0_norm_mix, l0_w_in, l0_ret_gn, l0_w_mem_kv, l0_w_o, l0_norm_ffn, l0_ffn_w_up, l0_ffn_conv_w, l0_ffn_conv_b, l0_ffn_w_down),
        (l1_norm_mix, l1_w_in, l1_conv_w, l1_w_mem_kv, l1_w_o, l1_norm_ffn, l1_ffn_w_up, l1_ffn_conv_w, l1_ffn_conv_b, l1_ffn_w_down),
        (l2_norm_mix, l2_w_in, None, l2_w_mem_kv, l2_w_o, l2_norm_ffn, l2_ffn_w_up, l2_ffn_conv_w, l2_ffn_conv_b, l2_ffn_w_down),
        (l3_norm_mix, l3_w_in, l3_ret_gn, l3_w_mem_kv, l3_w_o, l3_norm_ffn, l3_ffn_w_up, l3_ffn_conv_w, l3_ffn_conv_b, l3_ffn_w_down),
    ]
    return _trunk(x, mem, mem_norm, layers, final_norm)
```

```python
import functools
import math

import jax
import jax.numpy as jnp
import numpy as np
from jax import lax
from jax.experimental import pallas as pl
from jax.experimental.pallas import tpu as pltpu

F32 = jnp.float32
BF16 = jnp.bfloat16

NORM_EPS = 1e-6
MEM_HEADS = 4
RET_HEADS = 6
RET_CHUNK = 256
CONV_WIDTH = 3
MOBA_HEADS = 12
MOBA_BLOCK = 256
MOBA_TOPK = 3

SUBLANES = 8
VMEM_LIMIT_BYTES = 56 * 1024 * 1024

_NT_DIMS = (((1,), (1,)), ((), ()))
_TN_DIMS = (((0,), (0,)), ((), ()))


def _params(*semantics):
    return pltpu.CompilerParams(dimension_semantics=semantics,
                                vmem_limit_bytes=VMEM_LIMIT_BYTES)


def _rms_norm(x, gain):
    return x * lax.rsqrt(jnp.mean(x * x, axis=-1, keepdims=True) + NORM_EPS) * gain


def _silu(g):
    return g * (1.0 / (1.0 + jnp.exp(-g)))


def _shift_rows(u, prev, k):
    rolled = pltpu.roll(u, k, axis=0)
    prev_rolled = pltpu.roll(prev, k, axis=0)
    row = lax.broadcasted_iota(jnp.int32, prev.shape, 0)
    head = jnp.where(row < k, prev_rolled, rolled[:SUBLANES])
    return jnp.concatenate([head, rolled[SUBLANES:]], axis=0)


def _causal_conv3(u, prev, w):
    return (w[0:1] * _shift_rows(u, prev, 2) + w[1:2] * _shift_rows(u, prev, 1)
            + w[2:3] * u)


BF16_ROWS = 16


def _cast_specs(weights, n_steps, step_index):
    in_specs, out_specs, out_shapes = [], [], []
    for w in weights:
        rows, cols = w.shape
        assert rows % (n_steps * BF16_ROWS) == 0
        block = (rows // n_steps, cols)
        spec = pl.BlockSpec(block, lambda *idx: (step_index(*idx), 0))
        in_specs.append(spec)
        out_specs.append(spec)
        out_shapes.append(jax.ShapeDtypeStruct(w.shape, BF16))
    return in_specs, out_specs, out_shapes


def _cast_blocks(src_refs, dst_refs):
    for src, dst in zip(src_refs, dst_refs):
        dst[...] = src[...].astype(BF16)


def _norm_matmul_kernel(x_ref, g_ref, w_ref, *refs):
    n_casts = (len(refs) - 1) // 2
    o_ref = refs[n_casts]
    h = _rms_norm(x_ref[...], g_ref[...]).astype(BF16)
    o_ref[...] = jnp.dot(h, w_ref[...], preferred_element_type=F32).astype(o_ref.dtype)
    _cast_blocks(refs[:n_casts], refs[n_casts + 1:])


def _norm_matmul(x, gain, w, *, tm, tn, casts=()):
    m, d = x.shape
    n = w.shape[1]
    ni = m // tm
    cast_in, cast_out, cast_shapes = _cast_specs(casts, (n // tn) * ni, lambda j, i: j * ni + i)
    out, *casted = pl.pallas_call(
        _norm_matmul_kernel,
        grid=(n // tn, ni),
        in_specs=[
            pl.BlockSpec((tm, d), lambda j, i: (i, 0)),
            pl.BlockSpec((1, d), lambda j, i: (0, 0)),
            pl.BlockSpec((d, tn), lambda j, i: (0, j)),
        ] + cast_in,
        out_specs=[pl.BlockSpec((tm, tn), lambda j, i: (i, j))] + cast_out,
        out_shape=[jax.ShapeDtypeStruct((m, n), BF16)] + cast_shapes,
        compiler_params=_params("parallel", "parallel"),
        name="norm_matmul",
    )(x, gain.reshape(1, d), w, *casts)
    return out, casted


def _retention_kernel(lg_ref, q_ref, k_ref, v_ref, g_ref, gn_ref, o_ref, state_ref, *, chunk):
    lg = lg_ref[pl.program_id(1)]
    rows, d = q_ref.shape
    c = chunk
    k_scale = d ** -0.5

    @pl.when(pl.program_id(2) == 0)
    def _():
        state_ref[...] = jnp.zeros_like(state_ref)

    diff = (lax.broadcasted_iota(jnp.int32, (c, c), 0)
            - lax.broadcasted_iota(jnp.int32, (c, c), 1)).astype(F32)
    intra = jnp.where(diff >= 0, jnp.exp(jnp.maximum(diff, 0.0) * lg), 0.0) * k_scale
    pos = lax.broadcasted_iota(jnp.int32, (c, d), 0).astype(F32)
    q_decay = jnp.exp((pos + 1.0) * lg)
    k_decay = jnp.exp((c - 1.0 - pos) * lg) * k_scale
    chunk_decay = jnp.exp(jnp.zeros((1, d), F32) + c * lg)
    gn = gn_ref[...]

    for s in range(rows // c):
        sl = pl.ds(s * c, c)
        q = q_ref[sl, :]
        k = k_ref[sl, :]
        v = v_ref[sl, :]
        scores = lax.dot_general(q, k, _NT_DIMS, preferred_element_type=F32) * intra
        y = jnp.dot(scores.astype(BF16), v, preferred_element_type=F32)
        state = state_ref[...]
        y = y + jnp.dot(q, state.astype(BF16), preferred_element_type=F32) * q_decay
        kd = (k.astype(F32) * k_decay).astype(BF16)
        kv = lax.dot_general(kd, v, _TN_DIMS, preferred_element_type=F32)
        state_ref[...] = state * chunk_decay + kv
        g = g_ref[sl, :].astype(F32)
        o_ref[sl, :] = (_silu(g) * _rms_norm(y, gn)).astype(o_ref.dtype)


def _retention(proj, gn_gain, *, batch, seq, rows):
    m = proj.shape[0]
    d = gn_gain.shape[0] // RET_HEADS
    nt = seq // rows
    lg = np.log1p(-np.exp2(-5.0 - np.arange(RET_HEADS))).astype(np.float32)

    def spec(part):
        return pl.BlockSpec((rows, d), lambda b, h, t, lg_ref: (b * nt + t, part * RET_HEADS + h))

    return pl.pallas_call(
        functools.partial(_retention_kernel, chunk=RET_CHUNK),
        grid_spec=pltpu.PrefetchScalarGridSpec(
            num_scalar_prefetch=1,
            grid=(batch, RET_HEADS, nt),
            in_specs=[spec(0), spec(1), spec(2), spec(3),
                      pl.BlockSpec((1, d), lambda b, h, t, lg_ref: (0, h))],
            out_specs=pl.BlockSpec((rows, d), lambda b, h, t, lg_ref: (b * nt + t, h)),
            scratch_shapes=[pltpu.VMEM((d, d), F32)],
        ),
        out_shape=jax.ShapeDtypeStruct((m, RET_HEADS * d), BF16),
        compiler_params=_params("parallel", "parallel", "arbitrary"),
        name="retention",
    )(jnp.asarray(lg), proj, proj, proj, proj, gn_gain.reshape(1, -1))


def _alibi_slopes(n):
    def pow2_slopes(m):
        start = 2.0 ** (-8.0 / m)
        return [start ** (i + 1) for i in range(m)]
    if math.log2(n).is_integer():
        s = pow2_slopes(n)
    else:
        c = 2 ** int(math.floor(math.log2(n)))
        s = pow2_slopes(c) + list(_alibi_slopes(2 * c))[0::2][: n - c]
    return np.asarray(s, dtype=np.float32)


MOBA_MASKED = -1e30
AUG_PARTS = 3
AUG_PART_LANE = SUBLANES


def _split_bf16(x):
    parts = []
    for _ in range(AUG_PARTS):
        part = x.astype(BF16).astype(F32)
        parts.append(part)
        x = x - part
    return parts


def _moba_kernel(slope_ref, q_ref, k_ref, v_ref, o_ref, kmean_ref, vt_ref, kaug_ref, s_ref, *,
                 heads, group):
    qi = pl.program_id(2)
    blk = q_ref.shape[0]
    d = q_ref.shape[1] // heads
    nblk = k_ref.shape[0] // blk
    scale = d ** -0.5
    exp2_scale = scale * math.log2(math.e)
    assert AUG_PART_LANE + AUG_PARTS * nblk <= d

    def head_cols(t):
        return slice(t * d, (t + 1) * d)

    def head_slope(t):
        return slope_ref[pl.program_id(0) * heads + t]

    @pl.when((qi == 0) & (pl.program_id(1) == 0))
    def _():
        lane = lax.broadcasted_iota(jnp.int32, (blk, d), 1)
        row = lax.broadcasted_iota(jnp.int32, (blk, d), 0)
        for t in range(heads):
            key_scale = head_slope(t) * (1.0 / scale)
            for j in range(nblk):
                aug = jnp.zeros((blk, d), F32)
                for p, part in enumerate(_split_bf16((row + j * blk).astype(F32) * key_scale)):
                    aug = jnp.where(lane == p, part, aug)
                    aug = jnp.where(lane == AUG_PART_LANE + p * nblk + j, 1.0, aug)
                kaug_ref[t, j * blk:(j + 1) * blk, :] = aug.astype(BF16)

    @pl.when(qi == 0)
    def _():
        ones_row = jnp.where(lax.broadcasted_iota(jnp.int32, (BF16_ROWS, blk), 0) == 0, 1.0, 0.0)
        for t in range(heads):
            for j in range(nblk):
                rows = slice(j * blk, (j + 1) * blk)
                kmean_ref[t, j:j + 1, :] = jnp.sum(k_ref[rows, head_cols(t)].astype(F32), axis=0,
                                                   keepdims=True) * (1.0 / blk)
                vt_ref[t, j] = jnp.concatenate(
                    [v_ref[rows, head_cols(t)].astype(F32).T, ones_row], axis=0).astype(BF16)

    def query_operand(t):
        q = q_ref[:, head_cols(t)]
        kmean = kmean_ref[t]
        kmean_hi = kmean.astype(BF16)
        kmean_lo = (kmean - kmean_hi.astype(F32)).astype(BF16)
        gate = (lax.dot_general(kmean_hi, q, _NT_DIMS, preferred_element_type=F32)
                + lax.dot_general(kmean_lo, q, _NT_DIMS, preferred_element_type=F32))
        blk_id = lax.broadcasted_iota(jnp.int32, gate.shape, 0)
        rank = jnp.zeros(gate.shape, jnp.int32)
        for other in range(nblk):
            g_other = gate[other:other + 1, :]
            beats = (g_other > gate) | ((g_other == gate) & (other < blk_id))
            rank = rank + jnp.where(other < qi, jnp.where(beats, 1, 0), 0)
        keep = ((blk_id < qi) & (rank < MOBA_TOPK)) | (blk_id == qi)
        t_query = (qi * blk + lax.broadcasted_iota(jnp.int32, gate.shape, 1)).astype(F32)
        bias = (jnp.where(keep, 0.0, MOBA_MASKED) - head_slope(t) * t_query) * (1.0 / scale)
        ones = jnp.where(lax.broadcasted_iota(jnp.int32, (AUG_PART_LANE, blk), 0) < AUG_PARTS,
                         1.0, 0.0)
        pad = jnp.zeros((d - AUG_PART_LANE - AUG_PARTS * nblk, blk), F32)
        aug = jnp.concatenate([ones] + _split_bf16(bias) + [pad], axis=0)
        return jnp.concatenate([q, aug.T.astype(BF16)], axis=1)

    def scores(t, q_op, g):
        rows = pl.ds(pl.multiple_of(g * group * blk, group * blk), group * blk)
        k_op = jnp.concatenate([k_ref[rows, head_cols(t)], kaug_ref[t, rows, :]], axis=1)
        return lax.dot_general(k_op, q_op, _NT_DIMS, preferred_element_type=F32)

    q_ops = [query_operand(t) for t in range(heads)]

    own_group = qi // group
    key_minus_query = (lax.broadcasted_iota(jnp.int32, (blk, blk), 0)
                       - lax.broadcasted_iota(jnp.int32, (blk, blk), 1))

    def group_scores(t, g, causal):
        best = None
        group_s = scores(t, q_ops[t], g)
        for i in range(group):
            j = g * group + i
            s = group_s[i * blk:(i + 1) * blk] * exp2_scale
            if causal:
                s = jnp.where(key_minus_query <= (qi - j) * blk, s, MOBA_MASKED)
            s_ref[t, g, i] = s
            block_max = jnp.max(s, axis=0, keepdims=True)
            best = block_max if best is None else jnp.maximum(best, block_max)
        return best

    def past_scores(g, best):
        return tuple(jnp.maximum(best[t], group_scores(t, g, False)) for t in range(heads))

    best = lax.fori_loop(0, own_group, past_scores,
                         tuple(group_scores(t, own_group, True) for t in range(heads)))

    def group_values(g, accs):
        out = []
        for t in range(heads):
            p = jnp.concatenate([jnp.exp2(s_ref[t, g, i] - best[t]).astype(BF16)
                                 for i in range(group)], axis=0)
            vt = jnp.concatenate([vt_ref[t, g * group + i] for i in range(group)], axis=1)
            out.append(accs[t] + jnp.dot(vt, p, preferred_element_type=F32))
        return tuple(out)

    zero = jnp.zeros((d + BF16_ROWS, blk), F32)
    totals = lax.fori_loop(0, own_group + 1, group_values, (zero,) * heads)
    for t in range(heads):
        acc = totals[t]
        o_ref[:, head_cols(t)] = (acc[:d] / acc[d:d + 1]).T.astype(o_ref.dtype)


def _moba(proj, *, batch, seq, d, heads, group):
    m = proj.shape[0]
    blk = MOBA_BLOCK
    assert seq % (group * blk) == 0 and MOBA_HEADS % heads == 0
    nq = seq // blk
    groups = MOBA_HEADS // heads
    slopes = _alibi_slopes(MOBA_HEADS)
    return pl.pallas_call(
        functools.partial(_moba_kernel, heads=heads, group=group),
        grid_spec=pltpu.PrefetchScalarGridSpec(
            num_scalar_prefetch=1,
            grid=(groups, batch, nq),
            in_specs=[
                pl.BlockSpec((blk, heads * d), lambda h, b, i, s_ref: (b * nq + i, h)),
                pl.BlockSpec((seq, heads * d), lambda h, b, i, s_ref: (b, groups + h)),
                pl.BlockSpec((seq, heads * d), lambda h, b, i, s_ref: (b, 2 * groups + h)),
            ],
            out_specs=pl.BlockSpec((blk, heads * d), lambda h, b, i, s_ref: (b * nq + i, h)),
            scratch_shapes=[pltpu.VMEM((heads, nq, d), F32),
                            pltpu.VMEM((heads, nq, d + BF16_ROWS, blk), BF16),
                            pltpu.VMEM((heads, seq, d), BF16),
                            pltpu.VMEM((heads, nq // group, group, blk, blk), F32)],
        ),
        out_shape=jax.ShapeDtypeStruct((m, MOBA_HEADS * d), BF16),
        compiler_params=_params("parallel", "arbitrary", "arbitrary"),
        name="moba",
    )(jnp.asarray(slopes), proj, proj, proj)


def _memory_attention(qm_ref, kv_ref):
    width = qm_ref.shape[1]
    dh = width // MEM_HEADS
    outs = []
    for h in range(MEM_HEADS):
        q = qm_ref[:, h * dh:(h + 1) * dh]
        k = kv_ref[:, h * dh:(h + 1) * dh]
        v = kv_ref[:, width + h * dh:width + (h + 1) * dh]
        s = lax.dot_general(q, k, _NT_DIMS, preferred_element_type=F32) * dh ** -0.5
        p = jnp.exp(s - jnp.max(s, axis=-1, keepdims=True))
        o = jnp.dot(p.astype(BF16), v, preferred_element_type=F32)
        outs.append((o / jnp.sum(p, axis=-1, keepdims=True)).astype(BF16))
    return jnp.concatenate(outs, axis=-1)


def _project_out(tok, mem_out, x_ref, wo_ref, o_ref):
    mix = tok.shape[1]
    o_ref[...] = (x_ref[...]
                  + jnp.dot(tok, wo_ref[:mix, :], preferred_element_type=F32)
                  + jnp.dot(mem_out, wo_ref[mix:, :], preferred_element_type=F32))


def _out_proj_kernel(tok_ref, qm_ref, kv_ref, x_ref, wo_ref, *refs):
    n_casts = (len(refs) - 1) // 2
    _project_out(tok_ref[...], _memory_attention(qm_ref, kv_ref), x_ref, wo_ref, refs[n_casts])
    _cast_blocks(refs[:n_casts], refs[n_casts + 1:])


def _conv_out_proj_kernel(gb_ref, gc_ref, hh_ref, gc_prev_ref, hh_prev_ref, cw_ref,
                          qm_ref, kv_ref, x_ref, wo_ref, *refs, tiles_per_seq):
    n_casts = (len(refs) - 1) // 2
    first = pl.program_id(0) % tiles_per_seq == 0
    u = gc_ref[...].astype(F32) * hh_ref[...].astype(F32)
    prev = gc_prev_ref[...].astype(F32) * hh_prev_ref[...].astype(F32)
    prev = jnp.where(first, 0.0, prev)
    tok = (gb_ref[...].astype(F32) * _causal_conv3(u, prev, cw_ref[...])).astype(BF16)
    _project_out(tok, _memory_attention(qm_ref, kv_ref), x_ref, wo_ref, refs[n_casts])
    _cast_blocks(refs[:n_casts], refs[n_casts + 1:])


def _out_proj(tok, proj, conv_w, kv, x, wo, *, seq, mem_tokens, tm, casts=()):
    m, d = x.shape
    mem_width = kv.shape[1] // 2
    mix = d - mem_width
    tiles_per_seq = seq // tm
    qm_block = proj.shape[1] // mem_width - 1
    cast_in, cast_out, cast_shapes = _cast_specs(casts, m // tm, lambda i: i)
    tail_specs = [
        pl.BlockSpec((tm, mem_width), lambda i: (i, qm_block)),
        pl.BlockSpec((mem_tokens, 2 * mem_width), lambda i: (i // tiles_per_seq, 0)),
        pl.BlockSpec((tm, d), lambda i: (i, 0)),
        pl.BlockSpec((d, d), lambda i: (0, 0), pipeline_mode=pl.Buffered(1)),
    ] + cast_in
    if tok is not None:
        kernel = _out_proj_kernel
        head_specs = [pl.BlockSpec((tm, mix), lambda i: (i, 0))]
        head_args = (tok,)
    else:
        kernel = functools.partial(_conv_out_proj_kernel, tiles_per_seq=tiles_per_seq)
        halo = tm // SUBLANES

        def prev_rows(part):
            return pl.BlockSpec((SUBLANES, mix), lambda i: (jnp.maximum(i * halo - 1, 0), part))

        head_specs = [pl.BlockSpec((tm, mix), lambda i: (i, 0)),
                      pl.BlockSpec((tm, mix), lambda i: (i, 1)),
                      pl.BlockSpec((tm, mix), lambda i: (i, 2)),
                      prev_rows(1), prev_rows(2),
                      pl.BlockSpec((CONV_WIDTH, mix), lambda i: (0, 0))]
        head_args = (proj, proj, proj, proj, proj, conv_w)
    out, *casted = pl.pallas_call(
        kernel,
        grid=(m // tm,),
        in_specs=head_specs + tail_specs,
        out_specs=[pl.BlockSpec((tm, d), lambda i: (i, 0))] + cast_out,
        out_shape=[jax.ShapeDtypeStruct((m, d), F32)] + cast_shapes,
        compiler_params=_params("parallel"),
        name="out_proj",
    )(*head_args, proj, kv, x, wo, *casts)
    return out, casted


def _conv_ffn_kernel(x_ref, g_ref, wg_ref, wu_ref, conv_ref, wd_ref,
                     fg_ref, o_ref, h_ref, carry_ref, up_ref, *,
                     tiles_per_seq, row_strips, final_norm):
    f = pl.program_id(1)
    nf = pl.num_programs(1)
    first = pl.program_id(0) % tiles_per_seq == 0
    tm = x_ref.shape[0]

    @pl.when(f == 0)
    def _():
        x = x_ref[...]
        h_ref[...] = _rms_norm(x, g_ref[...]).astype(BF16)
        o_ref[...] = x

    @pl.when(first)
    def _():
        carry_ref[f] = jnp.zeros(carry_ref.shape[1:], F32)

    strip = tm // row_strips
    for slot in range(2):
        up_ref[slot, :SUBLANES, :] = carry_ref[f, slot]
    for s in range(row_strips):
        for slot, w_ref in enumerate((wg_ref, wu_ref)):
            up_ref[slot, pl.ds(SUBLANES + s * strip, strip), :] = jnp.dot(
                h_ref[pl.ds(s * strip, strip), :], w_ref[...], preferred_element_type=F32)
    conv = [conv_ref[f + slot * nf] for slot in range(2)]
    for s in range(row_strips):
        branches = []
        for slot in range(2):
            taps = [up_ref[slot, pl.ds(SUBLANES + s * strip - (CONV_WIDTH - 1 - k), strip), :]
                    for k in range(CONV_WIDTH)]
            branches.append(conv[slot][0:1] * taps[0] + conv[slot][1:2] * taps[1]
                            + conv[slot][2:3] * taps[2] + conv[slot][CONV_WIDTH:CONV_WIDTH + 1])
        act = (_silu(branches[0]) * branches[1]).astype(BF16)
        o_ref[pl.ds(s * strip, strip), :] += jnp.dot(act, wd_ref[...], preferred_element_type=F32)
    for slot in range(2):
        carry_ref[f, slot] = up_ref[slot, pl.ds(tm, SUBLANES), :]

    if final_norm:
        @pl.when(f == pl.num_programs(1) - 1)
        def _():
            o_ref[...] = _rms_norm(o_ref[...], fg_ref[...])


def _conv_ffn(x, gain, w_up, conv_w, conv_b, w_down, final_gain, *, seq, tm, tf, final_norm):
    m, d = x.shape
    d_ff = w_down.shape[0]
    nf = d_ff // tf
    conv = jnp.concatenate(
        [conv_w.reshape(CONV_WIDTH, 2 * nf, tf).transpose(1, 0, 2),
         conv_b.reshape(2 * nf, 1, tf),
         jnp.zeros((2 * nf, SUBLANES - CONV_WIDTH - 1, tf), F32)], axis=1)
    return pl.pallas_call(
        functools.partial(_conv_ffn_kernel, tiles_per_seq=seq // tm, row_strips=FFN_ROW_STRIPS,
                          final_norm=final_norm),
        grid=(m // tm, nf),
        in_specs=[
            pl.BlockSpec((tm, d), lambda i, f: (i, 0)),
            pl.BlockSpec((1, d), lambda i, f: (0, 0)),
            pl.BlockSpec((d, tf), lambda i, f: (0, f)),
            pl.BlockSpec((d, tf), lambda i, f: (0, nf + f)),
            pl.BlockSpec((2 * nf, SUBLANES, tf), lambda i, f: (0, 0, 0)),
            pl.BlockSpec((tf, d), lambda i, f: (f, 0)),
            pl.BlockSpec((1, d), lambda i, f: (0, 0)),
        ],
        out_specs=pl.BlockSpec((tm, d), lambda i, f: (i, 0)),
        out_shape=jax.ShapeDtypeStruct((m, d), F32),
        scratch_shapes=[pltpu.VMEM((tm, d), BF16),
                        pltpu.VMEM((nf, 2, SUBLANES, tf), F32),
                        pltpu.VMEM((2, SUBLANES + tm, tf), F32)],
        compiler_params=_params("arbitrary", "arbitrary"),
        name="conv_ffn",
    )(x, gain.reshape(1, d), w_up, w_up, conv, w_down, final_gain.reshape(1, d))


MXU_WIDTH = 256

PROJ_ROWS = 512
FFN_ROWS = 1024
FFN_COLS = 512
FFN_ROW_STRIPS = 2
MOBA_HEADS_PER_STEP = 4
MOBA_GROUP = 4


def _half_tile(n):
    return n // 2 if n % (2 * MXU_WIDTH) == 0 else n


def _trunk(x, mem, mem_norm, layers, final_norm):
    batch, seq, d = x.shape
    mem_tokens = mem.shape[1]
    x = x.reshape(batch * seq, d)
    mem = mem.reshape(batch * mem_tokens, d)
    n_layers = len(layers)
    w_in = layers[0][1].astype(BF16)
    w_mem_kv = layers[0][3].astype(BF16)
    for i, (norm_mix, _, extra, _, w_o, norm_ffn, w_up, cw, cb, w_down) in enumerate(layers):
        kind = i % 3
        mem_width = w_mem_kv.shape[1] // 2
        mix = d - mem_width
        proj, (w_up, w_o) = _norm_matmul(x, norm_mix, w_in, tm=PROJ_ROWS,
                                         tn=_half_tile(w_in.shape[1]), casts=(w_up, w_o))
        kv, _ = _norm_matmul(mem, mem_norm, w_mem_kv, tm=mem_tokens, tn=w_mem_kv.shape[1])
        if kind == 0:
            tok = _retention(proj, extra, batch=batch, seq=seq, rows=seq)
        elif kind == 1:
            tok = None
        else:
            tok = _moba(proj, batch=batch, seq=seq, d=mix // MOBA_HEADS,
                        heads=MOBA_HEADS_PER_STEP, group=MOBA_GROUP)
        next_weights = (layers[i + 1][1], layers[i + 1][3]) if i + 1 < n_layers else ()
        x, (w_down, *next_weights) = _out_proj(tok, proj, extra, kv, x, w_o, seq=seq,
                                               mem_tokens=mem_tokens, tm=PROJ_ROWS,
                                               casts=(w_down,) + next_weights)
        if next_weights:
            w_in, w_mem_kv = next_weights
        x = _conv_ffn(x, norm_ffn, w_up, cw, cb, w_down, final_norm, seq=seq, tm=FFN_ROWS,
                      tf=FFN_COLS, final_norm=(i == n_layers - 1))
    return x.reshape(batch, seq, d)


def kernel(x, mem, mem_norm,
           l0_norm_mix, l0_w_in, l0_ret_gn, l0_w_mem_kv, l0_w_o, l0_norm_ffn, l0_ffn_w_up, l0_ffn_conv_w, l0_ffn_conv_b, l0_ffn_w_down,
           l1_norm_mix, l1_w_in, l1_conv_w, l1_w_mem_kv, l1_w_o, l1_norm_ffn, l1_ffn_w_up, l1_ffn_conv_w, l1_ffn_conv_b, l1_ffn_w_down,
           l2_norm_mix, l2_w_in, l2_w_mem_kv, l2_w_o, l2_norm_ffn, l2_ffn_w_up, l2_ffn_conv_w, l2_ffn_conv_b, l2_ffn_w_down,
           l3_norm_mix, l3_w_in, l3_ret_gn, l3_w_mem_kv, l3_w_o, l3_norm_ffn, l3_ffn_w_up, l3_ffn_conv_w, l3_ffn_conv_b, l3_ffn_w_down,
           final_norm):
    layers = [
        (l0_norm_mix, l0_w_in, l0_ret_gn, l0_w_mem_kv, l0_w_o, l0_norm_ffn, l0_ffn_w_up, l0_ffn_conv_w, l0_ffn_conv_b, l0_ffn_w_down),
        (l1_norm_mix, l1_w_in, l1_conv_w, l1_w_mem_kv, l1_w_o, l1_norm_ffn, l1_ffn_w_up, l1_ffn_conv_w, l1_ffn_conv_b, l1_ffn_w_down),
        (l2_norm_mix, l2_w_in, None, l2_w_mem_kv, l2_w_o, l2_norm_ffn, l2_ffn_w_up, l2_ffn_conv_w, l2_ffn_conv_b, l2_ffn_w_down),
        (l3_norm_mix, l3_w_in, l3_ret_gn, l3_w_mem_kv, l3_w_o, l3_norm_ffn, l3_ffn_w_up, l3_ffn_conv_w, l3_ffn_conv_b, l3_ffn_w_down),
    ]
    return _trunk(x, mem, mem_norm, layers, final_norm)
```

```python
import functools
import math

import jax
import jax.numpy as jnp
import numpy as np
from jax import lax
from jax.experimental import pallas as pl
from jax.experimental.pallas import tpu as pltpu

F32 = jnp.float32
BF16 = jnp.bfloat16

NORM_EPS = 1e-6
MEM_HEADS = 4
RET_HEADS = 6
RET_CHUNK = 256
CONV_WIDTH = 3
MOBA_HEADS = 12
MOBA_BLOCK = 256
MOBA_TOPK = 3

SUBLANES = 8
VMEM_LIMIT_BYTES = 56 * 1024 * 1024

_NT_DIMS = (((1,), (1,)), ((), ()))
_TN_DIMS = (((0,), (0,)), ((), ()))


def _params(*semantics):
    return pltpu.CompilerParams(dimension_semantics=semantics,
                                vmem_limit_bytes=VMEM_LIMIT_BYTES)


def _rms_norm(x, gain):
    return x * lax.rsqrt(jnp.mean(x * x, axis=-1, keepdims=True) + NORM_EPS) * gain


def _silu(g):
    return g * (1.0 / (1.0 + jnp.exp(-g)))


def _shift_rows(u, prev, k):
    rolled = pltpu.roll(u, k, axis=0)
    prev_rolled = pltpu.roll(prev, k, axis=0)
    row = lax.broadcasted_iota(jnp.int32, prev.shape, 0)
    head = jnp.where(row < k, prev_rolled, rolled[:SUBLANES])
    return jnp.concatenate([head, rolled[SUBLANES:]], axis=0)


def _causal_conv3(u, prev, w):
    return (w[0:1] * _shift_rows(u, prev, 2) + w[1:2] * _shift_rows(u, prev, 1)
            + w[2:3] * u)


BF16_ROWS = 16


def _cast_specs(weights, n_steps, step_index):
    in_specs, out_specs, out_shapes = [], [], []
    for w in weights:
        rows, cols = w.shape
        assert rows % (n_steps * BF16_ROWS) == 0
        block = (rows // n_steps, cols)
        spec = pl.BlockSpec(block, lambda *idx: (step_index(*idx), 0))
        in_specs.append(spec)
        out_specs.append(spec)
        out_shapes.append(jax.ShapeDtypeStruct(w.shape, BF16))
    return in_specs, out_specs, out_shapes


def _cast_blocks(src_refs, dst_refs):
    for src, dst in zip(src_refs, dst_refs):
        dst[...] = src[...].astype(BF16)


def _norm_matmul_kernel(x_ref, g_ref, w_ref, *refs):
    n_casts = (len(refs) - 1) // 2
    o_ref = refs[n_casts]
    h = _rms_norm(x_ref[...], g_ref[...]).astype(BF16)
    o_ref[...] = jnp.dot(h, w_ref[...], preferred_element_type=F32).astype(o_ref.dtype)
    _cast_blocks(refs[:n_casts], refs[n_casts + 1:])


def _norm_matmul(x, gain, w, *, tm, tn, casts=()):
    m, d = x.shape
    n = w.shape[1]
    ni = m // tm
    cast_in, cast_out, cast_shapes = _cast_specs(casts, (n // tn) * ni, lambda j, i: j * ni + i)
    out, *casted = pl.pallas_call(
        _norm_matmul_kernel,
        grid=(n // tn, ni),
        in_specs=[
            pl.BlockSpec((tm, d), lambda j, i: (i, 0)),
            pl.BlockSpec((1, d), lambda j, i: (0, 0)),
            pl.BlockSpec((d, tn), lambda j, i: (0, j)),
        ] + cast_in,
        out_specs=[pl.BlockSpec((tm, tn), lambda j, i: (i, j))] + cast_out,
        out_shape=[jax.ShapeDtypeStruct((m, n), BF16)] + cast_shapes,
        compiler_params=_params("parallel", "parallel"),
        name="norm_matmul",
    )(x, gain.reshape(1, d), w, *casts)
    return out, casted


def _retention_kernel(lg_ref, q_ref, k_ref, v_ref, g_ref, gn_ref, o_ref, state_ref, *,
                      chunk, heads):
    rows = q_ref.shape[0]
    d = q_ref.shape[1] // heads
    c = chunk
    k_scale = d ** -0.5

    @pl.when(pl.program_id(2) == 0)
    def _():
        state_ref[...] = jnp.zeros_like(state_ref)

    diff = (lax.broadcasted_iota(jnp.int32, (c, c), 0)
            - lax.broadcasted_iota(jnp.int32, (c, c), 1)).astype(F32)
    pos = lax.broadcasted_iota(jnp.int32, (c, d), 0).astype(F32)
    tables = []
    for t in range(heads):
        lg = lg_ref[pl.program_id(1) * heads + t]
        intra = jnp.where(diff >= 0, jnp.exp(jnp.maximum(diff, 0.0) * lg), 0.0) * k_scale
        q_decay = jnp.exp((pos + 1.0) * lg)
        k_decay = jnp.exp((c - 1.0 - pos) * lg) * k_scale
        chunk_decay = jnp.exp(jnp.zeros((1, d), F32) + c * lg)
        tables.append((intra, q_decay, k_decay, chunk_decay))

    for s in range(rows // c):
        sl = pl.ds(s * c, c)
        for t, (intra, q_decay, k_decay, chunk_decay) in enumerate(tables):
            cols = slice(t * d, (t + 1) * d)
            q = q_ref[sl, cols]
            k = k_ref[sl, cols]
            v = v_ref[sl, cols]
            scores = lax.dot_general(q, k, _NT_DIMS, preferred_element_type=F32) * intra
            y = jnp.dot(scores.astype(BF16), v, preferred_element_type=F32)
            state = state_ref[t]
            y = y + jnp.dot(q, state.astype(BF16), preferred_element_type=F32) * q_decay
            kd = (k.astype(F32) * k_decay).astype(BF16)
            kv = lax.dot_general(kd, v, _TN_DIMS, preferred_element_type=F32)
            state_ref[t] = state * chunk_decay + kv
            g = g_ref[sl, cols].astype(F32)
            o_ref[sl, cols] = (_silu(g) * _rms_norm(y, gn_ref[:, cols])).astype(o_ref.dtype)


def _retention(proj, gn_gain, *, batch, seq, rows, heads):
    m = proj.shape[0]
    d = gn_gain.shape[0] // RET_HEADS
    nt = seq // rows
    groups = RET_HEADS // heads
    lg = np.log1p(-np.exp2(-5.0 - np.arange(RET_HEADS))).astype(np.float32)

    def spec(part):
        return pl.BlockSpec((rows, heads * d),
                            lambda b, h, t, lg_ref: (b * nt + t, part * groups + h))

    return pl.pallas_call(
        functools.partial(_retention_kernel, chunk=RET_CHUNK, heads=heads),
        grid_spec=pltpu.PrefetchScalarGridSpec(
            num_scalar_prefetch=1,
            grid=(batch, groups, nt),
            in_specs=[spec(0), spec(1), spec(2), spec(3),
                      pl.BlockSpec((1, heads * d), lambda b, h, t, lg_ref: (0, h))],
            out_specs=pl.BlockSpec((rows, heads * d), lambda b, h, t, lg_ref: (b * nt + t, h)),
            scratch_shapes=[pltpu.VMEM((heads, d, d), F32)],
        ),
        out_shape=jax.ShapeDtypeStruct((m, RET_HEADS * d), BF16),
        compiler_params=_params("parallel", "parallel", "arbitrary"),
        name="retention",
    )(jnp.asarray(lg), proj, proj, proj, proj, gn_gain.reshape(1, -1))


def _alibi_slopes(n):
    def pow2_slopes(m):
        start = 2.0 ** (-8.0 / m)
        return [start ** (i + 1) for i in range(m)]
    if math.log2(n).is_integer():
        s = pow2_slopes(n)
    else:
        c = 2 ** int(math.floor(math.log2(n)))
        s = pow2_slopes(c) + list(_alibi_slopes(2 * c))[0::2][: n - c]
    return np.asarray(s, dtype=np.float32)


MOBA_MASKED = -1e30
AUG_PARTS = 3
AUG_PART_LANE = SUBLANES


def _split_bf16(x):
    parts = []
    for _ in range(AUG_PARTS):
        part = x.astype(BF16).astype(F32)
        parts.append(part)
        x = x - part
    return parts


def _moba_kernel(slope_ref, q_ref, k_ref, v_ref, o_ref, kmean_ref, vt_ref, kaug_ref, s_ref, *,
                 heads, group):
    qi = pl.program_id(2)
    blk = q_ref.shape[0]
    d = q_ref.shape[1] // heads
    nblk = k_ref.shape[0] // blk
    scale = d ** -0.5
    exp2_scale = scale * math.log2(math.e)
    assert AUG_PART_LANE + AUG_PARTS * nblk <= d

    def head_cols(t):
        return slice(t * d, (t + 1) * d)

    def head_slope(t):
        return slope_ref[pl.program_id(0) * heads + t]

    @pl.when((qi == 0) & (pl.program_id(1) == 0))
    def _():
        lane = lax.broadcasted_iota(jnp.int32, (blk, d), 1)
        row = lax.broadcasted_iota(jnp.int32, (blk, d), 0)
        for t in range(heads):
            key_scale = head_slope(t) * (1.0 / scale)
            for j in range(nblk):
                aug = jnp.zeros((blk, d), F32)
                for p, part in enumerate(_split_bf16((row + j * blk).astype(F32) * key_scale)):
                    aug = jnp.where(lane == p, part, aug)
                    aug = jnp.where(lane == AUG_PART_LANE + p * nblk + j, 1.0, aug)
                kaug_ref[t, j * blk:(j + 1) * blk, :] = aug.astype(BF16)

    @pl.when(qi == 0)
    def _():
        ones_row = jnp.where(lax.broadcasted_iota(jnp.int32, (BF16_ROWS, blk), 0) == 0, 1.0, 0.0)
        for t in range(heads):
            for j in range(nblk):
                rows = slice(j * blk, (j + 1) * blk)
                kmean_ref[t, j:j + 1, :] = jnp.sum(k_ref[rows, head_cols(t)].astype(F32), axis=0,
                                                   keepdims=True) * (1.0 / blk)
                vt_ref[t, j] = jnp.concatenate(
                    [v_ref[rows, head_cols(t)].astype(F32).T, ones_row], axis=0).astype(BF16)

    def query_operand(t):
        q = q_ref[:, head_cols(t)]
        kmean = kmean_ref[t]
        kmean_hi = kmean.astype(BF16)
        kmean_lo = (kmean - kmean_hi.astype(F32)).astype(BF16)
        gate = (lax.dot_general(kmean_hi, q, _NT_DIMS, preferred_element_type=F32)
                + lax.dot_general(kmean_lo, q, _NT_DIMS, preferred_element_type=F32))
        blk_id = lax.broadcasted_iota(jnp.int32, gate.shape, 0)
        rank = jnp.zeros(gate.shape, jnp.int32)
        for other in range(nblk):
            g_other = gate[other:other + 1, :]
            beats = (g_other > gate) | ((g_other == gate) & (other < blk_id))
            rank = rank + jnp.where(other < qi, jnp.where(beats, 1, 0), 0)
        keep = ((blk_id < qi) & (rank < MOBA_TOPK)) | (blk_id == qi)
        t_query = (qi * blk + lax.broadcasted_iota(jnp.int32, gate.shape, 1)).astype(F32)
        bias = (jnp.where(keep, 0.0, MOBA_MASKED) - head_slope(t) * t_query) * (1.0 / scale)
        ones = jnp.where(lax.broadcasted_iota(jnp.int32, (AUG_PART_LANE, blk), 0) < AUG_PARTS,
                         1.0, 0.0)
        pad = jnp.zeros((d - AUG_PART_LANE - AUG_PARTS * nblk, blk), F32)
        aug = jnp.concatenate([ones] + _split_bf16(bias) + [pad], axis=0)
        return jnp.concatenate([q, aug.T.astype(BF16)], axis=1)

    def scores(t, q_op, g):
        rows = pl.ds(pl.multiple_of(g * group * blk, group * blk), group * blk)
        k_op = jnp.concatenate([k_ref[rows, head_cols(t)], kaug_ref[t, rows, :]], axis=1)
        return lax.dot_general(k_op, q_op, _NT_DIMS, preferred_element_type=F32)

    q_ops = [query_operand(t) for t in range(heads)]

    own_group = qi // group
    key_minus_query = (lax.broadcasted_iota(jnp.int32, (blk, blk), 0)
                       - lax.broadcasted_iota(jnp.int32, (blk, blk), 1))

    def group_scores(t, g, causal):
        best = None
        group_s = scores(t, q_ops[t], g)
        for i in range(group):
            j = g * group + i
            s = group_s[i * blk:(i + 1) * blk] * exp2_scale
            if causal:
                s = jnp.where(key_minus_query <= (qi - j) * blk, s, MOBA_MASKED)
            s_ref[t, g, i] = s
            block_max = jnp.max(s, axis=0, keepdims=True)
            best = block_max if best is None else jnp.maximum(best, block_max)
        return best

    def past_scores(g, best):
        return tuple(jnp.maximum(best[t], group_scores(t, g, False)) for t in range(heads))

    best = lax.fori_loop(0, own_group, past_scores,
                         tuple(group_scores(t, own_group, True) for t in range(heads)))

    def group_values(g, accs):
        out = []
        for t in range(heads):
            p = jnp.concatenate([jnp.exp2(s_ref[t, g, i] - best[t]).astype(BF16)
                                 for i in range(group)], axis=0)
            vt = jnp.concatenate([vt_ref[t, g * group + i] for i in range(group)], axis=1)
            out.append(accs[t] + jnp.dot(vt, p, preferred_element_type=F32))
        return tuple(out)

    zero = jnp.zeros((d + BF16_ROWS, blk), F32)
    totals = lax.fori_loop(0, own_group + 1, group_values, (zero,) * heads)
    for t in range(heads):
        acc = totals[t]
        o_ref[:, head_cols(t)] = (acc[:d] / acc[d:d + 1]).T.astype(o_ref.dtype)


def _moba(proj, *, batch, seq, d, heads, group):
    m = proj.shape[0]
    blk = MOBA_BLOCK
    assert seq % (group * blk) == 0 and MOBA_HEADS % heads == 0
    nq = seq // blk
    groups = MOBA_HEADS // heads
    slopes = _alibi_slopes(MOBA_HEADS)
    return pl.pallas_call(
        functools.partial(_moba_kernel, heads=heads, group=group),
        grid_spec=pltpu.PrefetchScalarGridSpec(
            num_scalar_prefetch=1,
            grid=(groups, batch, nq),
            in_specs=[
                pl.BlockSpec((blk, heads * d), lambda h, b, i, s_ref: (b * nq + i, h)),
                pl.BlockSpec((seq, heads * d), lambda h, b, i, s_ref: (b, groups + h)),
                pl.BlockSpec((seq, heads * d), lambda h, b, i, s_ref: (b, 2 * groups + h)),
            ],
            out_specs=pl.BlockSpec((blk, heads * d), lambda h, b, i, s_ref: (b * nq + i, h)),
            scratch_shapes=[pltpu.VMEM((heads, nq, d), F32),
                            pltpu.VMEM((heads, nq, d + BF16_ROWS, blk), BF16),
                            pltpu.VMEM((heads, seq, d), BF16),
                            pltpu.VMEM((heads, nq // group, group, blk, blk), F32)],
        ),
        out_shape=jax.ShapeDtypeStruct((m, MOBA_HEADS * d), BF16),
        compiler_params=_params("parallel", "arbitrary", "arbitrary"),
        name="moba",
    )(jnp.asarray(slopes), proj, proj, proj)


def _memory_attention(qm_ref, kv_ref):
    width = qm_ref.shape[1]
    dh = width // MEM_HEADS
    outs = []
    for h in range(MEM_HEADS):
        q = qm_ref[:, h * dh:(h + 1) * dh]
        k = kv_ref[:, h * dh:(h + 1) * dh]
        v = kv_ref[:, width + h * dh:width + (h + 1) * dh]
        s = lax.dot_general(q, k, _NT_DIMS, preferred_element_type=F32) * dh ** -0.5
        p = jnp.exp(s - jnp.max(s, axis=-1, keepdims=True))
        o = jnp.dot(p.astype(BF16), v, preferred_element_type=F32)
        outs.append((o / jnp.sum(p, axis=-1, keepdims=True)).astype(BF16))
    return jnp.concatenate(outs, axis=-1)


def _project_out(tok, mem_out, x_ref, wo_ref, o_ref):
    mix = tok.shape[1]
    o_ref[...] = (x_ref[...]
                  + jnp.dot(tok, wo_ref[:mix, :], preferred_element_type=F32)
                  + jnp.dot(mem_out, wo_ref[mix:, :], preferred_element_type=F32))


def _out_proj_kernel(tok_ref, qm_ref, kv_ref, x_ref, wo_ref, *refs):
    n_casts = (len(refs) - 1) // 2
    _project_out(tok_ref[...], _memory_attention(qm_ref, kv_ref), x_ref, wo_ref, refs[n_casts])
    _cast_blocks(refs[:n_casts], refs[n_casts + 1:])


def _conv_out_proj_kernel(gb_ref, gc_ref, hh_ref, gc_prev_ref, hh_prev_ref, cw_ref,
                          qm_ref, kv_ref, x_ref, wo_ref, *refs, tiles_per_seq):
    n_casts = (len(refs) - 1) // 2
    first = pl.program_id(0) % tiles_per_seq == 0
    u = gc_ref[...].astype(F32) * hh_ref[...].astype(F32)
    prev = gc_prev_ref[...].astype(F32) * hh_prev_ref[...].astype(F32)
    prev = jnp.where(first, 0.0, prev)
    tok = (gb_ref[...].astype(F32) * _causal_conv3(u, prev, cw_ref[...])).astype(BF16)
    _project_out(tok, _memory_attention(qm_ref, kv_ref), x_ref, wo_ref, refs[n_casts])
    _cast_blocks(refs[:n_casts], refs[n_casts + 1:])


def _out_proj(tok, proj, conv_w, kv, x, wo, *, seq, mem_tokens, tm, casts=()):
    m, d = x.shape
    mem_width = kv.shape[1] // 2
    mix = d - mem_width
    tiles_per_seq = seq // tm
    qm_block = proj.shape[1] // mem_width - 1
    cast_in, cast_out, cast_shapes = _cast_specs(casts, m // tm, lambda i: i)
    tail_specs = [
        pl.BlockSpec((tm, mem_width), lambda i: (i, qm_block)),
        pl.BlockSpec((mem_tokens, 2 * mem_width), lambda i: (i // tiles_per_seq, 0)),
        pl.BlockSpec((tm, d), lambda i: (i, 0)),
        pl.BlockSpec((d, d), lambda i: (0, 0), pipeline_mode=pl.Buffered(1)),
    ] + cast_in
    if tok is not None:
        kernel = _out_proj_kernel
        head_specs = [pl.BlockSpec((tm, mix), lambda i: (i, 0))]
        head_args = (tok,)
    else:
        kernel = functools.partial(_conv_out_proj_kernel, tiles_per_seq=tiles_per_seq)
        halo = tm // SUBLANES

        def prev_rows(part):
            return pl.BlockSpec((SUBLANES, mix), lambda i: (jnp.maximum(i * halo - 1, 0), part))

        head_specs = [pl.BlockSpec((tm, mix), lambda i: (i, 0)),
                      pl.BlockSpec((tm, mix), lambda i: (i, 1)),
                      pl.BlockSpec((tm, mix), lambda i: (i, 2)),
                      prev_rows(1), prev_rows(2),
                      pl.BlockSpec((CONV_WIDTH, mix), lambda i: (0, 0))]
        head_args = (proj, proj, proj, proj, proj, conv_w)
    out, *casted = pl.pallas_call(
        kernel,
        grid=(m // tm,),
        in_specs=head_specs + tail_specs,
        out_specs=[pl.BlockSpec((tm, d), lambda i: (i, 0))] + cast_out,
        out_shape=[jax.ShapeDtypeStruct((m, d), F32)] + cast_shapes,
        compiler_params=_params("parallel"),
        name="out_proj",
    )(*head_args, proj, kv, x, wo, *casts)
    return out, casted


def _conv_ffn_kernel(x_ref, g_ref, wg_ref, wu_ref, conv_ref, wd_ref,
                     fg_ref, o_ref, h_ref, carry_ref, up_ref, *,
                     tiles_per_seq, row_strips, final_norm):
    f = pl.program_id(1)
    nf = pl.num_programs(1)
    first = pl.program_id(0) % tiles_per_seq == 0
    tm = x_ref.shape[0]

    @pl.when(f == 0)
    def _():
        x = x_ref[...]
        h_ref[...] = _rms_norm(x, g_ref[...]).astype(BF16)
        o_ref[...] = x

    @pl.when(first)
    def _():
        carry_ref[f] = jnp.zeros(carry_ref.shape[1:], F32)

    strip = tm // row_strips
    for slot in range(2):
        up_ref[slot, :SUBLANES, :] = carry_ref[f, slot]
    for s in range(row_strips):
        for slot, w_ref in enumerate((wg_ref, wu_ref)):
            up_ref[slot, pl.ds(SUBLANES + s * strip, strip), :] = jnp.dot(
                h_ref[pl.ds(s * strip, strip), :], w_ref[...], preferred_element_type=F32)
    conv = [conv_ref[f + slot * nf] for slot in range(2)]
    for s in range(row_strips):
        branches = []
        for slot in range(2):
            taps = [up_ref[slot, pl.ds(SUBLANES + s * strip - (CONV_WIDTH - 1 - k), strip), :]
                    for k in range(CONV_WIDTH)]
            branches.append(conv[slot][0:1] * taps[0] + conv[slot][1:2] * taps[1]
                            + conv[slot][2:3] * taps[2] + conv[slot][CONV_WIDTH:CONV_WIDTH + 1])
        act = (_silu(branches[0]) * branches[1]).astype(BF16)
        o_ref[pl.ds(s * strip, strip), :] += jnp.dot(act, wd_ref[...], preferred_element_type=F32)
    for slot in range(2):
        carry_ref[f, slot] = up_ref[slot, pl.ds(tm, SUBLANES), :]

    if final_norm:
        @pl.when(f == pl.num_programs(1) - 1)
        def _():
            o_ref[...] = _rms_norm(o_ref[...], fg_ref[...])


def _conv_ffn(x, gain, w_up, conv_w, conv_b, w_down, final_gain, *, seq, tm, tf, final_norm):
    m, d = x.shape
    d_ff = w_down.shape[0]
    nf = d_ff // tf
    conv = jnp.concatenate(
        [conv_w.reshape(CONV_WIDTH, 2 * nf, tf).transpose(1, 0, 2),
         conv_b.reshape(2 * nf, 1, tf),
         jnp.zeros((2 * nf, SUBLANES - CONV_WIDTH - 1, tf), F32)], axis=1)
    return pl.pallas_call(
        functools.partial(_conv_ffn_kernel, tiles_per_seq=seq // tm, row_strips=FFN_ROW_STRIPS,
                          final_norm=final_norm),
        grid=(m // tm, nf),
        in_specs=[
            pl.BlockSpec((tm, d), lambda i, f: (i, 0)),
            pl.BlockSpec((1, d), lambda i, f: (0, 0)),
            pl.BlockSpec((d, tf), lambda i, f: (0, f)),
            pl.BlockSpec((d, tf), lambda i, f: (0, nf + f)),
            pl.BlockSpec((2 * nf, SUBLANES, tf), lambda i, f: (0, 0, 0)),
            pl.BlockSpec((tf, d), lambda i, f: (f, 0)),
            pl.BlockSpec((1, d), lambda i, f: (0, 0)),
        ],
        out_specs=pl.BlockSpec((tm, d), lambda i, f: (i, 0)),
        out_shape=jax.ShapeDtypeStruct((m, d), F32),
        scratch_shapes=[pltpu.VMEM((tm, d), BF16),
                        pltpu.VMEM((nf, 2, SUBLANES, tf), F32),
                        pltpu.VMEM((2, SUBLANES + tm, tf), F32)],
        compiler_params=_params("arbitrary", "arbitrary"),
        name="conv_ffn",
    )(x, gain.reshape(1, d), w_up, w_up, conv, w_down, final_gain.reshape(1, d))


MXU_WIDTH = 256

PROJ_ROWS = 512
FFN_ROWS = 1024
FFN_COLS = 512
FFN_ROW_STRIPS = 2
RET_HEADS_PER_STEP = 2
MOBA_HEADS_PER_STEP = 4
MOBA_GROUP = 4


def _half_tile(n):
    return n // 2 if n % (2 * MXU_WIDTH) == 0 else n


def _trunk(x, mem, mem_norm, layers, final_norm):
    batch, seq, d = x.shape
    mem_tokens = mem.shape[1]
    x = x.reshape(batch * seq, d)
    mem = mem.reshape(batch * mem_tokens, d)
    n_layers = len(layers)
    w_in = layers[0][1].astype(BF16)
    w_mem_kv = layers[0][3].astype(BF16)
    for i, (norm_mix, _, extra, _, w_o, norm_ffn, w_up, cw, cb, w_down) in enumerate(layers):
        kind = i % 3
        mem_width = w_mem_kv.shape[1] // 2
        mix = d - mem_width
        proj, (w_up, w_o) = _norm_matmul(x, norm_mix, w_in, tm=PROJ_ROWS,
                                         tn=_half_tile(w_in.shape[1]), casts=(w_up, w_o))
        kv, _ = _norm_matmul(mem, mem_norm, w_mem_kv, tm=mem_tokens, tn=w_mem_kv.shape[1])
        if kind == 0:
            tok = _retention(proj, extra, batch=batch, seq=seq, rows=seq, heads=RET_HEADS_PER_STEP)
        elif kind == 1:
            tok = None
        else:
            tok = _moba(proj, batch=batch, seq=seq, d=mix // MOBA_HEADS,
                        heads=MOBA_HEADS_PER_STEP, group=MOBA_GROUP)
        next_weights = (layers[i + 1][1], layers[i + 1][3]) if i + 1 < n_layers else ()
        x, (w_down, *next_weights) = _out_proj(tok, proj, extra, kv, x, w_o, seq=seq,
                                               mem_tokens=mem_tokens, tm=PROJ_ROWS,
                                               casts=(w_down,) + next_weights)
        if next_weights:
            w_in, w_mem_kv = next_weights
        x = _conv_ffn(x, norm_ffn, w_up, cw, cb, w_down, final_norm, seq=seq, tm=FFN_ROWS,
                      tf=FFN_COLS, final_norm=(i == n_layers - 1))
    return x.reshape(batch, seq, d)


def kernel(x, mem, mem_norm,
           l0_norm_mix, l0_w_in, l0_ret_gn, l0_w_mem_kv, l0_w_o, l0_norm_ffn, l0_ffn_w_up, l0_ffn_conv_w, l0_ffn_conv_b, l0_ffn_w_down,
           l1_norm_mix, l1_w_in, l1_conv_w, l1_w_mem_kv, l1_w_o, l1_norm_ffn, l1_ffn_w_up, l1_ffn_conv_w, l1_ffn_conv_b, l1_ffn_w_down,
           l2_norm_mix, l2_w_in, l2_w_mem_kv, l2_w_o, l2_norm_ffn, l2_ffn_w_up, l2_ffn_conv_w, l2_ffn_conv_b, l2_ffn_w_down,
           l3_norm_mix, l3_w_in, l3_ret_gn, l3_w_mem_kv, l3_w_o, l3_norm_ffn, l3_ffn_w_up, l3_ffn_conv_w, l3_ffn_conv_b, l3_ffn_w_down,
           final_norm):
    layers = [
        (l0_norm_mix, l0_w_in, l0_ret_gn, l0_w_mem_kv, l0_w_o, l0_norm_ffn, l0_ffn_w_up, l0_ffn_conv_w, l0_ffn_conv_b, l0_ffn_w_down),
        (l1_norm_mix, l1_w_in, l1_conv_w, l1_w_mem_kv, l1_w_o, l1_norm_ffn, l1_ffn_w_up, l1_ffn_conv_w, l1_ffn_conv_b, l1_ffn_w_down),
        (l2_norm_mix, l2_w_in, None, l2_w_mem_kv, l2_w_o, l2_norm_ffn, l2_ffn_w_up, l2_ffn_conv_w, l2_ffn_conv_b, l2_ffn_w_down),
        (l3_norm_mix, l3_w_in, l3_ret_gn, l3_w_mem_kv, l3_w_o, l3_norm_ffn, l3_ffn_w_up, l3_ffn_conv_w, l3_ffn_conv_b, l3_ffn_w_down),
    ]
    return _trunk(x, mem, mem_norm, layers, final_norm)
```

```python
import functools
import math

import jax
import jax.numpy as jnp
import numpy as np
from jax import lax
from jax.experimental import pallas as pl
from jax.experimental.pallas import tpu as pltpu

F32 = jnp.float32
BF16 = jnp.bfloat16

NORM_EPS = 1e-6
MEM_HEADS = 4
RET_HEADS = 6
RET_CHUNK = 256
CONV_WIDTH = 3
MOBA_HEADS = 12
MOBA_BLOCK = 256
MOBA_TOPK = 3

SUBLANES = 8
VMEM_LIMIT_BYTES = 56 * 1024 * 1024

_NT_DIMS = (((1,), (1,)), ((), ()))
_TN_DIMS = (((0,), (0,)), ((), ()))


def _params(*semantics):
    return pltpu.CompilerParams(dimension_semantics=semantics,
                                vmem_limit_bytes=VMEM_LIMIT_BYTES)


def _rms_norm(x, gain):
    return x * lax.rsqrt(jnp.mean(x * x, axis=-1, keepdims=True) + NORM_EPS) * gain


def _silu(g):
    return g * (1.0 / (1.0 + jnp.exp(-g)))


def _shift_rows(u, prev, k):
    rolled = pltpu.roll(u, k, axis=0)
    prev_rolled = pltpu.roll(prev, k, axis=0)
    row = lax.broadcasted_iota(jnp.int32, prev.shape, 0)
    head = jnp.where(row < k, prev_rolled, rolled[:SUBLANES])
    return jnp.concatenate([head, rolled[SUBLANES:]], axis=0)


def _causal_conv3(u, prev, w):
    return (w[0:1] * _shift_rows(u, prev, 2) + w[1:2] * _shift_rows(u, prev, 1)
            + w[2:3] * u)


BF16_ROWS = 16


def _cast_specs(weights, n_steps, step_index):
    in_specs, out_specs, out_shapes = [], [], []
    for w in weights:
        rows, cols = w.shape
        assert rows % (n_steps * BF16_ROWS) == 0
        block = (rows // n_steps, cols)
        spec = pl.BlockSpec(block, lambda *idx: (step_index(*idx), 0))
        in_specs.append(spec)
        out_specs.append(spec)
        out_shapes.append(jax.ShapeDtypeStruct(w.shape, BF16))
    return in_specs, out_specs, out_shapes


def _cast_blocks(src_refs, dst_refs):
    for src, dst in zip(src_refs, dst_refs):
        dst[...] = src[...].astype(BF16)


def _norm_matmul_kernel(x_ref, g_ref, w_ref, *refs):
    n_casts = (len(refs) - 1) // 2
    o_ref = refs[n_casts]
    h = _rms_norm(x_ref[...], g_ref[...]).astype(BF16)
    o_ref[...] = jnp.dot(h, w_ref[...], preferred_element_type=F32).astype(o_ref.dtype)
    _cast_blocks(refs[:n_casts], refs[n_casts + 1:])


def _norm_matmul(x, gain, w, *, tm, tn, casts=()):
    m, d = x.shape
    n = w.shape[1]
    ni = m // tm
    cast_in, cast_out, cast_shapes = _cast_specs(casts, (n // tn) * ni, lambda j, i: j * ni + i)
    out, *casted = pl.pallas_call(
        _norm_matmul_kernel,
        grid=(n // tn, ni),
        in_specs=[
            pl.BlockSpec((tm, d), lambda j, i: (i, 0)),
            pl.BlockSpec((1, d), lambda j, i: (0, 0)),
            pl.BlockSpec((d, tn), lambda j, i: (0, j)),
        ] + cast_in,
        out_specs=[pl.BlockSpec((tm, tn), lambda j, i: (i, j))] + cast_out,
        out_shape=[jax.ShapeDtypeStruct((m, n), BF16)] + cast_shapes,
        compiler_params=_params("parallel", "parallel"),
        name="norm_matmul",
    )(x, gain.reshape(1, d), w, *casts)
    return out, casted


def _retention_kernel(lg_ref, q_ref, k_ref, v_ref, g_ref, gn_ref, o_ref, state_ref, *,
                      chunk, heads):
    rows = q_ref.shape[0]
    d = q_ref.shape[1] // heads
    c = chunk
    k_scale = d ** -0.5

    @pl.when(pl.program_id(2) == 0)
    def _():
        state_ref[...] = jnp.zeros_like(state_ref)

    diff = (lax.broadcasted_iota(jnp.int32, (c, c), 0)
            - lax.broadcasted_iota(jnp.int32, (c, c), 1)).astype(F32)
    pos = lax.broadcasted_iota(jnp.int32, (c, d), 0).astype(F32)
    tables = []
    for t in range(heads):
        lg = lg_ref[pl.program_id(1) * heads + t]
        intra = jnp.where(diff >= 0, jnp.exp(jnp.maximum(diff, 0.0) * lg), 0.0) * k_scale
        q_decay = jnp.exp((pos + 1.0) * lg)
        k_decay = jnp.exp((c - 1.0 - pos) * lg) * k_scale
        chunk_decay = jnp.exp(jnp.zeros((1, d), F32) + c * lg)
        tables.append((intra, q_decay, k_decay, chunk_decay))

    for s in range(rows // c):
        sl = pl.ds(s * c, c)
        for t, (intra, q_decay, k_decay, chunk_decay) in enumerate(tables):
            cols = slice(t * d, (t + 1) * d)
            q = q_ref[sl, cols]
            k = k_ref[sl, cols]
            v = v_ref[sl, cols]
            scores = lax.dot_general(q, k, _NT_DIMS, preferred_element_type=F32) * intra
            y = jnp.dot(scores.astype(BF16), v, preferred_element_type=F32)
            state = state_ref[t]
            y = y + jnp.dot(q, state.astype(BF16), preferred_element_type=F32) * q_decay
            kd = (k.astype(F32) * k_decay).astype(BF16)
            kv = lax.dot_general(kd, v, _TN_DIMS, preferred_element_type=F32)
            state_ref[t] = state * chunk_decay + kv
            g = g_ref[sl, cols].astype(F32)
            o_ref[sl, cols] = (_silu(g) * _rms_norm(y, gn_ref[:, cols])).astype(o_ref.dtype)


def _retention(proj, gn_gain, *, batch, seq, rows, heads):
    m = proj.shape[0]
    d = gn_gain.shape[0] // RET_HEADS
    nt = seq // rows
    groups = RET_HEADS // heads
    lg = np.log1p(-np.exp2(-5.0 - np.arange(RET_HEADS))).astype(np.float32)

    def spec(part):
        return pl.BlockSpec((rows, heads * d),
                            lambda b, h, t, lg_ref: (b * nt + t, part * groups + h))

    return pl.pallas_call(
        functools.partial(_retention_kernel, chunk=RET_CHUNK, heads=heads),
        grid_spec=pltpu.PrefetchScalarGridSpec(
            num_scalar_prefetch=1,
            grid=(batch, groups, nt),
            in_specs=[spec(0), spec(1), spec(2), spec(3),
                      pl.BlockSpec((1, heads * d), lambda b, h, t, lg_ref: (0, h))],
            out_specs=pl.BlockSpec((rows, heads * d), lambda b, h, t, lg_ref: (b * nt + t, h)),
            scratch_shapes=[pltpu.VMEM((heads, d, d), F32)],
        ),
        out_shape=jax.ShapeDtypeStruct((m, RET_HEADS * d), BF16),
        compiler_params=_params("parallel", "parallel", "arbitrary"),
        name="retention",
    )(jnp.asarray(lg), proj, proj, proj, proj, gn_gain.reshape(1, -1))


def _alibi_slopes(n):
    def pow2_slopes(m):
        start = 2.0 ** (-8.0 / m)
        return [start ** (i + 1) for i in range(m)]
    if math.log2(n).is_integer():
        s = pow2_slopes(n)
    else:
        c = 2 ** int(math.floor(math.log2(n)))
        s = pow2_slopes(c) + list(_alibi_slopes(2 * c))[0::2][: n - c]
    return np.asarray(s, dtype=np.float32)


MOBA_MASKED = -1e30
AUG_PARTS = 3
AUG_PART_LANE = SUBLANES


def _split_bf16(x):
    parts = []
    for _ in range(AUG_PARTS):
        part = x.astype(BF16).astype(F32)
        parts.append(part)
        x = x - part
    return parts


def _moba_kernel(slope_ref, q_ref, k_ref, v_ref, o_ref, kmean_ref, vt_ref, kaug_ref, s_ref, *,
                 heads, group):
    qi = pl.program_id(2)
    blk = q_ref.shape[0]
    d = q_ref.shape[1] // heads
    nblk = k_ref.shape[0] // blk
    scale = d ** -0.5
    exp2_scale = scale * math.log2(math.e)
    assert AUG_PART_LANE + AUG_PARTS * nblk <= d

    def head_cols(t):
        return slice(t * d, (t + 1) * d)

    def head_slope(t):
        return slope_ref[pl.program_id(0) * heads + t]

    @pl.when((qi == 0) & (pl.program_id(1) == 0))
    def _():
        lane = lax.broadcasted_iota(jnp.int32, (blk, d), 1)
        row = lax.broadcasted_iota(jnp.int32, (blk, d), 0)
        for t in range(heads):
            key_scale = head_slope(t) * (1.0 / scale)
            for j in range(nblk):
                aug = jnp.zeros((blk, d), F32)
                for p, part in enumerate(_split_bf16((row + j * blk).astype(F32) * key_scale)):
                    aug = jnp.where(lane == p, part, aug)
                    aug = jnp.where(lane == AUG_PART_LANE + p * nblk + j, 1.0, aug)
                kaug_ref[t, j * blk:(j + 1) * blk, :] = aug.astype(BF16)

    @pl.when(qi == 0)
    def _():
        ones_row = jnp.where(lax.broadcasted_iota(jnp.int32, (BF16_ROWS, blk), 0) == 0, 1.0, 0.0)
        for t in range(heads):
            for j in range(nblk):
                rows = slice(j * blk, (j + 1) * blk)
                kmean_ref[t, j:j + 1, :] = jnp.sum(k_ref[rows, head_cols(t)].astype(F32), axis=0,
                                                   keepdims=True) * (1.0 / blk)
                vt_ref[t, j] = jnp.concatenate(
                    [v_ref[rows, head_cols(t)].astype(F32).T, ones_row], axis=0).astype(BF16)

    def query_operand(t):
        q = q_ref[:, head_cols(t)]
        kmean = kmean_ref[t]
        kmean_hi = kmean.astype(BF16)
        kmean_lo = (kmean - kmean_hi.astype(F32)).astype(BF16)
        gate = (lax.dot_general(kmean_hi, q, _NT_DIMS, preferred_element_type=F32)
                + lax.dot_general(kmean_lo, q, _NT_DIMS, preferred_element_type=F32))
        blk_id = lax.broadcasted_iota(jnp.int32, gate.shape, 0)
        rank = jnp.zeros(gate.shape, jnp.int32)
        for other in range(nblk):
            g_other = gate[other:other + 1, :]
            beats = (g_other > gate) | ((g_other == gate) & (other < blk_id))
            rank = rank + jnp.where(other < qi, jnp.where(beats, 1, 0), 0)
        keep = ((blk_id < qi) & (rank < MOBA_TOPK)) | (blk_id == qi)
        t_query = (qi * blk + lax.broadcasted_iota(jnp.int32, gate.shape, 1)).astype(F32)
        bias = (jnp.where(keep, 0.0, MOBA_MASKED) - head_slope(t) * t_query) * (1.0 / scale)
        ones = jnp.where(lax.broadcasted_iota(jnp.int32, (AUG_PART_LANE, blk), 0) < AUG_PARTS,
                         1.0, 0.0)
        pad = jnp.zeros((d - AUG_PART_LANE - AUG_PARTS * nblk, blk), F32)
        aug = jnp.concatenate([ones] + _split_bf16(bias) + [pad], axis=0)
        return jnp.concatenate([q, aug.T.astype(BF16)], axis=1)

    q_ops = [query_operand(t) for t in range(heads)]

    own_group = qi // group
    key_after_query = (lax.broadcasted_iota(jnp.int32, (blk, blk), 0)
                       > lax.broadcasted_iota(jnp.int32, (blk, blk), 1))

    def group_scores(t, g, n_blocks, own):
        rows = pl.ds(pl.multiple_of(g * group * blk, group * blk), n_blocks * blk)
        k_op = jnp.concatenate([k_ref[rows, head_cols(t)], kaug_ref[t, rows, :]], axis=1)
        group_s = lax.dot_general(k_op, q_ops[t], _NT_DIMS, preferred_element_type=F32)
        best = None
        for i in range(n_blocks):
            s = group_s[i * blk:(i + 1) * blk] * exp2_scale
            if own and i == n_blocks - 1:
                s = jnp.where(key_after_query, MOBA_MASKED, s)
            s_ref[t, g, i] = s
            block_max = jnp.max(s, axis=0, keepdims=True)
            best = block_max if best is None else jnp.maximum(best, block_max)
        return best

    def group_values(t, g, n_blocks, best, acc):
        p = jnp.concatenate([jnp.exp2(s_ref[t, g, i] - best).astype(BF16)
                             for i in range(n_blocks)], axis=0)
        vt = jnp.concatenate([vt_ref[t, g * group + i] for i in range(n_blocks)], axis=1)
        return acc + jnp.dot(vt, p, preferred_element_type=F32)

    def past_scores(g, best):
        return tuple(jnp.maximum(best[t], group_scores(t, g, group, False)) for t in range(heads))

    own_blocks = qi % group
    best = lax.switch(own_blocks, [
        functools.partial(lambda n: tuple(group_scores(t, own_group, n, True)
                                          for t in range(heads)), n + 1)
        for n in range(group)])
    best = lax.fori_loop(0, own_group, past_scores, best)

    def past_values(g, accs):
        return tuple(group_values(t, g, group, best[t], accs[t]) for t in range(heads))

    zero = jnp.zeros((d + BF16_ROWS, blk), F32)
    totals = lax.fori_loop(0, own_group, past_values, (zero,) * heads)
    totals = lax.switch(own_blocks, [
        functools.partial(lambda n, accs: tuple(group_values(t, own_group, n, best[t], accs[t])
                                                for t in range(heads)), n + 1)
        for n in range(group)], totals)
    for t in range(heads):
        acc = totals[t]
        o_ref[:, head_cols(t)] = (acc[:d] / acc[d:d + 1]).T.astype(o_ref.dtype)


def _moba(proj, *, batch, seq, d, heads, group):
    m = proj.shape[0]
    blk = MOBA_BLOCK
    assert seq % (group * blk) == 0 and MOBA_HEADS % heads == 0
    nq = seq // blk
    groups = MOBA_HEADS // heads
    slopes = _alibi_slopes(MOBA_HEADS)
    return pl.pallas_call(
        functools.partial(_moba_kernel, heads=heads, group=group),
        grid_spec=pltpu.PrefetchScalarGridSpec(
            num_scalar_prefetch=1,
            grid=(groups, batch, nq),
            in_specs=[
                pl.BlockSpec((blk, heads * d), lambda h, b, i, s_ref: (b * nq + i, h)),
                pl.BlockSpec((seq, heads * d), lambda h, b, i, s_ref: (b, groups + h)),
                pl.BlockSpec((seq, heads * d), lambda h, b, i, s_ref: (b, 2 * groups + h)),
            ],
            out_specs=pl.BlockSpec((blk, heads * d), lambda h, b, i, s_ref: (b * nq + i, h)),
            scratch_shapes=[pltpu.VMEM((heads, nq, d), F32),
                            pltpu.VMEM((heads, nq, d + BF16_ROWS, blk), BF16),
                            pltpu.VMEM((heads, seq, d), BF16),
                            pltpu.VMEM((heads, nq // group, group, blk, blk), F32)],
        ),
        out_shape=jax.ShapeDtypeStruct((m, MOBA_HEADS * d), BF16),
        compiler_params=_params("parallel", "arbitrary", "arbitrary"),
        name="moba",
    )(jnp.asarray(slopes), proj, proj, proj)


def _memory_attention(qm_ref, kv_ref):
    width = qm_ref.shape[1]
    dh = width // MEM_HEADS
    outs = []
    for h in range(MEM_HEADS):
        q = qm_ref[:, h * dh:(h + 1) * dh]
        k = kv_ref[:, h * dh:(h + 1) * dh]
        v = kv_ref[:, width + h * dh:width + (h + 1) * dh]
        s = lax.dot_general(q, k, _NT_DIMS, preferred_element_type=F32) * dh ** -0.5
        p = jnp.exp(s - jnp.max(s, axis=-1, keepdims=True))
        o = jnp.dot(p.astype(BF16), v, preferred_element_type=F32)
        outs.append((o / jnp.sum(p, axis=-1, keepdims=True)).astype(BF16))
    return jnp.concatenate(outs, axis=-1)


def _project_out(tok, mem_out, x_ref, wo_ref, o_ref):
    mix = tok.shape[1]
    o_ref[...] = (x_ref[...]
                  + jnp.dot(tok, wo_ref[:mix, :], preferred_element_type=F32)
                  + jnp.dot(mem_out, wo_ref[mix:, :], preferred_element_type=F32))


def _out_proj_kernel(tok_ref, qm_ref, kv_ref, x_ref, wo_ref, *refs):
    n_casts = (len(refs) - 1) // 2
    _project_out(tok_ref[...], _memory_attention(qm_ref, kv_ref), x_ref, wo_ref, refs[n_casts])
    _cast_blocks(refs[:n_casts], refs[n_casts + 1:])


def _conv_out_proj_kernel(gb_ref, gc_ref, hh_ref, gc_prev_ref, hh_prev_ref, cw_ref,
                          qm_ref, kv_ref, x_ref, wo_ref, *refs, tiles_per_seq):
    n_casts = (len(refs) - 1) // 2
    first = pl.program_id(0) % tiles_per_seq == 0
    u = gc_ref[...].astype(F32) * hh_ref[...].astype(F32)
    prev = gc_prev_ref[...].astype(F32) * hh_prev_ref[...].astype(F32)
    prev = jnp.where(first, 0.0, prev)
    tok = (gb_ref[...].astype(F32) * _causal_conv3(u, prev, cw_ref[...])).astype(BF16)
    _project_out(tok, _memory_attention(qm_ref, kv_ref), x_ref, wo_ref, refs[n_casts])
    _cast_blocks(refs[:n_casts], refs[n_casts + 1:])


def _out_proj(tok, proj, conv_w, kv, x, wo, *, seq, mem_tokens, tm, casts=()):
    m, d = x.shape
    mem_width = kv.shape[1] // 2
    mix = d - mem_width
    tiles_per_seq = seq // tm
    qm_block = proj.shape[1] // mem_width - 1
    cast_in, cast_out, cast_shapes = _cast_specs(casts, m // tm, lambda i: i)
    tail_specs = [
        pl.BlockSpec((tm, mem_width), lambda i: (i, qm_block)),
        pl.BlockSpec((mem_tokens, 2 * mem_width), lambda i: (i // tiles_per_seq, 0)),
        pl.BlockSpec((tm, d), lambda i: (i, 0)),
        pl.BlockSpec((d, d), lambda i: (0, 0), pipeline_mode=pl.Buffered(1)),
    ] + cast_in
    if tok is not None:
        kernel = _out_proj_kernel
        head_specs = [pl.BlockSpec((tm, mix), lambda i: (i, 0))]
        head_args = (tok,)
    else:
        kernel = functools.partial(_conv_out_proj_kernel, tiles_per_seq=tiles_per_seq)
        halo = tm // SUBLANES

        def prev_rows(part):
            return pl.BlockSpec((SUBLANES, mix), lambda i: (jnp.maximum(i * halo - 1, 0), part))

        head_specs = [pl.BlockSpec((tm, mix), lambda i: (i, 0)),
                      pl.BlockSpec((tm, mix), lambda i: (i, 1)),
                      pl.BlockSpec((tm, mix), lambda i: (i, 2)),
                      prev_rows(1), prev_rows(2),
                      pl.BlockSpec((CONV_WIDTH, mix), lambda i: (0, 0))]
        head_args = (proj, proj, proj, proj, proj, conv_w)
    out, *casted = pl.pallas_call(
        kernel,
        grid=(m // tm,),
        in_specs=head_specs + tail_specs,
        out_specs=[pl.BlockSpec((tm, d), lambda i: (i, 0))] + cast_out,
        out_shape=[jax.ShapeDtypeStruct((m, d), F32)] + cast_shapes,
        compiler_params=_params("parallel"),
        name="out_proj",
    )(*head_args, proj, kv, x, wo, *casts)
    return out, casted


def _conv_ffn_kernel(x_ref, g_ref, wg_ref, wu_ref, conv_ref, wd_ref,
                     fg_ref, o_ref, h_ref, carry_ref, up_ref, *,
                     tiles_per_seq, row_strips, final_norm):
    f = pl.program_id(1)
    nf = pl.num_programs(1)
    first = pl.program_id(0) % tiles_per_seq == 0
    tm = x_ref.shape[0]

    @pl.when(f == 0)
    def _():
        x = x_ref[...]
        h_ref[...] = _rms_norm(x, g_ref[...]).astype(BF16)
        o_ref[...] = x

    @pl.when(first)
    def _():
        carry_ref[f] = jnp.zeros(carry_ref.shape[1:], F32)

    strip = tm // row_strips
    for slot in range(2):
        up_ref[slot, :SUBLANES, :] = carry_ref[f, slot]
    for s in range(row_strips):
        for slot, w_ref in enumerate((wg_ref, wu_ref)):
            up_ref[slot, pl.ds(SUBLANES + s * strip, strip), :] = jnp.dot(
                h_ref[pl.ds(s * strip, strip), :], w_ref[...], preferred_element_type=F32)
    conv = [conv_ref[f + slot * nf] for slot in range(2)]
    for s in range(row_strips):
        branches = []
        for slot in range(2):
            taps = [up_ref[slot, pl.ds(SUBLANES + s * strip - (CONV_WIDTH - 1 - k), strip), :]
                    for k in range(CONV_WIDTH)]
            branches.append(conv[slot][0:1] * taps[0] + conv[slot][1:2] * taps[1]
                            + conv[slot][2:3] * taps[2] + conv[slot][CONV_WIDTH:CONV_WIDTH + 1])
        act = (_silu(branches[0]) * branches[1]).astype(BF16)
        o_ref[pl.ds(s * strip, strip), :] += jnp.dot(act, wd_ref[...], preferred_element_type=F32)
    for slot in range(2):
        carry_ref[f, slot] = up_ref[slot, pl.ds(tm, SUBLANES), :]

    if final_norm:
        @pl.when(f == pl.num_programs(1) - 1)
        def _():
            o_ref[...] = _rms_norm(o_ref[...], fg_ref[...])


def _conv_ffn(x, gain, w_up, conv_w, conv_b, w_down, final_gain, *, seq, tm, tf, final_norm):
    m, d = x.shape
    d_ff = w_down.shape[0]
    nf = d_ff // tf
    conv = jnp.concatenate(
        [conv_w.reshape(CONV_WIDTH, 2 * nf, tf).transpose(1, 0, 2),
         conv_b.reshape(2 * nf, 1, tf),
         jnp.zeros((2 * nf, SUBLANES - CONV_WIDTH - 1, tf), F32)], axis=1)
    return pl.pallas_call(
        functools.partial(_conv_ffn_kernel, tiles_per_seq=seq // tm, row_strips=FFN_ROW_STRIPS,
                          final_norm=final_norm),
        grid=(m // tm, nf),
        in_specs=[
            pl.BlockSpec((tm, d), lambda i, f: (i, 0)),
            pl.BlockSpec((1, d), lambda i, f: (0, 0)),
            pl.BlockSpec((d, tf), lambda i, f: (0, f)),
            pl.BlockSpec((d, tf), lambda i, f: (0, nf + f)),
            pl.BlockSpec((2 * nf, SUBLANES, tf), lambda i, f: (0, 0, 0)),
            pl.BlockSpec((tf, d), lambda i, f: (f, 0)),
            pl.BlockSpec((1, d), lambda i, f: (0, 0)),
        ],
        out_specs=pl.BlockSpec((tm, d), lambda i, f: (i, 0)),
        out_shape=jax.ShapeDtypeStruct((m, d), F32),
        scratch_shapes=[pltpu.VMEM((tm, d), BF16),
                        pltpu.VMEM((nf, 2, SUBLANES, tf), F32),
                        pltpu.VMEM((2, SUBLANES + tm, tf), F32)],
        compiler_params=_params("arbitrary", "arbitrary"),
        name="conv_ffn",
    )(x, gain.reshape(1, d), w_up, w_up, conv, w_down, final_gain.reshape(1, d))


MXU_WIDTH = 256

PROJ_ROWS = 512
FFN_ROWS = 1024
FFN_COLS = 512
FFN_ROW_STRIPS = 2
RET_HEADS_PER_STEP = 2
MOBA_HEADS_PER_STEP = 4
MOBA_GROUP = 4


def _half_tile(n):
    return n // 2 if n % (2 * MXU_WIDTH) == 0 else n


def _trunk(x, mem, mem_norm, layers, final_norm):
    batch, seq, d = x.shape
    mem_tokens = mem.shape[1]
    x = x.reshape(batch * seq, d)
    mem = mem.reshape(batch * mem_tokens, d)
    n_layers = len(layers)
    w_in = layers[0][1].astype(BF16)
    w_mem_kv = layers[0][3].astype(BF16)
    for i, (norm_mix, _, extra, _, w_o, norm_ffn, w_up, cw, cb, w_down) in enumerate(layers):
        kind = i % 3
        mem_width = w_mem_kv.shape[1] // 2
        mix = d - mem_width
        proj, (w_up, w_o) = _norm_matmul(x, norm_mix, w_in, tm=PROJ_ROWS,
                                         tn=_half_tile(w_in.shape[1]), casts=(w_up, w_o))
        kv, _ = _norm_matmul(mem, mem_norm, w_mem_kv, tm=mem_tokens, tn=w_mem_kv.shape[1])
        if kind == 0:
            tok = _retention(proj, extra, batch=batch, seq=seq, rows=seq, heads=RET_HEADS_PER_STEP)
        elif kind == 1:
            tok = None
        else:
            tok = _moba(proj, batch=batch, seq=seq, d=mix // MOBA_HEADS,
                        heads=MOBA_HEADS_PER_STEP, group=MOBA_GROUP)
        next_weights = (layers[i + 1][1], layers[i + 1][3]) if i + 1 < n_layers else ()
        x, (w_down, *next_weights) = _out_proj(tok, proj, extra, kv, x, w_o, seq=seq,
                                               mem_tokens=mem_tokens, tm=PROJ_ROWS,
                                               casts=(w_down,) + next_weights)
        if next_weights:
            w_in, w_mem_kv = next_weights
        x = _conv_ffn(x, norm_ffn, w_up, cw, cb, w_down, final_norm, seq=seq, tm=FFN_ROWS,
                      tf=FFN_COLS, final_norm=(i == n_layers - 1))
    return x.reshape(batch, seq, d)


def kernel(x, mem, mem_norm,
           l0_norm_mix, l0_w_in, l0_ret_gn, l0_w_mem_kv, l0_w_o, l0_norm_ffn, l0_ffn_w_up, l0_ffn_conv_w, l0_ffn_conv_b, l0_ffn_w_down,
           l1_norm_mix, l1_w_in, l1_conv_w, l1_w_mem_kv, l1_w_o, l1_norm_ffn, l1_ffn_w_up, l1_ffn_conv_w, l1_ffn_conv_b, l1_ffn_w_down,
           l2_norm_mix, l2_w_in, l2_w_mem_kv, l2_w_o, l2_norm_ffn, l2_ffn_w_up, l2_ffn_conv_w, l2_ffn_conv_b, l2_ffn_w_down,
           l3_norm_mix, l3_w_in, l3_ret_gn, l3_w_mem_kv, l3_w_o, l3_norm_ffn, l3_ffn_w_up, l3_ffn_conv_w, l3_ffn_conv_b, l3_ffn_w_down,
           final_norm):
    layers = [
        (l0_norm_mix, l0_w_in, l0_ret_gn, l0_w_mem_kv, l0_w_o, l0_norm_ffn, l0_ffn_w_up, l0_ffn_conv_w, l0_ffn_conv_b, l0_ffn_w_down),
        (l1_norm_mix, l1_w_in, l1_conv_w, l1_w_mem_kv, l1_w_o, l1_norm_ffn, l1_ffn_w_up, l1_ffn_conv_w, l1_ffn_conv_b, l1_ffn_w_down),
        (l2_norm_mix, l2_w_in, None, l2_w_mem_kv, l2_w_o, l2_norm_ffn, l2_ffn_w_up, l2_ffn_conv_w, l2_ffn_conv_b, l2_ffn_w_down),
        (l3_norm_mix, l3_w_in, l3_ret_gn, l3_w_mem_kv, l3_w_o, l3_norm_ffn, l3_ffn_w_up, l3_ffn_conv_w, l3_ffn_conv_b, l3_ffn_w_down),
    ]
    return _trunk(x, mem, mem_norm, layers, final_norm)
```

```python
import functools
import math

import jax
import jax.numpy as jnp
import numpy as np
from jax import lax
from jax.experimental import pallas as pl
from jax.experimental.pallas import tpu as pltpu

F32 = jnp.float32
BF16 = jnp.bfloat16

NORM_EPS = 1e-6
MEM_HEADS = 4
RET_HEADS = 6
RET_CHUNK = 256
CONV_WIDTH = 3
MOBA_HEADS = 12
MOBA_BLOCK = 256
MOBA_TOPK = 3

SUBLANES = 8
VMEM_LIMIT_BYTES = 56 * 1024 * 1024

_NT_DIMS = (((1,), (1,)), ((), ()))
_TN_DIMS = (((0,), (0,)), ((), ()))


def _params(*semantics):
    return pltpu.CompilerParams(dimension_semantics=semantics,
                                vmem_limit_bytes=VMEM_LIMIT_BYTES)


def _rms_norm(x, gain):
    return x * lax.rsqrt(jnp.mean(x * x, axis=-1, keepdims=True) + NORM_EPS) * gain


def _silu(g):
    return g * (1.0 / (1.0 + jnp.exp(-g)))


def _shift_rows(u, prev, k):
    rolled = pltpu.roll(u, k, axis=0)
    prev_rolled = pltpu.roll(prev, k, axis=0)
    row = lax.broadcasted_iota(jnp.int32, prev.shape, 0)
    head = jnp.where(row < k, prev_rolled, rolled[:SUBLANES])
    return jnp.concatenate([head, rolled[SUBLANES:]], axis=0)


def _causal_conv3(u, prev, w):
    return (w[0:1] * _shift_rows(u, prev, 2) + w[1:2] * _shift_rows(u, prev, 1)
            + w[2:3] * u)


BF16_ROWS = 16


def _cast_specs(weights, n_steps, step_index):
    in_specs, out_specs, out_shapes = [], [], []
    for w in weights:
        rows, cols = w.shape
        assert rows % (n_steps * BF16_ROWS) == 0
        block = (rows // n_steps, cols)
        spec = pl.BlockSpec(block, lambda *idx: (step_index(*idx), 0))
        in_specs.append(spec)
        out_specs.append(spec)
        out_shapes.append(jax.ShapeDtypeStruct(w.shape, BF16))
    return in_specs, out_specs, out_shapes


def _cast_blocks(src_refs, dst_refs):
    for src, dst in zip(src_refs, dst_refs):
        dst[...] = src[...].astype(BF16)


def _norm_matmul_kernel(x_ref, g_ref, w_ref, *refs):
    n_casts = (len(refs) - 1) // 2
    o_ref = refs[n_casts]
    h = _rms_norm(x_ref[...], g_ref[...]).astype(BF16)
    o_ref[...] = jnp.dot(h, w_ref[...], preferred_element_type=F32).astype(o_ref.dtype)
    _cast_blocks(refs[:n_casts], refs[n_casts + 1:])


def _norm_matmul(x, gain, w, *, tm, tn, casts=()):
    m, d = x.shape
    n = w.shape[1]
    ni = m // tm
    cast_in, cast_out, cast_shapes = _cast_specs(casts, (n // tn) * ni, lambda j, i: j * ni + i)
    out, *casted = pl.pallas_call(
        _norm_matmul_kernel,
        grid=(n // tn, ni),
        in_specs=[
            pl.BlockSpec((tm, d), lambda j, i: (i, 0)),
            pl.BlockSpec((1, d), lambda j, i: (0, 0)),
            pl.BlockSpec((d, tn), lambda j, i: (0, j)),
        ] + cast_in,
        out_specs=[pl.BlockSpec((tm, tn), lambda j, i: (i, j))] + cast_out,
        out_shape=[jax.ShapeDtypeStruct((m, n), BF16)] + cast_shapes,
        compiler_params=_params("parallel", "parallel"),
        name="norm_matmul",
    )(x, gain.reshape(1, d), w, *casts)
    return out, casted


def _retention_kernel(lg_ref, q_ref, k_ref, v_ref, g_ref, gn_ref, o_ref, state_ref, *,
                      chunk, heads):
    rows = q_ref.shape[0]
    d = q_ref.shape[1] // heads
    c = chunk
    k_scale = d ** -0.5

    @pl.when(pl.program_id(2) == 0)
    def _():
        state_ref[...] = jnp.zeros_like(state_ref)

    diff = (lax.broadcasted_iota(jnp.int32, (c, c), 0)
            - lax.broadcasted_iota(jnp.int32, (c, c), 1)).astype(F32)
    pos = lax.broadcasted_iota(jnp.int32, (c, d), 0).astype(F32)
    tables = []
    for t in range(heads):
        lg = lg_ref[pl.program_id(1) * heads + t]
        intra = jnp.where(diff >= 0, jnp.exp(jnp.maximum(diff, 0.0) * lg), 0.0) * k_scale
        q_decay = jnp.exp((pos + 1.0) * lg)
        k_decay = jnp.exp((c - 1.0 - pos) * lg) * k_scale
        chunk_decay = jnp.exp(jnp.zeros((1, d), F32) + c * lg)
        tables.append((intra, q_decay, k_decay, chunk_decay))

    for s in range(rows // c):
        sl = pl.ds(s * c, c)
        for t, (intra, q_decay, k_decay, chunk_decay) in enumerate(tables):
            cols = slice(t * d, (t + 1) * d)
            q = q_ref[sl, cols]
            k = k_ref[sl, cols]
            v = v_ref[sl, cols]
            scores = lax.dot_general(q, k, _NT_DIMS, preferred_element_type=F32) * intra
            y = jnp.dot(scores.astype(BF16), v, preferred_element_type=F32)
            state = state_ref[t]
            y = y + jnp.dot(q, state.astype(BF16), preferred_element_type=F32) * q_decay
            kd = (k.astype(F32) * k_decay).astype(BF16)
            kv = lax.dot_general(kd, v, _TN_DIMS, preferred_element_type=F32)
            state_ref[t] = state * chunk_decay + kv
            g = g_ref[sl, cols].astype(F32)
            o_ref[sl, cols] = (_silu(g) * _rms_norm(y, gn_ref[:, cols])).astype(o_ref.dtype)


def _retention(proj, gn_gain, *, batch, seq, rows, heads):
    m = proj.shape[0]
    d = gn_gain.shape[0] // RET_HEADS
    nt = seq // rows
    groups = RET_HEADS // heads
    lg = np.log1p(-np.exp2(-5.0 - np.arange(RET_HEADS))).astype(np.float32)

    def spec(part):
        return pl.BlockSpec((rows, heads * d),
                            lambda b, h, t, lg_ref: (b * nt + t, part * groups + h))

    return pl.pallas_call(
        functools.partial(_retention_kernel, chunk=RET_CHUNK, heads=heads),
        grid_spec=pltpu.PrefetchScalarGridSpec(
            num_scalar_prefetch=1,
            grid=(batch, groups, nt),
            in_specs=[spec(0), spec(1), spec(2), spec(3),
                      pl.BlockSpec((1, heads * d), lambda b, h, t, lg_ref: (0, h))],
            out_specs=pl.BlockSpec((rows, heads * d), lambda b, h, t, lg_ref: (b * nt + t, h)),
            scratch_shapes=[pltpu.VMEM((heads, d, d), F32)],
        ),
        out_shape=jax.ShapeDtypeStruct((m, RET_HEADS * d), BF16),
        compiler_params=_params("parallel", "parallel", "arbitrary"),
        name="retention",
    )(jnp.asarray(lg), proj, proj, proj, proj, gn_gain.reshape(1, -1))


def _alibi_slopes(n):
    def pow2_slopes(m):
        start = 2.0 ** (-8.0 / m)
        return [start ** (i + 1) for i in range(m)]
    if math.log2(n).is_integer():
        s = pow2_slopes(n)
    else:
        c = 2 ** int(math.floor(math.log2(n)))
        s = pow2_slopes(c) + list(_alibi_slopes(2 * c))[0::2][: n - c]
    return np.asarray(s, dtype=np.float32)


MOBA_MASKED = -1e30
AUG_PARTS = 3
AUG_PART_LANE = SUBLANES


def _split_bf16(x):
    parts = []
    for _ in range(AUG_PARTS):
        part = x.astype(BF16).astype(F32)
        parts.append(part)
        x = x - part
    return parts


def _moba_kernel(slope_ref, q_ref, k_ref, v_ref, o_ref, kmean_ref, vt_ref, kaug_ref, s_ref, *,
                 heads, group):
    qi = pl.program_id(2)
    blk = q_ref.shape[0]
    d = q_ref.shape[1] // heads
    nblk = k_ref.shape[0] // blk
    scale = d ** -0.5
    exp2_scale = scale * math.log2(math.e)
    assert AUG_PART_LANE + AUG_PARTS * nblk <= d

    def head_cols(t):
        return slice(t * d, (t + 1) * d)

    def head_slope(t):
        return slope_ref[pl.program_id(0) * heads + t]

    @pl.when((qi == 0) & (pl.program_id(1) == 0))
    def _():
        lane = lax.broadcasted_iota(jnp.int32, (blk, d), 1)
        row = lax.broadcasted_iota(jnp.int32, (blk, d), 0)
        for t in range(heads):
            key_scale = head_slope(t) * (1.0 / scale)
            for j in range(nblk):
                aug = jnp.zeros((blk, d), F32)
                for p, part in enumerate(_split_bf16((row + j * blk).astype(F32) * key_scale)):
                    aug = jnp.where(lane == p, part, aug)
                    aug = jnp.where(lane == AUG_PART_LANE + p * nblk + j, 1.0, aug)
                kaug_ref[t, j * blk:(j + 1) * blk, :] = aug.astype(BF16)

    @pl.when(qi == 0)
    def _():
        ones_row = jnp.where(lax.broadcasted_iota(jnp.int32, (BF16_ROWS, blk), 0) == 0, 1.0, 0.0)
        for t in range(heads):
            for j in range(nblk):
                rows = slice(j * blk, (j + 1) * blk)
                kmean_ref[t, j:j + 1, :] = jnp.sum(k_ref[rows, head_cols(t)].astype(F32), axis=0,
                                                   keepdims=True) * (1.0 / blk)
                vt_ref[t, j] = jnp.concatenate(
                    [v_ref[rows, head_cols(t)].astype(F32).T, ones_row], axis=0).astype(BF16)

    def query_operand(t):
        q = q_ref[:, head_cols(t)]
        kmean = kmean_ref[t]
        kmean_hi = kmean.astype(BF16)
        kmean_lo = (kmean - kmean_hi.astype(F32)).astype(BF16)
        gate = (lax.dot_general(kmean_hi, q, _NT_DIMS, preferred_element_type=F32)
                + lax.dot_general(kmean_lo, q, _NT_DIMS, preferred_element_type=F32))
        blk_id = lax.broadcasted_iota(jnp.int32, gate.shape, 0)
        rank = jnp.zeros(gate.shape, jnp.int32)
        for other in range(nblk):
            g_other = gate[other:other + 1, :]
            beats = (g_other > gate) | ((g_other == gate) & (other < blk_id))
            rank = rank + jnp.where(other < qi, jnp.where(beats, 1, 0), 0)
        keep = ((blk_id < qi) & (rank < MOBA_TOPK)) | (blk_id == qi)
        t_query = (qi * blk + lax.broadcasted_iota(jnp.int32, gate.shape, 1)).astype(F32)
        bias = (jnp.where(keep, 0.0, MOBA_MASKED) - head_slope(t) * t_query) * (1.0 / scale)
        ones = jnp.where(lax.broadcasted_iota(jnp.int32, (AUG_PART_LANE, blk), 0) < AUG_PARTS,
                         1.0, 0.0)
        pad = jnp.zeros((d - AUG_PART_LANE - AUG_PARTS * nblk, blk), F32)
        aug = jnp.concatenate([ones] + _split_bf16(bias) + [pad], axis=0)
        return jnp.concatenate([q, aug.T.astype(BF16)], axis=1)

    q_ops = [query_operand(t) for t in range(heads)]

    own_group = qi // group
    key_after_query = (lax.broadcasted_iota(jnp.int32, (blk, blk), 0)
                       > lax.broadcasted_iota(jnp.int32, (blk, blk), 1))

    def group_scores(t, g, n_blocks, own):
        rows = pl.ds(pl.multiple_of(g * group * blk, group * blk), n_blocks * blk)
        k_op = jnp.concatenate([k_ref[rows, head_cols(t)], kaug_ref[t, rows, :]], axis=1)
        group_s = lax.dot_general(k_op, q_ops[t], _NT_DIMS, preferred_element_type=F32)
        best = None
        for i in range(n_blocks):
            s = group_s[i * blk:(i + 1) * blk] * exp2_scale
            if own and i == n_blocks - 1:
                s = jnp.where(key_after_query, MOBA_MASKED, s)
            s_ref[t, g, i] = s
            block_max = jnp.max(s, axis=0, keepdims=True)
            best = block_max if best is None else jnp.maximum(best, block_max)
        return best

    def group_values(t, g, n_blocks, best, acc):
        p = jnp.concatenate([jnp.exp2(s_ref[t, g, i] - best).astype(BF16)
                             for i in range(n_blocks)], axis=0)
        vt = jnp.concatenate([vt_ref[t, g * group + i] for i in range(n_blocks)], axis=1)
        return acc + jnp.dot(vt, p, preferred_element_type=F32)

    def past_scores(g, best):
        return tuple(jnp.maximum(best[t], group_scores(t, g, group, False)) for t in range(heads))

    own_blocks = qi % group
    best = lax.switch(own_blocks, [
        functools.partial(lambda n: tuple(group_scores(t, own_group, n, True)
                                          for t in range(heads)), n + 1)
        for n in range(group)])
    best = lax.fori_loop(0, own_group, past_scores, best)

    def past_values(g, accs):
        return tuple(group_values(t, g, group, best[t], accs[t]) for t in range(heads))

    zero = jnp.zeros((d + BF16_ROWS, blk), F32)
    totals = lax.fori_loop(0, own_group, past_values, (zero,) * heads)
    totals = lax.switch(own_blocks, [
        functools.partial(lambda n, accs: tuple(group_values(t, own_group, n, best[t], accs[t])
                                                for t in range(heads)), n + 1)
        for n in range(group)], totals)
    for t in range(heads):
        acc = totals[t]
        o_ref[:, head_cols(t)] = (acc[:d] / acc[d:d + 1]).T.astype(o_ref.dtype)


def _moba(proj, *, batch, seq, d, heads, group):
    m = proj.shape[0]
    blk = MOBA_BLOCK
    assert seq % (group * blk) == 0 and MOBA_HEADS % heads == 0
    nq = seq // blk
    groups = MOBA_HEADS // heads
    slopes = _alibi_slopes(MOBA_HEADS)
    return pl.pallas_call(
        functools.partial(_moba_kernel, heads=heads, group=group),
        grid_spec=pltpu.PrefetchScalarGridSpec(
            num_scalar_prefetch=1,
            grid=(groups, batch, nq),
            in_specs=[
                pl.BlockSpec((blk, heads * d), lambda h, b, i, s_ref: (b * nq + i, h)),
                pl.BlockSpec((seq, heads * d), lambda h, b, i, s_ref: (b, groups + h),
                             pipeline_mode=pl.Buffered(1)),
                pl.BlockSpec((seq, heads * d), lambda h, b, i, s_ref: (b, 2 * groups + h),
                             pipeline_mode=pl.Buffered(1)),
            ],
            out_specs=pl.BlockSpec((blk, heads * d), lambda h, b, i, s_ref: (b * nq + i, h)),
            scratch_shapes=[pltpu.VMEM((heads, nq, d), F32),
                            pltpu.VMEM((heads, nq, d + BF16_ROWS, blk), BF16),
                            pltpu.VMEM((heads, seq, d), BF16),
                            pltpu.VMEM((heads, nq // group, group, blk, blk), F32)],
        ),
        out_shape=jax.ShapeDtypeStruct((m, MOBA_HEADS * d), BF16),
        compiler_params=_params("parallel", "arbitrary", "arbitrary"),
        name="moba",
    )(jnp.asarray(slopes), proj, proj, proj)


def _memory_attention(qm_ref, kv_ref):
    width = qm_ref.shape[1]
    dh = width // MEM_HEADS
    outs = []
    for h in range(MEM_HEADS):
        q = qm_ref[:, h * dh:(h + 1) * dh]
        k = kv_ref[:, h * dh:(h + 1) * dh]
        v = kv_ref[:, width + h * dh:width + (h + 1) * dh]
        s = lax.dot_general(q, k, _NT_DIMS, preferred_element_type=F32) * dh ** -0.5
        p = jnp.exp(s - jnp.max(s, axis=-1, keepdims=True))
        o = jnp.dot(p.astype(BF16), v, preferred_element_type=F32)
        outs.append((o / jnp.sum(p, axis=-1, keepdims=True)).astype(BF16))
    return jnp.concatenate(outs, axis=-1)


def _project_out(tok, mem_out, x_ref, wo_ref, o_ref):
    mix = tok.shape[1]
    o_ref[...] = (x_ref[...]
                  + jnp.dot(tok, wo_ref[:mix, :], preferred_element_type=F32)
                  + jnp.dot(mem_out, wo_ref[mix:, :], preferred_element_type=F32))


def _out_proj_kernel(tok_ref, qm_ref, kv_ref, x_ref, wo_ref, *refs):
    n_casts = (len(refs) - 1) // 2
    _project_out(tok_ref[...], _memory_attention(qm_ref, kv_ref), x_ref, wo_ref, refs[n_casts])
    _cast_blocks(refs[:n_casts], refs[n_casts + 1:])


def _conv_out_proj_kernel(gb_ref, gc_ref, hh_ref, gc_prev_ref, hh_prev_ref, cw_ref,
                          qm_ref, kv_ref, x_ref, wo_ref, *refs, tiles_per_seq):
    n_casts = (len(refs) - 1) // 2
    first = pl.program_id(0) % tiles_per_seq == 0
    u = gc_ref[...].astype(F32) * hh_ref[...].astype(F32)
    prev = gc_prev_ref[...].astype(F32) * hh_prev_ref[...].astype(F32)
    prev = jnp.where(first, 0.0, prev)
    tok = (gb_ref[...].astype(F32) * _causal_conv3(u, prev, cw_ref[...])).astype(BF16)
    _project_out(tok, _memory_attention(qm_ref, kv_ref), x_ref, wo_ref, refs[n_casts])
    _cast_blocks(refs[:n_casts], refs[n_casts + 1:])


def _out_proj(tok, proj, conv_w, kv, x, wo, *, seq, mem_tokens, tm, casts=()):
    m, d = x.shape
    mem_width = kv.shape[1] // 2
    mix = d - mem_width
    tiles_per_seq = seq // tm
    qm_block = proj.shape[1] // mem_width - 1
    cast_in, cast_out, cast_shapes = _cast_specs(casts, m // tm, lambda i: i)
    tail_specs = [
        pl.BlockSpec((tm, mem_width), lambda i: (i, qm_block)),
        pl.BlockSpec((mem_tokens, 2 * mem_width), lambda i: (i // tiles_per_seq, 0)),
        pl.BlockSpec((tm, d), lambda i: (i, 0)),
        pl.BlockSpec((d, d), lambda i: (0, 0), pipeline_mode=pl.Buffered(1)),
    ] + cast_in
    if tok is not None:
        kernel = _out_proj_kernel
        head_specs = [pl.BlockSpec((tm, mix), lambda i: (i, 0))]
        head_args = (tok,)
    else:
        kernel = functools.partial(_conv_out_proj_kernel, tiles_per_seq=tiles_per_seq)
        halo = tm // SUBLANES

        def prev_rows(part):
            return pl.BlockSpec((SUBLANES, mix), lambda i: (jnp.maximum(i * halo - 1, 0), part))

        head_specs = [pl.BlockSpec((tm, mix), lambda i: (i, 0)),
                      pl.BlockSpec((tm, mix), lambda i: (i, 1)),
                      pl.BlockSpec((tm, mix), lambda i: (i, 2)),
                      prev_rows(1), prev_rows(2),
                      pl.BlockSpec((CONV_WIDTH, mix), lambda i: (0, 0))]
        head_args = (proj, proj, proj, proj, proj, conv_w)
    out, *casted = pl.pallas_call(
        kernel,
        grid=(m // tm,),
        in_specs=head_specs + tail_specs,
        out_specs=[pl.BlockSpec((tm, d), lambda i: (i, 0))] + cast_out,
        out_shape=[jax.ShapeDtypeStruct((m, d), F32)] + cast_shapes,
        compiler_params=_params("parallel"),
        name="out_proj",
    )(*head_args, proj, kv, x, wo, *casts)
    return out, casted


def _conv_ffn_kernel(x_ref, g_ref, wg_ref, wu_ref, conv_ref, wd_ref,
                     fg_ref, o_ref, h_ref, carry_ref, up_ref, *,
                     tiles_per_seq, row_strips, final_norm):
    f = pl.program_id(1)
    nf = pl.num_programs(1)
    first = pl.program_id(0) % tiles_per_seq == 0
    tm = x_ref.shape[0]

    @pl.when(f == 0)
    def _():
        x = x_ref[...]
        h_ref[...] = _rms_norm(x, g_ref[...]).astype(BF16)
        o_ref[...] = x

    @pl.when(first)
    def _():
        carry_ref[f] = jnp.zeros(carry_ref.shape[1:], F32)

    strip = tm // row_strips
    for slot in range(2):
        up_ref[slot, :SUBLANES, :] = carry_ref[f, slot]
    for s in range(row_strips):
        for slot, w_ref in enumerate((wg_ref, wu_ref)):
            up_ref[slot, pl.ds(SUBLANES + s * strip, strip), :] = jnp.dot(
                h_ref[pl.ds(s * strip, strip), :], w_ref[...], preferred_element_type=F32)
    conv = [conv_ref[f + slot * nf] for slot in range(2)]
    for s in range(row_strips):
        branches = []
        for slot in range(2):
            taps = [up_ref[slot, pl.ds(SUBLANES + s * strip - (CONV_WIDTH - 1 - k), strip), :]
                    for k in range(CONV_WIDTH)]
            branches.append(conv[slot][0:1] * taps[0] + conv[slot][1:2] * taps[1]
                            + conv[slot][2:3] * taps[2] + conv[slot][CONV_WIDTH:CONV_WIDTH + 1])
        act = (_silu(branches[0]) * branches[1]).astype(BF16)
        o_ref[pl.ds(s * strip, strip), :] += jnp.dot(act, wd_ref[...], preferred_element_type=F32)
    for slot in range(2):
        carry_ref[f, slot] = up_ref[slot, pl.ds(tm, SUBLANES), :]

    if final_norm:
        @pl.when(f == pl.num_programs(1) - 1)
        def _():
            o_ref[...] = _rms_norm(o_ref[...], fg_ref[...])


def _conv_ffn(x, gain, w_up, conv_w, conv_b, w_down, final_gain, *, seq, tm, tf, final_norm):
    m, d = x.shape
    d_ff = w_down.shape[0]
    nf = d_ff // tf
    conv = jnp.concatenate(
        [conv_w.reshape(CONV_WIDTH, 2 * nf, tf).transpose(1, 0, 2),
         conv_b.reshape(2 * nf, 1, tf),
         jnp.zeros((2 * nf, SUBLANES - CONV_WIDTH - 1, tf), F32)], axis=1)
    return pl.pallas_call(
        functools.partial(_conv_ffn_kernel, tiles_per_seq=seq // tm, row_strips=FFN_ROW_STRIPS,
                          final_norm=final_norm),
        grid=(m // tm, nf),
        in_specs=[
            pl.BlockSpec((tm, d), lambda i, f: (i, 0)),
            pl.BlockSpec((1, d), lambda i, f: (0, 0)),
            pl.BlockSpec((d, tf), lambda i, f: (0, f)),
            pl.BlockSpec((d, tf), lambda i, f: (0, nf + f)),
            pl.BlockSpec((2 * nf, SUBLANES, tf), lambda i, f: (0, 0, 0)),
            pl.BlockSpec((tf, d), lambda i, f: (f, 0)),
            pl.BlockSpec((1, d), lambda i, f: (0, 0)),
        ],
        out_specs=pl.BlockSpec((tm, d), lambda i, f: (i, 0)),
        out_shape=jax.ShapeDtypeStruct((m, d), F32),
        scratch_shapes=[pltpu.VMEM((tm, d), BF16),
                        pltpu.VMEM((nf, 2, SUBLANES, tf), F32),
                        pltpu.VMEM((2, SUBLANES + tm, tf), F32)],
        compiler_params=_params("arbitrary", "arbitrary"),
        name="conv_ffn",
    )(x, gain.reshape(1, d), w_up, w_up, conv, w_down, final_gain.reshape(1, d))


MXU_WIDTH = 256

PROJ_ROWS = 512
FFN_ROWS = 1024
FFN_COLS = 512
FFN_ROW_STRIPS = 2
RET_HEADS_PER_STEP = 2
MOBA_HEADS_PER_STEP = 6
MOBA_GROUP = 4


def _half_tile(n):
    return n // 2 if n % (2 * MXU_WIDTH) == 0 else n


def _trunk(x, mem, mem_norm, layers, final_norm):
    batch, seq, d = x.shape
    mem_tokens = mem.shape[1]
    x = x.reshape(batch * seq, d)
    mem = mem.reshape(batch * mem_tokens, d)
    n_layers = len(layers)
    w_in = layers[0][1].astype(BF16)
    w_mem_kv = layers[0][3].astype(BF16)
    for i, (norm_mix, _, extra, _, w_o, norm_ffn, w_up, cw, cb, w_down) in enumerate(layers):
        kind = i % 3
        mem_width = w_mem_kv.shape[1] // 2
        mix = d - mem_width
        proj, (w_up, w_o) = _norm_matmul(x, norm_mix, w_in, tm=PROJ_ROWS,
                                         tn=_half_tile(w_in.shape[1]), casts=(w_up, w_o))
        kv, _ = _norm_matmul(mem, mem_norm, w_mem_kv, tm=mem_tokens, tn=w_mem_kv.shape[1])
        if kind == 0:
            tok = _retention(proj, extra, batch=batch, seq=seq, rows=seq, heads=RET_HEADS_PER_STEP)
        elif kind == 1:
            tok = None
        else:
            tok = _moba(proj, batch=batch, seq=seq, d=mix // MOBA_HEADS,
                        heads=MOBA_HEADS_PER_STEP, group=MOBA_GROUP)
        next_weights = (layers[i + 1][1], layers[i + 1][3]) if i + 1 < n_layers else ()
        x, (w_down, *next_weights) = _out_proj(tok, proj, extra, kv, x, w_o, seq=seq,
                                               mem_tokens=mem_tokens, tm=PROJ_ROWS,
                                               casts=(w_down,) + next_weights)
        if next_weights:
            w_in, w_mem_kv = next_weights
        x = _conv_ffn(x, norm_ffn, w_up, cw, cb, w_down, final_norm, seq=seq, tm=FFN_ROWS,
                      tf=FFN_COLS, final_norm=(i == n_layers - 1))
    return x.reshape(batch, seq, d)


def kernel(x, mem, mem_norm,
           l0_norm_mix, l0_w_in, l0_ret_gn, l0_w_mem_kv, l0_w_o, l0_norm_ffn, l0_ffn_w_up, l0_ffn_conv_w, l0_ffn_conv_b, l0_ffn_w_down,
           l1_norm_mix, l1_w_in, l1_conv_w, l1_w_mem_kv, l1_w_o, l1_norm_ffn, l1_ffn_w_up, l1_ffn_conv_w, l1_ffn_conv_b, l1_ffn_w_down,
           l2_norm_mix, l2_w_in, l2_w_mem_kv, l2_w_o, l2_norm_ffn, l2_ffn_w_up, l2_ffn_conv_w, l2_ffn_conv_b, l2_ffn_w_down,
           l3_norm_mix, l3_w_in, l3_ret_gn, l3_w_mem_kv, l3_w_o, l3_norm_ffn, l3_ffn_w_up, l3_ffn_conv_w, l3_ffn_conv_b, l3_ffn_w_down,
           final_norm):
    layers = [
        (l0_norm_mix, l0_w_in, l0_ret_gn, l0_w_mem_kv, l0_w_o, l0_norm_ffn, l0_ffn_w_up, l0_ffn_conv_w, l0_ffn_conv_b, l0_ffn_w_down),
        (l1_norm_mix, l1_w_in, l1_conv_w, l1_w_mem_kv, l1_w_o, l1_norm_ffn, l1_ffn_w_up, l1_ffn_conv_w, l1_ffn_conv_b, l1_ffn_w_down),
        (l2_norm_mix, l2_w_in, None, l2_w_mem_kv, l2_w_o, l2_norm_ffn, l2_ffn_w_up, l2_ffn_conv_w, l2_ffn_conv_b, l2_ffn_w_down),
        (l3_norm_mix, l3_w_in, l3_ret_gn, l3_w_mem_kv, l3_w_o, l3_norm_ffn, l3_ffn_w_up, l3_ffn_conv_w, l3_ffn_conv_b, l3_ffn_w_down),
    ]
    return _trunk(x, mem, mem_norm, layers, final_norm)
```

```python
import functools
import math

import jax
import jax.numpy as jnp
import numpy as np
from jax import lax
from jax.experimental import pallas as pl
from jax.experimental.pallas import tpu as pltpu

F32 = jnp.float32
BF16 = jnp.bfloat16

NORM_EPS = 1e-6
MEM_HEADS = 4
RET_HEADS = 6
RET_CHUNK = 256
CONV_WIDTH = 3
MOBA_HEADS = 12
MOBA_BLOCK = 256
MOBA_TOPK = 3

SUBLANES = 8
VMEM_LIMIT_BYTES = 56 * 1024 * 1024

_NT_DIMS = (((1,), (1,)), ((), ()))
_TN_DIMS = (((0,), (0,)), ((), ()))


def _params(*semantics):
    return pltpu.CompilerParams(dimension_semantics=semantics,
                                vmem_limit_bytes=VMEM_LIMIT_BYTES)


def _rms_norm(x, gain):
    return x * lax.rsqrt(jnp.mean(x * x, axis=-1, keepdims=True) + NORM_EPS) * gain


def _silu(g):
    return g * (1.0 / (1.0 + jnp.exp(-g)))


def _shift_rows(u, prev, k):
    rolled = pltpu.roll(u, k, axis=0)
    prev_rolled = pltpu.roll(prev, k, axis=0)
    row = lax.broadcasted_iota(jnp.int32, prev.shape, 0)
    head = jnp.where(row < k, prev_rolled, rolled[:SUBLANES])
    return jnp.concatenate([head, rolled[SUBLANES:]], axis=0)


def _causal_conv3(u, prev, w):
    return (w[0:1] * _shift_rows(u, prev, 2) + w[1:2] * _shift_rows(u, prev, 1)
            + w[2:3] * u)


BF16_ROWS = 16


def _cast_specs(weights, n_steps, step_index):
    in_specs, out_specs, out_shapes = [], [], []
    for w in weights:
        rows, cols = w.shape
        assert rows % (n_steps * BF16_ROWS) == 0
        block = (rows // n_steps, cols)
        spec = pl.BlockSpec(block, lambda *idx: (step_index(*idx), 0))
        in_specs.append(spec)
        out_specs.append(spec)
        out_shapes.append(jax.ShapeDtypeStruct(w.shape, BF16))
    return in_specs, out_specs, out_shapes


def _cast_blocks(src_refs, dst_refs):
    for src, dst in zip(src_refs, dst_refs):
        dst[...] = src[...].astype(BF16)


def _norm_matmul_kernel(x_ref, g_ref, w_ref, *refs):
    n_casts = (len(refs) - 1) // 2
    o_ref = refs[n_casts]
    h = _rms_norm(x_ref[...], g_ref[...]).astype(BF16)
    o_ref[...] = jnp.dot(h, w_ref[...], preferred_element_type=F32).astype(o_ref.dtype)
    _cast_blocks(refs[:n_casts], refs[n_casts + 1:])


def _norm_matmul(x, gain, w, *, tm, tn, casts=()):
    m, d = x.shape
    n = w.shape[1]
    ni = m // tm
    cast_in, cast_out, cast_shapes = _cast_specs(casts, (n // tn) * ni, lambda j, i: j * ni + i)
    out, *casted = pl.pallas_call(
        _norm_matmul_kernel,
        grid=(n // tn, ni),
        in_specs=[
            pl.BlockSpec((tm, d), lambda j, i: (i, 0)),
            pl.BlockSpec((1, d), lambda j, i: (0, 0)),
            pl.BlockSpec((d, tn), lambda j, i: (0, j)),
        ] + cast_in,
        out_specs=[pl.BlockSpec((tm, tn), lambda j, i: (i, j))] + cast_out,
        out_shape=[jax.ShapeDtypeStruct((m, n), BF16)] + cast_shapes,
        compiler_params=_params("parallel", "parallel"),
        name="norm_matmul",
    )(x, gain.reshape(1, d), w, *casts)
    return out, casted


def _retention_kernel(lg_ref, q_ref, k_ref, v_ref, g_ref, gn_ref, o_ref, state_ref, *,
                      chunk, heads):
    rows = q_ref.shape[0]
    d = q_ref.shape[1] // heads
    c = chunk
    k_scale = d ** -0.5

    @pl.when(pl.program_id(2) == 0)
    def _():
        state_ref[...] = jnp.zeros_like(state_ref)

    diff = (lax.broadcasted_iota(jnp.int32, (c, c), 0)
            - lax.broadcasted_iota(jnp.int32, (c, c), 1)).astype(F32)
    pos = lax.broadcasted_iota(jnp.int32, (c, d), 0).astype(F32)
    tables = []
    for t in range(heads):
        lg = lg_ref[pl.program_id(1) * heads + t]
        intra = jnp.where(diff >= 0, jnp.exp(jnp.maximum(diff, 0.0) * lg), 0.0) * k_scale
        q_decay = jnp.exp((pos + 1.0) * lg)
        k_decay = jnp.exp((c - 1.0 - pos) * lg) * k_scale
        chunk_decay = jnp.exp(jnp.zeros((1, d), F32) + c * lg)
        tables.append((intra, q_decay, k_decay, chunk_decay))

    for s in range(rows // c):
        sl = pl.ds(s * c, c)
        for t, (intra, q_decay, k_decay, chunk_decay) in enumerate(tables):
            cols = slice(t * d, (t + 1) * d)
            q = q_ref[sl, cols]
            k = k_ref[sl, cols]
            v = v_ref[sl, cols]
            scores = lax.dot_general(q, k, _NT_DIMS, preferred_element_type=F32) * intra
            y = jnp.dot(scores.astype(BF16), v, preferred_element_type=F32)
            state = state_ref[t]
            y = y + jnp.dot(q, state.astype(BF16), preferred_element_type=F32) * q_decay
            kd = (k.astype(F32) * k_decay).astype(BF16)
            kv = lax.dot_general(kd, v, _TN_DIMS, preferred_element_type=F32)
            state_ref[t] = state * chunk_decay + kv
            g = g_ref[sl, cols].astype(F32)
            o_ref[sl, cols] = (_silu(g) * _rms_norm(y, gn_ref[:, cols])).astype(o_ref.dtype)


def _retention(proj, gn_gain, *, batch, seq, rows, heads):
    m = proj.shape[0]
    d = gn_gain.shape[0] // RET_HEADS
    nt = seq // rows
    groups = RET_HEADS // heads
    lg = np.log1p(-np.exp2(-5.0 - np.arange(RET_HEADS))).astype(np.float32)

    def spec(part):
        return pl.BlockSpec((rows, heads * d),
                            lambda b, h, t, lg_ref: (b * nt + t, part * groups + h))

    return pl.pallas_call(
        functools.partial(_retention_kernel, chunk=RET_CHUNK, heads=heads),
        grid_spec=pltpu.PrefetchScalarGridSpec(
            num_scalar_prefetch=1,
            grid=(batch, groups, nt),
            in_specs=[spec(0), spec(1), spec(2), spec(3),
                      pl.BlockSpec((1, heads * d), lambda b, h, t, lg_ref: (0, h))],
            out_specs=pl.BlockSpec((rows, heads * d), lambda b, h, t, lg_ref: (b * nt + t, h)),
            scratch_shapes=[pltpu.VMEM((heads, d, d), F32)],
        ),
        out_shape=jax.ShapeDtypeStruct((m, RET_HEADS * d), BF16),
        compiler_params=_params("parallel", "parallel", "arbitrary"),
        name="retention",
    )(jnp.asarray(lg), proj, proj, proj, proj, gn_gain.reshape(1, -1))


def _alibi_slopes(n):
    def pow2_slopes(m):
        start = 2.0 ** (-8.0 / m)
        return [start ** (i + 1) for i in range(m)]
    if math.log2(n).is_integer():
        s = pow2_slopes(n)
    else:
        c = 2 ** int(math.floor(math.log2(n)))
        s = pow2_slopes(c) + list(_alibi_slopes(2 * c))[0::2][: n - c]
    return np.asarray(s, dtype=np.float32)


MOBA_MASKED = -1e30
AUG_PARTS = 3
AUG_PART_LANE = SUBLANES


def _split_bf16(x):
    parts = []
    for _ in range(AUG_PARTS):
        part = x.astype(BF16).astype(F32)
        parts.append(part)
        x = x - part
    return parts


def _moba_kernel(slope_ref, q_ref, k_ref, v_ref, o_ref, kmean_ref, vt_ref, kaug_ref, s_ref, *,
                 heads, group):
    qi = pl.program_id(2)
    blk = q_ref.shape[0]
    d = q_ref.shape[1] // heads
    nblk = k_ref.shape[0] // blk
    scale = d ** -0.5
    exp2_scale = scale * math.log2(math.e)
    assert AUG_PART_LANE + AUG_PARTS * nblk <= d

    def head_cols(t):
        return slice(t * d, (t + 1) * d)

    def head_slope(t):
        return slope_ref[pl.program_id(0) * heads + t]

    @pl.when((qi == 0) & (pl.program_id(1) == 0))
    def _():
        lane = lax.broadcasted_iota(jnp.int32, (blk, d), 1)
        row = lax.broadcasted_iota(jnp.int32, (blk, d), 0)
        for t in range(heads):
            key_scale = head_slope(t) * (1.0 / scale)
            for j in range(nblk):
                aug = jnp.zeros((blk, d), F32)
                for p, part in enumerate(_split_bf16((row + j * blk).astype(F32) * key_scale)):
                    aug = jnp.where(lane == p, part, aug)
                    aug = jnp.where(lane == AUG_PART_LANE + p * nblk + j, 1.0, aug)
                kaug_ref[t, j * blk:(j + 1) * blk, :] = aug.astype(BF16)

    @pl.when(qi == 0)
    def _():
        ones_row = jnp.where(lax.broadcasted_iota(jnp.int32, (BF16_ROWS, blk), 0) == 0, 1.0, 0.0)
        for t in range(heads):
            for j in range(nblk):
                rows = slice(j * blk, (j + 1) * blk)
                kmean_ref[t, j:j + 1, :] = jnp.sum(k_ref[rows, head_cols(t)].astype(F32), axis=0,
                                                   keepdims=True) * (1.0 / blk)
                vt_ref[t, j] = jnp.concatenate(
                    [v_ref[rows, head_cols(t)].astype(F32).T, ones_row], axis=0).astype(BF16)

    def query_operand(t):
        q = q_ref[:, head_cols(t)]
        kmean = kmean_ref[t]
        kmean_hi = kmean.astype(BF16)
        kmean_lo = (kmean - kmean_hi.astype(F32)).astype(BF16)
        gate = (lax.dot_general(kmean_hi, q, _NT_DIMS, preferred_element_type=F32)
                + lax.dot_general(kmean_lo, q, _NT_DIMS, preferred_element_type=F32))
        blk_id = lax.broadcasted_iota(jnp.int32, gate.shape, 0)
        rank = jnp.zeros(gate.shape, jnp.int32)
        for other in range(nblk):
            g_other = gate[other:other + 1, :]
            beats = (g_other > gate) | ((g_other == gate) & (other < blk_id))
            rank = rank + jnp.where(other < qi, jnp.where(beats, 1, 0), 0)
        keep = ((blk_id < qi) & (rank < MOBA_TOPK)) | (blk_id == qi)
        t_query = (qi * blk + lax.broadcasted_iota(jnp.int32, gate.shape, 1)).astype(F32)
        bias = (jnp.where(keep, 0.0, MOBA_MASKED) - head_slope(t) * t_query) * (1.0 / scale)
        ones = jnp.where(lax.broadcasted_iota(jnp.int32, (AUG_PART_LANE, blk), 0) < AUG_PARTS,
                         1.0, 0.0)
        pad = jnp.zeros((d - AUG_PART_LANE - AUG_PARTS * nblk, blk), F32)
        aug = jnp.concatenate([ones] + _split_bf16(bias) + [pad], axis=0)
        return jnp.concatenate([q, aug.T.astype(BF16)], axis=1)

    q_ops = [query_operand(t) for t in range(heads)]

    own_group = qi // group
    key_after_query = (lax.broadcasted_iota(jnp.int32, (blk, blk), 0)
                       > lax.broadcasted_iota(jnp.int32, (blk, blk), 1))

    def group_scores(t, g, n_blocks, own):
        rows = pl.ds(pl.multiple_of(g * group * blk, group * blk), n_blocks * blk)
        k_op = jnp.concatenate([k_ref[rows, head_cols(t)], kaug_ref[t, rows, :]], axis=1)
        group_s = lax.dot_general(k_op, q_ops[t], _NT_DIMS, preferred_element_type=F32)
        best = None
        for i in range(n_blocks):
            s = group_s[i * blk:(i + 1) * blk] * exp2_scale
            if own and i == n_blocks - 1:
                s = jnp.where(key_after_query, MOBA_MASKED, s)
            s_ref[t, g, i] = s
            block_max = jnp.max(s, axis=0, keepdims=True)
            best = block_max if best is None else jnp.maximum(best, block_max)
        return best

    def group_values(t, g, n_blocks, best, acc):
        p = jnp.concatenate([jnp.exp2(s_ref[t, g, i] - best).astype(BF16)
                             for i in range(n_blocks)], axis=0)
        vt = jnp.concatenate([vt_ref[t, g * group + i] for i in range(n_blocks)], axis=1)
        return acc + jnp.dot(vt, p, preferred_element_type=F32)

    def past_scores(g, best):
        return tuple(jnp.maximum(best[t], group_scores(t, g, group, False)) for t in range(heads))

    own_blocks = qi % group
    best = lax.switch(own_blocks, [
        functools.partial(lambda n: tuple(group_scores(t, own_group, n, True)
                                          for t in range(heads)), n + 1)
        for n in range(group)])
    best = lax.fori_loop(0, own_group, past_scores, best)

    def past_values(g, accs):
        return tuple(group_values(t, g, group, best[t], accs[t]) for t in range(heads))

    zero = jnp.zeros((d + BF16_ROWS, blk), F32)
    totals = lax.fori_loop(0, own_group, past_values, (zero,) * heads)
    totals = lax.switch(own_blocks, [
        functools.partial(lambda n, accs: tuple(group_values(t, own_group, n, best[t], accs[t])
                                                for t in range(heads)), n + 1)
        for n in range(group)], totals)
    for t in range(heads):
        acc = totals[t]
        o_ref[:, head_cols(t)] = (acc[:d] / acc[d:d + 1]).T.astype(o_ref.dtype)


def _moba(proj, *, batch, seq, d, heads, group):
    m = proj.shape[0]
    blk = MOBA_BLOCK
    assert seq % (group * blk) == 0 and MOBA_HEADS % heads == 0
    nq = seq // blk
    groups = MOBA_HEADS // heads
    slopes = _alibi_slopes(MOBA_HEADS)
    return pl.pallas_call(
        functools.partial(_moba_kernel, heads=heads, group=group),
        grid_spec=pltpu.PrefetchScalarGridSpec(
            num_scalar_prefetch=1,
            grid=(groups, batch, nq),
            in_specs=[
                pl.BlockSpec((blk, heads * d), lambda h, b, i, s_ref: (b * nq + i, h)),
                pl.BlockSpec((seq, heads * d), lambda h, b, i, s_ref: (b, groups + h),
                             pipeline_mode=pl.Buffered(1)),
                pl.BlockSpec((seq, heads * d), lambda h, b, i, s_ref: (b, 2 * groups + h),
                             pipeline_mode=pl.Buffered(1)),
            ],
            out_specs=pl.BlockSpec((blk, heads * d), lambda h, b, i, s_ref: (b * nq + i, h)),
            scratch_shapes=[pltpu.VMEM((heads, nq, d), F32),
                            pltpu.VMEM((heads, nq, d + BF16_ROWS, blk), BF16),
                            pltpu.VMEM((heads, seq, d), BF16),
                            pltpu.VMEM((heads, nq // group, group, blk, blk), F32)],
        ),
        out_shape=jax.ShapeDtypeStruct((m, MOBA_HEADS * d), BF16),
        compiler_params=_params("parallel", "arbitrary", "arbitrary"),
        name="moba",
    )(jnp.asarray(slopes), proj, proj, proj)


def _memory_attention(qm_ref, kv_ref):
    width = qm_ref.shape[1]
    dh = width // MEM_HEADS
    outs = []
    for h in range(MEM_HEADS):
        q = qm_ref[:, h * dh:(h + 1) * dh]
        k = kv_ref[:, h * dh:(h + 1) * dh]
        v = kv_ref[:, width + h * dh:width + (h + 1) * dh]
        s = lax.dot_general(q, k, _NT_DIMS, preferred_element_type=F32) * dh ** -0.5
        p = jnp.exp(s - jnp.max(s, axis=-1, keepdims=True))
        o = jnp.dot(p.astype(BF16), v, preferred_element_type=F32)
        outs.append((o / jnp.sum(p, axis=-1, keepdims=True)).astype(BF16))
    return jnp.concatenate(outs, axis=-1)


def _project_out(tok, mem_out, x_ref, wo_ref, o_ref):
    mix = tok.shape[1]
    o_ref[...] = (x_ref[...]
                  + jnp.dot(tok, wo_ref[:mix, :], preferred_element_type=F32)
                  + jnp.dot(mem_out, wo_ref[mix:, :], preferred_element_type=F32))


def _memory_kv(first, mem_ref, mg_ref, wkv_ref, kv_ref):
    @pl.when(first)
    def _():
        mem_n = _rms_norm(mem_ref[...], mg_ref[...]).astype(BF16)
        kv_ref[...] = jnp.dot(mem_n, wkv_ref[...], preferred_element_type=F32).astype(BF16)


def _out_proj_kernel(tok_ref, qm_ref, mem_ref, mg_ref, wkv_ref, x_ref, wo_ref, *refs,
                     tiles_per_seq):
    *refs, kv_ref = refs
    n_casts = (len(refs) - 1) // 2
    _memory_kv(pl.program_id(0) % tiles_per_seq == 0, mem_ref, mg_ref, wkv_ref, kv_ref)
    _project_out(tok_ref[...], _memory_attention(qm_ref, kv_ref), x_ref, wo_ref, refs[n_casts])
    _cast_blocks(refs[:n_casts], refs[n_casts + 1:])


def _conv_out_proj_kernel(gb_ref, gc_ref, hh_ref, gc_prev_ref, hh_prev_ref, cw_ref,
                          qm_ref, mem_ref, mg_ref, wkv_ref, x_ref, wo_ref, *refs, tiles_per_seq):
    *refs, kv_ref = refs
    n_casts = (len(refs) - 1) // 2
    first = pl.program_id(0) % tiles_per_seq == 0
    _memory_kv(first, mem_ref, mg_ref, wkv_ref, kv_ref)
    u = gc_ref[...].astype(F32) * hh_ref[...].astype(F32)
    prev = gc_prev_ref[...].astype(F32) * hh_prev_ref[...].astype(F32)
    prev = jnp.where(first, 0.0, prev)
    tok = (gb_ref[...].astype(F32) * _causal_conv3(u, prev, cw_ref[...])).astype(BF16)
    _project_out(tok, _memory_attention(qm_ref, kv_ref), x_ref, wo_ref, refs[n_casts])
    _cast_blocks(refs[:n_casts], refs[n_casts + 1:])


def _out_proj(tok, proj, conv_w, mem, mem_gain, w_mem_kv, x, wo, *, seq, mem_tokens, tm, casts=()):
    m, d = x.shape
    mem_width = w_mem_kv.shape[1] // 2
    mix = d - mem_width
    tiles_per_seq = seq // tm
    qm_block = proj.shape[1] // mem_width - 1
    cast_in, cast_out, cast_shapes = _cast_specs(casts, m // tm, lambda i: i)
    tail_specs = [
        pl.BlockSpec((tm, mem_width), lambda i: (i, qm_block)),
        pl.BlockSpec((mem_tokens, d), lambda i: (i // tiles_per_seq, 0)),
        pl.BlockSpec((1, d), lambda i: (0, 0)),
        pl.BlockSpec((d, 2 * mem_width), lambda i: (0, 0), pipeline_mode=pl.Buffered(1)),
        pl.BlockSpec((tm, d), lambda i: (i, 0)),
        pl.BlockSpec((d, d), lambda i: (0, 0), pipeline_mode=pl.Buffered(1)),
    ] + cast_in
    if tok is not None:
        kernel = functools.partial(_out_proj_kernel, tiles_per_seq=tiles_per_seq)
        head_specs = [pl.BlockSpec((tm, mix), lambda i: (i, 0))]
        head_args = (tok,)
    else:
        kernel = functools.partial(_conv_out_proj_kernel, tiles_per_seq=tiles_per_seq)
        halo = tm // SUBLANES

        def prev_rows(part):
            return pl.BlockSpec((SUBLANES, mix), lambda i: (jnp.maximum(i * halo - 1, 0), part))

        head_specs = [pl.BlockSpec((tm, mix), lambda i: (i, 0)),
                      pl.BlockSpec((tm, mix), lambda i: (i, 1)),
                      pl.BlockSpec((tm, mix), lambda i: (i, 2)),
                      prev_rows(1), prev_rows(2),
                      pl.BlockSpec((CONV_WIDTH, mix), lambda i: (0, 0))]
        head_args = (proj, proj, proj, proj, proj, conv_w)
    out, *casted = pl.pallas_call(
        kernel,
        grid=(m // tm,),
        in_specs=head_specs + tail_specs,
        out_specs=[pl.BlockSpec((tm, d), lambda i: (i, 0))] + cast_out,
        out_shape=[jax.ShapeDtypeStruct((m, d), F32)] + cast_shapes,
        scratch_shapes=[pltpu.VMEM((mem_tokens, 2 * mem_width), BF16)],
        compiler_params=_params("arbitrary"),
        name="out_proj",
    )(*head_args, proj, mem, mem_gain.reshape(1, d), w_mem_kv, x, wo, *casts)
    return out, casted


def _conv_ffn_kernel(x_ref, g_ref, wg_ref, wu_ref, conv_ref, wd_ref,
                     fg_ref, o_ref, h_ref, carry_ref, up_ref, *,
                     tiles_per_seq, row_strips, final_norm):
    f = pl.program_id(1)
    nf = pl.num_programs(1)
    first = pl.program_id(0) % tiles_per_seq == 0
    tm = x_ref.shape[0]

    @pl.when(f == 0)
    def _():
        x = x_ref[...]
        h_ref[...] = _rms_norm(x, g_ref[...]).astype(BF16)
        o_ref[...] = x

    @pl.when(first)
    def _():
        carry_ref[f] = jnp.zeros(carry_ref.shape[1:], F32)

    strip = tm // row_strips
    for slot in range(2):
        up_ref[slot, :SUBLANES, :] = carry_ref[f, slot]
    for s in range(row_strips):
        for slot, w_ref in enumerate((wg_ref, wu_ref)):
            up_ref[slot, pl.ds(SUBLANES + s * strip, strip), :] = jnp.dot(
                h_ref[pl.ds(s * strip, strip), :], w_ref[...], preferred_element_type=F32)
    conv = [conv_ref[f + slot * nf] for slot in range(2)]
    for s in range(row_strips):
        branches = []
        for slot in range(2):
            taps = [up_ref[slot, pl.ds(SUBLANES + s * strip - (CONV_WIDTH - 1 - k), strip), :]
                    for k in range(CONV_WIDTH)]
            branches.append(conv[slot][0:1] * taps[0] + conv[slot][1:2] * taps[1]
                            + conv[slot][2:3] * taps[2] + conv[slot][CONV_WIDTH:CONV_WIDTH + 1])
        act = (_silu(branches[0]) * branches[1]).astype(BF16)
        o_ref[pl.ds(s * strip, strip), :] += jnp.dot(act, wd_ref[...], preferred_element_type=F32)
    for slot in range(2):
        carry_ref[f, slot] = up_ref[slot, pl.ds(tm, SUBLANES), :]

    if final_norm:
        @pl.when(f == pl.num_programs(1) - 1)
        def _():
            o_ref[...] = _rms_norm(o_ref[...], fg_ref[...])


def _conv_ffn(x, gain, w_up, conv_w, conv_b, w_down, final_gain, *, seq, tm, tf, final_norm):
    m, d = x.shape
    d_ff = w_down.shape[0]
    nf = d_ff // tf
    conv = jnp.concatenate(
        [conv_w.reshape(CONV_WIDTH, 2 * nf, tf).transpose(1, 0, 2),
         conv_b.reshape(2 * nf, 1, tf),
         jnp.zeros((2 * nf, SUBLANES - CONV_WIDTH - 1, tf), F32)], axis=1)
    return pl.pallas_call(
        functools.partial(_conv_ffn_kernel, tiles_per_seq=seq // tm, row_strips=FFN_ROW_STRIPS,
                          final_norm=final_norm),
        grid=(m // tm, nf),
        in_specs=[
            pl.BlockSpec((tm, d), lambda i, f: (i, 0)),
            pl.BlockSpec((1, d), lambda i, f: (0, 0)),
            pl.BlockSpec((d, tf), lambda i, f: (0, f)),
            pl.BlockSpec((d, tf), lambda i, f: (0, nf + f)),
            pl.BlockSpec((2 * nf, SUBLANES, tf), lambda i, f: (0, 0, 0)),
            pl.BlockSpec((tf, d), lambda i, f: (f, 0)),
            pl.BlockSpec((1, d), lambda i, f: (0, 0)),
        ],
        out_specs=pl.BlockSpec((tm, d), lambda i, f: (i, 0)),
        out_shape=jax.ShapeDtypeStruct((m, d), F32),
        scratch_shapes=[pltpu.VMEM((tm, d), BF16),
                        pltpu.VMEM((nf, 2, SUBLANES, tf), F32),
                        pltpu.VMEM((2, SUBLANES + tm, tf), F32)],
        compiler_params=_params("arbitrary", "arbitrary"),
        name="conv_ffn",
    )(x, gain.reshape(1, d), w_up, w_up, conv, w_down, final_gain.reshape(1, d))


MXU_WIDTH = 256

PROJ_ROWS = 512
FFN_ROWS = 1024
FFN_COLS = 512
FFN_ROW_STRIPS = 2
RET_HEADS_PER_STEP = 2
MOBA_HEADS_PER_STEP = 6
MOBA_GROUP = 4


def _half_tile(n):
    return n // 2 if n % (2 * MXU_WIDTH) == 0 else n


def _trunk(x, mem, mem_norm, layers, final_norm):
    batch, seq, d = x.shape
    mem_tokens = mem.shape[1]
    x = x.reshape(batch * seq, d)
    mem = mem.reshape(batch * mem_tokens, d)
    n_layers = len(layers)
    w_in = layers[0][1].astype(BF16)
    w_mem_kv = layers[0][3].astype(BF16)
    for i, (norm_mix, _, extra, _, w_o, norm_ffn, w_up, cw, cb, w_down) in enumerate(layers):
        kind = i % 3
        mem_width = w_mem_kv.shape[1] // 2
        mix = d - mem_width
        proj, (w_up, w_o) = _norm_matmul(x, norm_mix, w_in, tm=PROJ_ROWS,
                                         tn=_half_tile(w_in.shape[1]), casts=(w_up, w_o))
        if kind == 0:
            tok = _retention(proj, extra, batch=batch, seq=seq, rows=seq, heads=RET_HEADS_PER_STEP)
        elif kind == 1:
            tok = None
        else:
            tok = _moba(proj, batch=batch, seq=seq, d=mix // MOBA_HEADS,
                        heads=MOBA_HEADS_PER_STEP, group=MOBA_GROUP)
        next_weights = (layers[i + 1][1], layers[i + 1][3]) if i + 1 < n_layers else ()
        x, (w_down, *next_weights) = _out_proj(tok, proj, extra, mem, mem_norm, w_mem_kv, x, w_o,
                                               seq=seq, mem_tokens=mem_tokens, tm=PROJ_ROWS,
                                               casts=(w_down,) + next_weights)
        if next_weights:
            w_in, w_mem_kv = next_weights
        x = _conv_ffn(x, norm_ffn, w_up, cw, cb, w_down, final_norm, seq=seq, tm=FFN_ROWS,
                      tf=FFN_COLS, final_norm=(i == n_layers - 1))
    return x.reshape(batch, seq, d)


def kernel(x, mem, mem_norm,
           l0_norm_mix, l0_w_in, l0_ret_gn, l0_w_mem_kv, l0_w_o, l0_norm_ffn, l0_ffn_w_up, l0_ffn_conv_w, l0_ffn_conv_b, l0_ffn_w_down,
           l1_norm_mix, l1_w_in, l1_conv_w, l1_w_mem_kv, l1_w_o, l1_norm_ffn, l1_ffn_w_up, l1_ffn_conv_w, l1_ffn_conv_b, l1_ffn_w_down,
           l2_norm_mix, l2_w_in, l2_w_mem_kv, l2_w_o, l2_norm_ffn, l2_ffn_w_up, l2_ffn_conv_w, l2_ffn_conv_b, l2_ffn_w_down,
           l3_norm_mix, l3_w_in, l3_ret_gn, l3_w_mem_kv, l3_w_o, l3_norm_ffn, l3_ffn_w_up, l3_ffn_conv_w, l3_ffn_conv_b, l3_ffn_w_down,
           final_norm):
    layers = [
        (l0_norm_mix, l0_w_in, l0_ret_gn, l0_w_mem_kv, l0_w_o, l0_norm_ffn, l0_ffn_w_up, l0_ffn_conv_w, l0_ffn_conv_b, l0_ffn_w_down),
        (l1_norm_mix, l1_w_in, l1_conv_w, l1_w_mem_kv, l1_w_o, l1_norm_ffn, l1_ffn_w_up, l1_ffn_conv_w, l1_ffn_conv_b, l1_ffn_w_down),
        (l2_norm_mix, l2_w_in, None, l2_w_mem_kv, l2_w_o, l2_norm_ffn, l2_ffn_w_up, l2_ffn_conv_w, l2_ffn_conv_b, l2_ffn_w_down),
        (l3_norm_mix, l3_w_in, l3_ret_gn, l3_w_mem_kv, l3_w_o, l3_norm_ffn, l3_ffn_w_up, l3_ffn_conv_w, l3_ffn_conv_b, l3_ffn_w_down),
    ]
    return _trunk(x, mem, mem_norm, layers, final_norm)
```

```python
import functools
import math

import jax
import jax.numpy as jnp
import numpy as np
from jax import lax
from jax.experimental import pallas as pl
from jax.experimental.pallas import tpu as pltpu

F32 = jnp.float32
BF16 = jnp.bfloat16

NORM_EPS = 1e-6
MEM_HEADS = 4
RET_HEADS = 6
RET_CHUNK = 256
CONV_WIDTH = 3
MOBA_HEADS = 12
MOBA_BLOCK = 256
MOBA_TOPK = 3

SUBLANES = 8
VMEM_LIMIT_BYTES = 56 * 1024 * 1024

_NT_DIMS = (((1,), (1,)), ((), ()))
_TN_DIMS = (((0,), (0,)), ((), ()))


def _params(*semantics):
    return pltpu.CompilerParams(dimension_semantics=semantics,
                                vmem_limit_bytes=VMEM_LIMIT_BYTES)


def _rms_norm(x, gain):
    return x * lax.rsqrt(jnp.mean(x * x, axis=-1, keepdims=True) + NORM_EPS) * gain


def _silu(g):
    return g * (1.0 / (1.0 + jnp.exp(-g)))


def _shift_rows(u, prev, k):
    rolled = pltpu.roll(u, k, axis=0)
    prev_rolled = pltpu.roll(prev, k, axis=0)
    row = lax.broadcasted_iota(jnp.int32, prev.shape, 0)
    head = jnp.where(row < k, prev_rolled, rolled[:SUBLANES])
    return jnp.concatenate([head, rolled[SUBLANES:]], axis=0)


def _causal_conv3(u, prev, w):
    return (w[0:1] * _shift_rows(u, prev, 2) + w[1:2] * _shift_rows(u, prev, 1)
            + w[2:3] * u)


BF16_ROWS = 16


def _cast_specs(weights, n_steps, step_index):
    in_specs, out_specs, out_shapes = [], [], []
    for w in weights:
        rows, cols = w.shape
        assert rows % (n_steps * BF16_ROWS) == 0
        block = (rows // n_steps, cols)
        spec = pl.BlockSpec(block, lambda *idx: (step_index(*idx), 0))
        in_specs.append(spec)
        out_specs.append(spec)
        out_shapes.append(jax.ShapeDtypeStruct(w.shape, BF16))
    return in_specs, out_specs, out_shapes


def _cast_blocks(src_refs, dst_refs):
    for src, dst in zip(src_refs, dst_refs):
        dst[...] = src[...].astype(BF16)


def _norm_matmul_kernel(x_ref, g_ref, w_ref, *refs):
    n_casts = (len(refs) - 1) // 2
    o_ref = refs[n_casts]
    h = _rms_norm(x_ref[...], g_ref[...]).astype(BF16)
    o_ref[...] = jnp.dot(h, w_ref[...], preferred_element_type=F32).astype(o_ref.dtype)
    _cast_blocks(refs[:n_casts], refs[n_casts + 1:])


def _norm_matmul(x, gain, w, *, tm, tn, casts=()):
    m, d = x.shape
    n = w.shape[1]
    ni = m // tm
    cast_in, cast_out, cast_shapes = _cast_specs(casts, (n // tn) * ni, lambda j, i: j * ni + i)
    out, *casted = pl.pallas_call(
        _norm_matmul_kernel,
        grid=(n // tn, ni),
        in_specs=[
            pl.BlockSpec((tm, d), lambda j, i: (i, 0)),
            pl.BlockSpec((1, d), lambda j, i: (0, 0)),
            pl.BlockSpec((d, tn), lambda j, i: (0, j),
                         pipeline_mode=pl.Buffered(1 if tn == n else 2)),
        ] + cast_in,
        out_specs=[pl.BlockSpec((tm, tn), lambda j, i: (i, j))] + cast_out,
        out_shape=[jax.ShapeDtypeStruct((m, n), BF16)] + cast_shapes,
        compiler_params=_params("parallel", "parallel"),
        name="norm_matmul",
    )(x, gain.reshape(1, d), w, *casts)
    return out, casted


def _retention_kernel(lg_ref, q_ref, k_ref, v_ref, g_ref, gn_ref, o_ref, state_ref, *,
                      chunk, heads):
    rows = q_ref.shape[0]
    d = q_ref.shape[1] // heads
    c = chunk
    k_scale = d ** -0.5

    @pl.when(pl.program_id(2) == 0)
    def _():
        state_ref[...] = jnp.zeros_like(state_ref)

    diff = (lax.broadcasted_iota(jnp.int32, (c, c), 0)
            - lax.broadcasted_iota(jnp.int32, (c, c), 1)).astype(F32)
    pos = lax.broadcasted_iota(jnp.int32, (c, d), 0).astype(F32)
    tables = []
    for t in range(heads):
        lg = lg_ref[pl.program_id(1) * heads + t]
        intra = jnp.where(diff >= 0, jnp.exp(jnp.maximum(diff, 0.0) * lg), 0.0) * k_scale
        q_decay = jnp.exp((pos + 1.0) * lg)
        k_decay = jnp.exp((c - 1.0 - pos) * lg) * k_scale
        chunk_decay = jnp.exp(jnp.zeros((1, d), F32) + c * lg)
        tables.append((intra, q_decay, k_decay, chunk_decay))

    for s in range(rows // c):
        sl = pl.ds(s * c, c)
        for t, (intra, q_decay, k_decay, chunk_decay) in enumerate(tables):
            cols = slice(t * d, (t + 1) * d)
            q = q_ref[sl, cols]
            k = k_ref[sl, cols]
            v = v_ref[sl, cols]
            scores = lax.dot_general(q, k, _NT_DIMS, preferred_element_type=F32) * intra
            y = jnp.dot(scores.astype(BF16), v, preferred_element_type=F32)
            state = state_ref[t]
            y = y + jnp.dot(q, state.astype(BF16), preferred_element_type=F32) * q_decay
            kd = (k.astype(F32) * k_decay).astype(BF16)
            kv = lax.dot_general(kd, v, _TN_DIMS, preferred_element_type=F32)
            state_ref[t] = state * chunk_decay + kv
            g = g_ref[sl, cols].astype(F32)
            o_ref[sl, cols] = (_silu(g) * _rms_norm(y, gn_ref[:, cols])).astype(o_ref.dtype)


def _retention(proj, gn_gain, *, batch, seq, rows, heads):
    m = proj.shape[0]
    d = gn_gain.shape[0] // RET_HEADS
    nt = seq // rows
    groups = RET_HEADS // heads
    lg = np.log1p(-np.exp2(-5.0 - np.arange(RET_HEADS))).astype(np.float32)

    def spec(part):
        return pl.BlockSpec((rows, heads * d),
                            lambda b, h, t, lg_ref: (b * nt + t, part * groups + h))

    return pl.pallas_call(
        functools.partial(_retention_kernel, chunk=RET_CHUNK, heads=heads),
        grid_spec=pltpu.PrefetchScalarGridSpec(
            num_scalar_prefetch=1,
            grid=(batch, groups, nt),
            in_specs=[spec(0), spec(1), spec(2), spec(3),
                      pl.BlockSpec((1, heads * d), lambda b, h, t, lg_ref: (0, h))],
            out_specs=pl.BlockSpec((rows, heads * d), lambda b, h, t, lg_ref: (b * nt + t, h)),
            scratch_shapes=[pltpu.VMEM((heads, d, d), F32)],
        ),
        out_shape=jax.ShapeDtypeStruct((m, RET_HEADS * d), BF16),
        compiler_params=_params("parallel", "parallel", "arbitrary"),
        name="retention",
    )(jnp.asarray(lg), proj, proj, proj, proj, gn_gain.reshape(1, -1))


def _alibi_slopes(n):
    def pow2_slopes(m):
        start = 2.0 ** (-8.0 / m)
        return [start ** (i + 1) for i in range(m)]
    if math.log2(n).is_integer():
        s = pow2_slopes(n)
    else:
        c = 2 ** int(math.floor(math.log2(n)))
        s = pow2_slopes(c) + list(_alibi_slopes(2 * c))[0::2][: n - c]
    return np.asarray(s, dtype=np.float32)


MOBA_MASKED = -1e30
AUG_PARTS = 3
AUG_PART_LANE = SUBLANES


def _split_bf16(x):
    parts = []
    for _ in range(AUG_PARTS):
        part = x.astype(BF16).astype(F32)
        parts.append(part)
        x = x - part
    return parts


def _moba_kernel(slope_ref, q_ref, k_ref, v_ref, o_ref, kmean_ref, vt_ref, kaug_ref, s_ref, *,
                 heads, group):
    qi = pl.program_id(2)
    blk = q_ref.shape[0]
    d = q_ref.shape[1] // heads
    nblk = k_ref.shape[0] // blk
    scale = d ** -0.5
    exp2_scale = scale * math.log2(math.e)
    assert AUG_PART_LANE + AUG_PARTS * nblk <= d

    def head_cols(t):
        return slice(t * d, (t + 1) * d)

    def head_slope(t):
        return slope_ref[pl.program_id(0) * heads + t]

    @pl.when((qi == 0) & (pl.program_id(1) == 0))
    def _():
        lane = lax.broadcasted_iota(jnp.int32, (blk, d), 1)
        row = lax.broadcasted_iota(jnp.int32, (blk, d), 0)
        for t in range(heads):
            key_scale = head_slope(t) * (1.0 / scale)
            for j in range(nblk):
                aug = jnp.zeros((blk, d), F32)
                for p, part in enumerate(_split_bf16((row + j * blk).astype(F32) * key_scale)):
                    aug = jnp.where(lane == p, part, aug)
                    aug = jnp.where(lane == AUG_PART_LANE + p * nblk + j, 1.0, aug)
                kaug_ref[t, j * blk:(j + 1) * blk, :] = aug.astype(BF16)

    @pl.when(qi == 0)
    def _():
        ones_row = jnp.where(lax.broadcasted_iota(jnp.int32, (BF16_ROWS, blk), 0) == 0, 1.0, 0.0)
        for t in range(heads):
            for j in range(nblk):
                rows = slice(j * blk, (j + 1) * blk)
                kmean_ref[t, j:j + 1, :] = jnp.sum(k_ref[rows, head_cols(t)].astype(F32), axis=0,
                                                   keepdims=True) * (1.0 / blk)
                vt_ref[t, j] = jnp.concatenate(
                    [v_ref[rows, head_cols(t)].astype(F32).T, ones_row], axis=0).astype(BF16)

    def query_operand(t):
        q = q_ref[:, head_cols(t)]
        kmean = kmean_ref[t]
        kmean_hi = kmean.astype(BF16)
        kmean_lo = (kmean - kmean_hi.astype(F32)).astype(BF16)
        gate = (lax.dot_general(kmean_hi, q, _NT_DIMS, preferred_element_type=F32)
                + lax.dot_general(kmean_lo, q, _NT_DIMS, preferred_element_type=F32))
        blk_id = lax.broadcasted_iota(jnp.int32, gate.shape, 0)
        rank = jnp.zeros(gate.shape, jnp.int32)
        for other in range(nblk):
            g_other = gate[other:other + 1, :]
            beats = (g_other > gate) | ((g_other == gate) & (other < blk_id))
            rank = rank + jnp.where(other < qi, jnp.where(beats, 1, 0), 0)
        keep = ((blk_id < qi) & (rank < MOBA_TOPK)) | (blk_id == qi)
        t_query = (qi * blk + lax.broadcasted_iota(jnp.int32, gate.shape, 1)).astype(F32)
        bias = (jnp.where(keep, 0.0, MOBA_MASKED) - head_slope(t) * t_query) * (1.0 / scale)
        ones = jnp.where(lax.broadcasted_iota(jnp.int32, (AUG_PART_LANE, blk), 0) < AUG_PARTS,
                         1.0, 0.0)
        pad = jnp.zeros((d - AUG_PART_LANE - AUG_PARTS * nblk, blk), F32)
        aug = jnp.concatenate([ones] + _split_bf16(bias) + [pad], axis=0)
        return jnp.concatenate([q, aug.T.astype(BF16)], axis=1)

    q_ops = [query_operand(t) for t in range(heads)]

    own_group = qi // group
    key_after_query = (lax.broadcasted_iota(jnp.int32, (blk, blk), 0)
                       > lax.broadcasted_iota(jnp.int32, (blk, blk), 1))

    def group_scores(t, g, n_blocks, own):
        rows = pl.ds(pl.multiple_of(g * group * blk, group * blk), n_blocks * blk)
        k_op = jnp.concatenate([k_ref[rows, head_cols(t)], kaug_ref[t, rows, :]], axis=1)
        group_s = lax.dot_general(k_op, q_ops[t], _NT_DIMS, preferred_element_type=F32)
        best = None
        for i in range(n_blocks):
            s = group_s[i * blk:(i + 1) * blk] * exp2_scale
            if own and i == n_blocks - 1:
                s = jnp.where(key_after_query, MOBA_MASKED, s)
            s_ref[t, g, i] = s
            block_max = jnp.max(s, axis=0, keepdims=True)
            best = block_max if best is None else jnp.maximum(best, block_max)
        return best

    def group_values(t, g, n_blocks, best, acc):
        p = jnp.concatenate([jnp.exp2(s_ref[t, g, i] - best).astype(BF16)
                             for i in range(n_blocks)], axis=0)
        vt = jnp.concatenate([vt_ref[t, g * group + i] for i in range(n_blocks)], axis=1)
        return acc + jnp.dot(vt, p, preferred_element_type=F32)

    def past_scores(g, best):
        return tuple(jnp.maximum(best[t], group_scores(t, g, group, False)) for t in range(heads))

    own_blocks = qi % group
    best = lax.switch(own_blocks, [
        functools.partial(lambda n: tuple(group_scores(t, own_group, n, True)
                                          for t in range(heads)), n + 1)
        for n in range(group)])
    best = lax.fori_loop(0, own_group, past_scores, best)

    def past_values(g, accs):
        return tuple(group_values(t, g, group, best[t], accs[t]) for t in range(heads))

    zero = jnp.zeros((d + BF16_ROWS, blk), F32)
    totals = lax.fori_loop(0, own_group, past_values, (zero,) * heads)
    totals = lax.switch(own_blocks, [
        functools.partial(lambda n, accs: tuple(group_values(t, own_group, n, best[t], accs[t])
                                                for t in range(heads)), n + 1)
        for n in range(group)], totals)
    for t in range(heads):
        acc = totals[t]
        o_ref[:, head_cols(t)] = (acc[:d] / acc[d:d + 1]).T.astype(o_ref.dtype)


def _moba(proj, *, batch, seq, d, heads, group):
    m = proj.shape[0]
    blk = MOBA_BLOCK
    assert seq % (group * blk) == 0 and MOBA_HEADS % heads == 0
    nq = seq // blk
    groups = MOBA_HEADS // heads
    slopes = _alibi_slopes(MOBA_HEADS)
    return pl.pallas_call(
        functools.partial(_moba_kernel, heads=heads, group=group),
        grid_spec=pltpu.PrefetchScalarGridSpec(
            num_scalar_prefetch=1,
            grid=(groups, batch, nq),
            in_specs=[
                pl.BlockSpec((blk, heads * d), lambda h, b, i, s_ref: (b * nq + i, h)),
                pl.BlockSpec((seq, heads * d), lambda h, b, i, s_ref: (b, groups + h),
                             pipeline_mode=pl.Buffered(1)),
                pl.BlockSpec((seq, heads * d), lambda h, b, i, s_ref: (b, 2 * groups + h),
                             pipeline_mode=pl.Buffered(1)),
            ],
            out_specs=pl.BlockSpec((blk, heads * d), lambda h, b, i, s_ref: (b * nq + i, h)),
            scratch_shapes=[pltpu.VMEM((heads, nq, d), F32),
                            pltpu.VMEM((heads, nq, d + BF16_ROWS, blk), BF16),
                            pltpu.VMEM((heads, seq, d), BF16),
                            pltpu.VMEM((heads, nq // group, group, blk, blk), F32)],
        ),
        out_shape=jax.ShapeDtypeStruct((m, MOBA_HEADS * d), BF16),
        compiler_params=_params("parallel", "arbitrary", "arbitrary"),
        name="moba",
    )(jnp.asarray(slopes), proj, proj, proj)


def _memory_attention(qm_ref, kv_ref):
    width = qm_ref.shape[1]
    dh = width // MEM_HEADS
    outs = []
    for h in range(MEM_HEADS):
        q = qm_ref[:, h * dh:(h + 1) * dh]
        k = kv_ref[:, h * dh:(h + 1) * dh]
        v = kv_ref[:, width + h * dh:width + (h + 1) * dh]
        s = lax.dot_general(q, k, _NT_DIMS, preferred_element_type=F32) * dh ** -0.5
        p = jnp.exp(s - jnp.max(s, axis=-1, keepdims=True))
        o = jnp.dot(p.astype(BF16), v, preferred_element_type=F32)
        outs.append((o / jnp.sum(p, axis=-1, keepdims=True)).astype(BF16))
    return jnp.concatenate(outs, axis=-1)


def _project_out(tok, mem_out, x_ref, wo_ref, o_ref):
    mix = tok.shape[1]
    o_ref[...] = (x_ref[...]
                  + jnp.dot(tok, wo_ref[:mix, :], preferred_element_type=F32)
                  + jnp.dot(mem_out, wo_ref[mix:, :], preferred_element_type=F32))


def _memory_kv(first, mem_ref, mg_ref, wkv_ref, kv_ref):
    @pl.when(first)
    def _():
        mem_n = _rms_norm(mem_ref[...], mg_ref[...]).astype(BF16)
        kv_ref[...] = jnp.dot(mem_n, wkv_ref[...], preferred_element_type=F32).astype(BF16)


def _out_proj_kernel(tok_ref, qm_ref, mem_ref, mg_ref, wkv_ref, x_ref, wo_ref, *refs,
                     tiles_per_seq):
    *refs, kv_ref = refs
    n_casts = (len(refs) - 1) // 2
    _memory_kv(pl.program_id(0) % tiles_per_seq == 0, mem_ref, mg_ref, wkv_ref, kv_ref)
    _project_out(tok_ref[...], _memory_attention(qm_ref, kv_ref), x_ref, wo_ref, refs[n_casts])
    _cast_blocks(refs[:n_casts], refs[n_casts + 1:])


def _conv_out_proj_kernel(gb_ref, gc_ref, hh_ref, gc_prev_ref, hh_prev_ref, cw_ref,
                          qm_ref, mem_ref, mg_ref, wkv_ref, x_ref, wo_ref, *refs, tiles_per_seq):
    *refs, kv_ref = refs
    n_casts = (len(refs) - 1) // 2
    first = pl.program_id(0) % tiles_per_seq == 0
    _memory_kv(first, mem_ref, mg_ref, wkv_ref, kv_ref)
    u = gc_ref[...].astype(F32) * hh_ref[...].astype(F32)
    prev = gc_prev_ref[...].astype(F32) * hh_prev_ref[...].astype(F32)
    prev = jnp.where(first, 0.0, prev)
    tok = (gb_ref[...].astype(F32) * _causal_conv3(u, prev, cw_ref[...])).astype(BF16)
    _project_out(tok, _memory_attention(qm_ref, kv_ref), x_ref, wo_ref, refs[n_casts])
    _cast_blocks(refs[:n_casts], refs[n_casts + 1:])


def _out_proj(tok, proj, conv_w, mem, mem_gain, w_mem_kv, x, wo, *, seq, mem_tokens, tm, casts=()):
    m, d = x.shape
    mem_width = w_mem_kv.shape[1] // 2
    mix = d - mem_width
    tiles_per_seq = seq // tm
    qm_block = proj.shape[1] // mem_width - 1
    cast_in, cast_out, cast_shapes = _cast_specs(casts, m // tm, lambda i: i)
    tail_specs = [
        pl.BlockSpec((tm, mem_width), lambda i: (i, qm_block)),
        pl.BlockSpec((mem_tokens, d), lambda i: (i // tiles_per_seq, 0)),
        pl.BlockSpec((1, d), lambda i: (0, 0)),
        pl.BlockSpec((d, 2 * mem_width), lambda i: (0, 0), pipeline_mode=pl.Buffered(1)),
        pl.BlockSpec((tm, d), lambda i: (i, 0)),
        pl.BlockSpec((d, d), lambda i: (0, 0), pipeline_mode=pl.Buffered(1)),
    ] + cast_in
    if tok is not None:
        kernel = functools.partial(_out_proj_kernel, tiles_per_seq=tiles_per_seq)
        head_specs = [pl.BlockSpec((tm, mix), lambda i: (i, 0))]
        head_args = (tok,)
    else:
        kernel = functools.partial(_conv_out_proj_kernel, tiles_per_seq=tiles_per_seq)
        halo = tm // SUBLANES

        def prev_rows(part):
            return pl.BlockSpec((SUBLANES, mix), lambda i: (jnp.maximum(i * halo - 1, 0), part))

        head_specs = [pl.BlockSpec((tm, mix), lambda i: (i, 0)),
                      pl.BlockSpec((tm, mix), lambda i: (i, 1)),
                      pl.BlockSpec((tm, mix), lambda i: (i, 2)),
                      prev_rows(1), prev_rows(2),
                      pl.BlockSpec((CONV_WIDTH, mix), lambda i: (0, 0))]
        head_args = (proj, proj, proj, proj, proj, conv_w)
    out, *casted = pl.pallas_call(
        kernel,
        grid=(m // tm,),
        in_specs=head_specs + tail_specs,
        out_specs=[pl.BlockSpec((tm, d), lambda i: (i, 0))] + cast_out,
        out_shape=[jax.ShapeDtypeStruct((m, d), F32)] + cast_shapes,
        scratch_shapes=[pltpu.VMEM((mem_tokens, 2 * mem_width), BF16)],
        compiler_params=_params("arbitrary"),
        name="out_proj",
    )(*head_args, proj, mem, mem_gain.reshape(1, d), w_mem_kv, x, wo, *casts)
    return out, casted


def _conv_ffn_kernel(x_ref, g_ref, wg_ref, wu_ref, conv_ref, wd_ref,
                     fg_ref, o_ref, h_ref, carry_ref, up_ref, *,
                     tiles_per_seq, row_strips, final_norm):
    f = pl.program_id(1)
    nf = pl.num_programs(1)
    first = pl.program_id(0) % tiles_per_seq == 0
    tm = x_ref.shape[0]

    @pl.when(f == 0)
    def _():
        x = x_ref[...]
        h_ref[...] = _rms_norm(x, g_ref[...]).astype(BF16)
        o_ref[...] = x

    @pl.when(first)
    def _():
        carry_ref[f] = jnp.zeros(carry_ref.shape[1:], F32)

    strip = tm // row_strips
    for slot in range(2):
        up_ref[slot, :SUBLANES, :] = carry_ref[f, slot]
    for s in range(row_strips):
        for slot, w_ref in enumerate((wg_ref, wu_ref)):
            up_ref[slot, pl.ds(SUBLANES + s * strip, strip), :] = jnp.dot(
                h_ref[pl.ds(s * strip, strip), :], w_ref[...], preferred_element_type=F32)
    conv = [conv_ref[f + slot * nf] for slot in range(2)]
    for s in range(row_strips):
        branches = []
        for slot in range(2):
            taps = [up_ref[slot, pl.ds(SUBLANES + s * strip - (CONV_WIDTH - 1 - k), strip), :]
                    for k in range(CONV_WIDTH)]
            branches.append(conv[slot][0:1] * taps[0] + conv[slot][1:2] * taps[1]
                            + conv[slot][2:3] * taps[2] + conv[slot][CONV_WIDTH:CONV_WIDTH + 1])
        act = (_silu(branches[0]) * branches[1]).astype(BF16)
        o_ref[pl.ds(s * strip, strip), :] += jnp.dot(act, wd_ref[...], preferred_element_type=F32)
    for slot in range(2):
        carry_ref[f, slot] = up_ref[slot, pl.ds(tm, SUBLANES), :]

    if final_norm:
        @pl.when(f == pl.num_programs(1) - 1)
        def _():
            o_ref[...] = _rms_norm(o_ref[...], fg_ref[...])


def _conv_ffn(x, gain, w_up, conv_w, conv_b, w_down, final_gain, *, seq, tm, tf, final_norm):
    m, d = x.shape
    d_ff = w_down.shape[0]
    nf = d_ff // tf
    conv = jnp.concatenate(
        [conv_w.reshape(CONV_WIDTH, 2 * nf, tf).transpose(1, 0, 2),
         conv_b.reshape(2 * nf, 1, tf),
         jnp.zeros((2 * nf, SUBLANES - CONV_WIDTH - 1, tf), F32)], axis=1)
    return pl.pallas_call(
        functools.partial(_conv_ffn_kernel, tiles_per_seq=seq // tm, row_strips=FFN_ROW_STRIPS,
                          final_norm=final_norm),
        grid=(m // tm, nf),
        in_specs=[
            pl.BlockSpec((tm, d), lambda i, f: (i, 0)),
            pl.BlockSpec((1, d), lambda i, f: (0, 0)),
            pl.BlockSpec((d, tf), lambda i, f: (0, f)),
            pl.BlockSpec((d, tf), lambda i, f: (0, nf + f)),
            pl.BlockSpec((2 * nf, SUBLANES, tf), lambda i, f: (0, 0, 0)),
            pl.BlockSpec((tf, d), lambda i, f: (f, 0)),
            pl.BlockSpec((1, d), lambda i, f: (0, 0)),
        ],
        out_specs=pl.BlockSpec((tm, d), lambda i, f: (i, 0)),
        out_shape=jax.ShapeDtypeStruct((m, d), F32),
        scratch_shapes=[pltpu.VMEM((tm, d), BF16),
                        pltpu.VMEM((nf, 2, SUBLANES, tf), F32),
                        pltpu.VMEM((2, SUBLANES + tm, tf), F32)],
        compiler_params=_params("arbitrary", "arbitrary"),
        name="conv_ffn",
    )(x, gain.reshape(1, d), w_up, w_up, conv, w_down, final_gain.reshape(1, d))


MXU_WIDTH = 256

PROJ_ROWS = 512
FFN_ROWS = 1024
FFN_COLS = 512
FFN_ROW_STRIPS = 2
RET_HEADS_PER_STEP = 2
MOBA_HEADS_PER_STEP = 6
MOBA_GROUP = 4


RESIDENT_WEIGHT_BYTES = 24 * 1024 * 1024


def _proj_cols(rows, n):
    if rows * n * 2 <= RESIDENT_WEIGHT_BYTES:
        return n
    return n // 2 if n % (2 * MXU_WIDTH) == 0 else n


def _trunk(x, mem, mem_norm, layers, final_norm):
    batch, seq, d = x.shape
    mem_tokens = mem.shape[1]
    x = x.reshape(batch * seq, d)
    mem = mem.reshape(batch * mem_tokens, d)
    n_layers = len(layers)
    w_in = layers[0][1].astype(BF16)
    w_mem_kv = layers[0][3]
    for i, (norm_mix, _, extra, _, w_o, norm_ffn, w_up, cw, cb, w_down) in enumerate(layers):
        kind = i % 3
        mem_width = w_mem_kv.shape[1] // 2
        mix = d - mem_width
        own_kv = (w_mem_kv,) if i == 0 else ()
        proj, (w_up, w_o, *own_kv) = _norm_matmul(
            x, norm_mix, w_in, tm=PROJ_ROWS, tn=_proj_cols(d, w_in.shape[1]),
            casts=(w_up, w_o) + own_kv)
        if own_kv:
            w_mem_kv, = own_kv
        if kind == 0:
            tok = _retention(proj, extra, batch=batch, seq=seq, rows=seq, heads=RET_HEADS_PER_STEP)
        elif kind == 1:
            tok = None
        else:
            tok = _moba(proj, batch=batch, seq=seq, d=mix // MOBA_HEADS,
                        heads=MOBA_HEADS_PER_STEP, group=MOBA_GROUP)
        next_weights = (layers[i + 1][1], layers[i + 1][3]) if i + 1 < n_layers else ()
        x, (w_down, *next_weights) = _out_proj(tok, proj, extra, mem, mem_norm, w_mem_kv, x, w_o,
                                               seq=seq, mem_tokens=mem_tokens, tm=PROJ_ROWS,
                                               casts=(w_down,) + next_weights)
        if next_weights:
            w_in, w_mem_kv = next_weights
        x = _conv_ffn(x, norm_ffn, w_up, cw, cb, w_down, final_norm, seq=seq, tm=FFN_ROWS,
                      tf=FFN_COLS, final_norm=(i == n_layers - 1))
    return x.reshape(batch, seq, d)


def kernel(x, mem, mem_norm,
           l0_norm_mix, l0_w_in, l0_ret_gn, l0_w_mem_kv, l0_w_o, l0_norm_ffn, l0_ffn_w_up, l0_ffn_conv_w, l0_ffn_conv_b, l0_ffn_w_down,
           l1_norm_mix, l1_w_in, l1_conv_w, l1_w_mem_kv, l1_w_o, l1_norm_ffn, l1_ffn_w_up, l1_ffn_conv_w, l1_ffn_conv_b, l1_ffn_w_down,
           l2_norm_mix, l2_w_in, l2_w_mem_kv, l2_w_o, l2_norm_ffn, l2_ffn_w_up, l2_ffn_conv_w, l2_ffn_conv_b, l2_ffn_w_down,
           l3_norm_mix, l3_w_in, l3_ret_gn, l3_w_mem_kv, l3_w_o, l3_norm_ffn, l3_ffn_w_up, l3_ffn_conv_w, l3_ffn_conv_b, l3_ffn_w_down,
           final_norm):
    layers = [
        (l0_norm_mix, l0_w_in, l0_ret_gn, l0_w_mem_kv, l0_w_o, l0_norm_ffn, l0_ffn_w_up, l0_ffn_conv_w, l0_ffn_conv_b, l0_ffn_w_down),
        (l1_norm_mix, l1_w_in, l1_conv_w, l1_w_mem_kv, l1_w_o, l1_norm_ffn, l1_ffn_w_up, l1_ffn_conv_w, l1_ffn_conv_b, l1_ffn_w_down),
        (l2_norm_mix, l2_w_in, None, l2_w_mem_kv, l2_w_o, l2_norm_ffn, l2_ffn_w_up, l2_ffn_conv_w, l2_ffn_conv_b, l2_ffn_w_down),
        (l3_norm_mix, l3_w_in, l3_ret_gn, l3_w_mem_kv, l3_w_o, l3_norm_ffn, l3_ffn_w_up, l3_ffn_conv_w, l3_ffn_conv_b, l3_ffn_w_down),
    ]
    return _trunk(x, mem, mem_norm, layers, final_norm)
```
